```python
import math
import jax, jax.numpy as jnp
from jax import lax
import numpy as np

D_MODEL = 1024
BATCH = 8
SEQ = 4096
DEPTH = 1

MIX_WIDTH = D_MODEL
HEAD_DIM = 64
ATTN_WIDTH = MIX_WIDTH // 2
N_ATTN_HEADS = ATTN_WIDTH // HEAD_DIM
CONV_WIDTH = MIX_WIDTH - ATTN_WIDTH
CONV_GROUPS = 8
CONV_KERNEL = 31
MOBA_BLOCK = 256
MOBA_TOPK = 3
Q_CHUNK = 32
N_BUCKETS = 32
MAX_DISTANCE = 1024
FFN_HIDDEN = 2816
FFN_CONV = 3
EPS = 1e-6
NEG = -1e30

kernel_name = "hymba_moba_conformer_convffn_adaln"


def rmsnorm(x, g):
    xf = x.astype(jnp.float32)
    y = xf * lax.rsqrt(jnp.mean(xf * xf, axis=-1, keepdims=True) + EPS)
    return y.astype(x.dtype) * g


def group_norm(x, g, b, groups):
    B, S, C = x.shape
    xf = x.astype(jnp.float32).reshape(B, S, groups, C // groups)
    mu = jnp.mean(xf, axis=-1, keepdims=True)
    var = jnp.mean(jnp.square(xf - mu), axis=-1, keepdims=True)
    y = ((xf - mu) * lax.rsqrt(var + EPS)).reshape(B, S, C)
    return y.astype(x.dtype) * g + b


def causal_dwconv(x, w, b):
    K, C = w.shape
    y = lax.conv_general_dilated(
        x, w[:, None, :], window_strides=(1,), padding=[(K - 1, 0)],
        dimension_numbers=("NWC", "WIO", "NWC"), feature_group_count=C)
    return y + b


def t5_bucket(dist):
    n = jnp.maximum(dist, 0)
    max_exact = N_BUCKETS // 2
    nf = jnp.maximum(n, 1).astype(jnp.float32)
    large = max_exact + (jnp.log(nf / max_exact) / math.log(MAX_DISTANCE / max_exact)
                         * (N_BUCKETS - max_exact)).astype(jnp.int32)
    large = jnp.minimum(large, N_BUCKETS - 1)
    return jnp.where(n < max_exact, n, large)


def moba_attention(q, k, v, rel_bias):
    B, H, S, hd = q.shape
    s_pad = -(-S // MOBA_BLOCK) * MOBA_BLOCK
    pad = ((0, 0), (0, 0), (0, s_pad - S), (0, 0))
    k_p = jnp.pad(k, pad)
    v_p = jnp.pad(v, pad)
    nb = s_pad // MOBA_BLOCK
    topk = min(MOBA_TOPK, nb)
    kb = k_p.reshape(B, H, nb, MOBA_BLOCK, hd)
    vb = v_p.reshape(B, H, nb, MOBA_BLOCK, hd)
    kmean = jnp.mean(kb, axis=3)
    scale = hd ** -0.5
    b_idx = jnp.arange(B)[:, None, None, None]
    h_idx = jnp.arange(H)[None, :, None, None]
    h_idx5 = jnp.arange(H)[None, :, None, None, None]
    offs = jnp.arange(MOBA_BLOCK)
    n_chunks = S // Q_CHUNK

    def chunk(ci):
        q0 = ci * Q_CHUNK
        ib = q0 // MOBA_BLOCK
        qc = lax.dynamic_slice_in_dim(q, q0, Q_CHUNK, axis=2)
        qpos = q0 + jnp.arange(Q_CHUNK)
        gate = jnp.einsum("bhqd,bhnd->bhqn", qc, kmean).astype(jnp.float32)
        gate = jnp.where(jnp.arange(nb) < ib, gate, -jnp.inf)
        _, idx = lax.top_k(gate, topk)
        valid = jnp.arange(topk) < ib
        k_sel = kb[b_idx, h_idx, idx]
        v_sel = vb[b_idx, h_idx, idx]
        s_sel = jnp.einsum("bhqd,bhqnkd->bhqnk", qc, k_sel).astype(jnp.float32) * scale
        kpos_sel = idx[..., None] * MOBA_BLOCK + offs
        bucket_sel = t5_bucket(qpos[None, None, :, None, None] - kpos_sel)
        s_sel = s_sel + rel_bias[h_idx5, bucket_sel].astype(jnp.float32)
        s_sel = jnp.where(valid[:, None], s_sel, NEG)
        k_own = lax.dynamic_slice_in_dim(k_p, ib * MOBA_BLOCK, MOBA_BLOCK, axis=2)
        v_own = lax.dynamic_slice_in_dim(v_p, ib * MOBA_BLOCK, MOBA_BLOCK, axis=2)
        s_own = jnp.einsum("bhqd,bhkd->bhqk", qc, k_own).astype(jnp.float32) * scale
        d_own = qpos[:, None] - (ib * MOBA_BLOCK + offs)[None, :]
        s_own = s_own + rel_bias[:, t5_bucket(d_own)][None].astype(jnp.float32)
        s_own = jnp.where(d_own >= 0, s_own, NEG)
        s_all = jnp.concatenate(
            [s_sel.reshape(B, H, Q_CHUNK, topk * MOBA_BLOCK), s_own], axis=-1)
        p = jax.nn.softmax(s_all, axis=-1).astype(v.dtype)
        p_sel = p[..., :topk * MOBA_BLOCK].reshape(B, H, Q_CHUNK, topk, MOBA_BLOCK)
        p_own = p[..., topk * MOBA_BLOCK:]
        return (jnp.einsum("bhqnk,bhqnkd->bhqd", p_sel, v_sel)
                + jnp.einsum("bhqk,bhkd->bhqd", p_own, v_own))

    out = lax.map(chunk, jnp.arange(n_chunks))
    return out.transpose(1, 0, 3, 2, 4).reshape(B, S, H, hd)


def setup_inputs(seed: int = 0) -> dict:
    key = jax.random.key(seed)
    ks = jax.random.split(key, 20)
    D, F = D_MODEL, FFN_HIDDEN
    n_in = 3 * ATTN_WIDTH + 2 * CONV_WIDTH
    nrm = jax.random.normal
    f32 = jnp.float32
    return {
        "x": nrm(ks[0], (BATCH, SEQ, D), f32),
        "c": nrm(ks[1], (BATCH, D), f32),
        "rel_bias": 0.5 * nrm(ks[2], (N_ATTN_HEADS, N_BUCKETS), f32),
        "ada_w": 0.5 * D ** -0.5 * nrm(ks[3], (DEPTH, D, 6 * D), f32),
        "ada_b": 0.02 * nrm(ks[4], (DEPTH, 6 * D), f32),
        "norm1_g": 1.0 + 0.05 * nrm(ks[5], (DEPTH, D), f32),
        "w_in": D ** -0.5 * nrm(ks[6], (DEPTH, D, n_in), f32),
        "q_norm_g": 1.0 + 0.05 * nrm(ks[7], (DEPTH, HEAD_DIM), f32),
        "k_norm_g": 1.0 + 0.05 * nrm(ks[8], (DEPTH, HEAD_DIM), f32),
        "conv_dw_w": CONV_KERNEL ** -0.5 * nrm(ks[9], (DEPTH, CONV_KERNEL, CONV_WIDTH), f32),
        "conv_dw_b": 0.02 * nrm(ks[10], (DEPTH, CONV_WIDTH), f32),
        "conv_norm_g": 1.0 + 0.05 * nrm(ks[11], (DEPTH, CONV_WIDTH), f32),
        "conv_norm_b": 0.02 * nrm(ks[12], (DEPTH, CONV_WIDTH), f32),
        "w_out": MIX_WIDTH ** -0.5 * nrm(ks[13], (DEPTH, MIX_WIDTH, D), f32),
        "norm2_g": 1.0 + 0.05 * nrm(ks[14], (DEPTH, D), f32),
        "w_up": D ** -0.5 * nrm(ks[15], (DEPTH, D, 2 * F), f32),
        "ffn_dw_w": FFN_CONV ** -0.5 * nrm(ks[16], (DEPTH, FFN_CONV, 2 * F), f32),
        "ffn_dw_b": 0.02 * nrm(ks[17], (DEPTH, 2 * F), f32),
        "w_down": F ** -0.5 * nrm(ks[18], (DEPTH, F, D), f32),
    }


def reference(x, c, rel_bias, ada_w, ada_b, norm1_g, w_in, q_norm_g, k_norm_g,
              conv_dw_w, conv_dw_b, conv_norm_g, conv_norm_b, w_out, norm2_g,
              w_up, ffn_dw_w, ffn_dw_b, w_down):
    B, S, D = x.shape
    H, hd = N_ATTN_HEADS, HEAD_DIM
    sc = jax.nn.silu(c)
    for l in range(DEPTH):
        mod = sc @ ada_w[l] + ada_b[l]
        sh1, sc1, g1, sh2, sc2, g2 = [m[:, None, :] for m in jnp.split(mod, 6, axis=-1)]

        xn = rmsnorm(x, norm1_g[l]) * (1.0 + sc1) + sh1
        proj = xn @ w_in[l]
        q, k, v, ca, cg = jnp.split(
            proj, np.cumsum([ATTN_WIDTH] * 3 + [CONV_WIDTH]).tolist(), axis=-1)
        q = rmsnorm(q.reshape(B, S, H, hd), q_norm_g[l]).transpose(0, 2, 1, 3)
        k = rmsnorm(k.reshape(B, S, H, hd), k_norm_g[l]).transpose(0, 2, 1, 3)
        v = v.reshape(B, S, H, hd).transpose(0, 2, 1, 3)
        y_attn = moba_attention(q, k, v, rel_bias).reshape(B, S, ATTN_WIDTH)

        a = ca * jax.nn.sigmoid(cg)
        a = causal_dwconv(a, conv_dw_w[l], conv_dw_b[l])
        a = jax.nn.silu(group_norm(a, conv_norm_g[l], conv_norm_b[l], CONV_GROUPS))

        y_mix = jnp.concatenate([y_attn, a], axis=-1) @ w_out[l]
        x = x + g1 * y_mix

        xn2 = rmsnorm(x, norm2_g[l]) * (1.0 + sc2) + sh2
        h = causal_dwconv(xn2 @ w_up[l], ffn_dw_w[l], ffn_dw_b[l])
        u, g = jnp.split(h, 2, axis=-1)
        x = x + g2 * ((jax.nn.silu(g) * u) @ w_down[l])
    return x
```

```python
import functools
import math

import numpy as np
import jax
import jax.numpy as jnp
from jax import lax
from jax.experimental import pallas as pl
from jax.experimental.pallas import tpu as pltpu

F32 = jnp.float32
BF16 = jnp.bfloat16

D_MODEL = 1024
HEAD_DIM = 64
ATTN_WIDTH = 512
N_HEADS = ATTN_WIDTH // HEAD_DIM
CONV_WIDTH = 512
CONV_GROUPS = 8
CONV_KERNEL = 31
MOBA_BLOCK = 256
MOBA_TOPK = 3
N_BUCKETS = 32
MAX_DISTANCE = 1024
FFN_HIDDEN = 2816
FFN_CONV = 3
EPS = 1e-6
NEG = -1e30

N_NEAR = 5
PAIR = 2 * HEAD_DIM
HALO = 32
FFN_HALO = 8
FFN_CHUNK = 256
TM_IN = 512
TM_FFN = 512
VMEM_LIMIT = 56 * 1024 * 1024


def _sigmoid(x):
    return 1.0 / (1.0 + jnp.exp(-x))


def _dot(a, b):
    return jnp.dot(a, b, preferred_element_type=F32)


def _dot_nt(a, b):
    return lax.dot_general(a, b, (((1,), (1,)), ((), ())), preferred_element_type=F32)


def _resident(shape):
    zeros = (0,) * len(shape)
    return pl.BlockSpec(shape, lambda *_: zeros, pipeline_mode=pl.Buffered(1))


def _mod_kernel(c_ref, w_ref, b_ref, o_ref):
    c = c_ref[...]
    sc = c * _sigmoid(c)
    o_ref[...] = _dot(sc.astype(BF16), w_ref[...].astype(BF16)) + b_ref[...]


def _modulation(c, ada_w, ada_b):
    bsz, d = c.shape
    n = ada_w.shape[1]
    tn = 1536
    return pl.pallas_call(
        _mod_kernel,
        grid=(n // tn,),
        in_specs=[pl.BlockSpec((bsz, d), lambda j: (0, 0)),
                  pl.BlockSpec((d, tn), lambda j: (0, j)),
                  pl.BlockSpec((1, tn), lambda j: (0, j))],
        out_specs=pl.BlockSpec((bsz, tn), lambda j: (0, j)),
        out_shape=jax.ShapeDtypeStruct((bsz, n), F32),
        compiler_params=pltpu.CompilerParams(dimension_semantics=("arbitrary",),
                                             vmem_limit_bytes=VMEM_LIMIT),
        name="adaln_mod",
    )(c, ada_w, ada_b.reshape(1, n))


def _bias_kernel(rb_ref, o_ref):
    h = pl.program_id(0)
    kk = lax.broadcasted_iota(jnp.int32, (MOBA_BLOCK, MOBA_BLOCK), 0)
    qq = lax.broadcasted_iota(jnp.int32, (MOBA_BLOCK, MOBA_BLOCK), 1)
    max_exact = N_BUCKETS // 2
    for delta in range(N_NEAR):
        dist = delta * MOBA_BLOCK + qq - kk
        n = jnp.maximum(dist, 0)
        nf = jnp.maximum(n, 1).astype(F32)
        large = max_exact + (jnp.log(nf / max_exact) / math.log(MAX_DISTANCE / max_exact)
                             * (N_BUCKETS - max_exact)).astype(jnp.int32)
        large = jnp.minimum(large, N_BUCKETS - 1)
        bucket = jnp.where(n < max_exact, n, large)
        bias = jnp.zeros((MOBA_BLOCK, MOBA_BLOCK), F32)
        for b in range(N_BUCKETS):
            bias = jnp.where(bucket == b, rb_ref[h, b], bias)
        if delta == 0:
            bias = jnp.where(dist >= 0, bias, NEG)
        o_ref[0, delta] = bias


def _bias_tiles(rel_bias):
    return pl.pallas_call(
        _bias_kernel,
        grid=(N_HEADS,),
        in_specs=[pl.BlockSpec(memory_space=pltpu.SMEM)],
        out_specs=pl.BlockSpec((1, N_NEAR, MOBA_BLOCK, MOBA_BLOCK), lambda h: (h, 0, 0, 0)),
        out_shape=jax.ShapeDtypeStruct((N_HEADS, N_NEAR, MOBA_BLOCK, MOBA_BLOCK), F32),
        compiler_params=pltpu.CompilerParams(dimension_semantics=("arbitrary",)),
        name="rel_bias_tiles",
    )(rel_bias)


def _group_mean(v, g_ref):
    vb = v.astype(BF16)
    g = g_ref[...]
    half = g.shape[0]
    parts = [_dot(vb[:, s:s + half], g) for s in range(0, v.shape[1], half)]
    return jnp.concatenate(parts, axis=-1)


def _in_kernel(x_ref, mod_ref, n1g_ref, w_ref, qg_ref, kg_ref, g_ref, cw_ref, cb_ref, gng_ref, gnb_ref,
               q_ref, k_ref, v_ref, km_ref, a_ref, abuf, ybuf):
    i = pl.program_id(1)
    tm = x_ref.shape[1]
    x = x_ref[0]
    ms = jnp.mean(x * x, axis=-1, keepdims=True)
    xn = x * lax.rsqrt(ms + EPS) * n1g_ref[...]
    xn = xn * (1.0 + mod_ref[0, 1:2, :]) + mod_ref[0, 0:1, :]
    xb = xn.astype(BF16)

    def proj(c):
        return _dot(xb, w_ref[:, c * ATTN_WIDTH:(c + 1) * ATTN_WIDTH])

    q = proj(0)
    qn = q * lax.rsqrt(_group_mean(q * q, g_ref) + EPS) * qg_ref[...]
    q_ref[0] = (qn * (HEAD_DIM ** -0.5)).astype(BF16)
    k = proj(1)
    kn = k * lax.rsqrt(_group_mean(k * k, g_ref) + EPS) * kg_ref[...]
    k_ref[0] = kn.astype(BF16)
    km_ref[0, 0] = jnp.concatenate(
        [jnp.mean(kn[t * MOBA_BLOCK:(t + 1) * MOBA_BLOCK], axis=0, keepdims=True)
         for t in range(tm // MOBA_BLOCK)], axis=0)
    v_ref[0] = proj(2).astype(BF16)

    glu = proj(3) * _sigmoid(proj(4))

    @pl.when(i == 0)
    def _():
        abuf[0:HALO, :] = jnp.zeros((HALO, CONV_WIDTH), F32)

    abuf[HALO:HALO + tm, :] = glu
    rc, lc = 64, 256
    base = HALO - (CONV_KERNEL - 1)
    for r in range(tm // rc):
        for l in range(CONV_WIDTH // lc):
            acc = jnp.broadcast_to(cb_ref[:, l * lc:(l + 1) * lc], (rc, lc))
            for j in range(CONV_KERNEL):
                acc = acc + cw_ref[j:j + 1, l * lc:(l + 1) * lc] * abuf[pl.ds(r * rc + base + j, rc), l * lc:(l + 1) * lc]
            ybuf[r * rc:(r + 1) * rc, l * lc:(l + 1) * lc] = acc
    abuf[0:HALO, :] = abuf[tm:tm + HALO, :]

    y = ybuf[...]
    yc = y - _group_mean(y, g_ref)
    var = _group_mean(yc * yc, g_ref)
    yn = yc * lax.rsqrt(var + EPS) * gng_ref[...] + gnb_ref[...]
    a_ref[0] = (yn * _sigmoid(yn)).astype(BF16)


def _in_proj(x, mod, n1g, w_in, qg, kg, gmat, cw, cb, gng, gnb):
    bsz, s, d = x.shape
    tm = TM_IN
    nt = s // tm
    nblk = tm // MOBA_BLOCK
    tok = lambda w: pl.BlockSpec((1, tm, w), lambda b, i: (b, i, 0))
    seq = lambda dt: jax.ShapeDtypeStruct((bsz, s, ATTN_WIDTH), dt)
    return pl.pallas_call(
        _in_kernel,
        grid=(bsz, nt),
        in_specs=[tok(d),
                  pl.BlockSpec((1, 6, d), lambda b, i: (b, 0, 0)),
                  _resident((1, d)),
                  _resident(w_in.shape),
                  _resident((1, ATTN_WIDTH)), _resident((1, ATTN_WIDTH)),
                  _resident(gmat.shape),
                  _resident(cw.shape), _resident((1, CONV_WIDTH)),
                  _resident((1, CONV_WIDTH)), _resident((1, CONV_WIDTH))],
        out_specs=[tok(ATTN_WIDTH), tok(ATTN_WIDTH), tok(ATTN_WIDTH),
                   pl.BlockSpec((1, 1, nblk, ATTN_WIDTH), lambda b, i: (b, i, 0, 0)),
                   tok(CONV_WIDTH)],
        out_shape=[seq(BF16), seq(BF16), seq(BF16),
                   jax.ShapeDtypeStruct((bsz, nt, nblk, ATTN_WIDTH), F32),
                   seq(BF16)],
        scratch_shapes=[pltpu.VMEM((tm + HALO, CONV_WIDTH), F32),
                        pltpu.VMEM((tm, CONV_WIDTH), F32)],
        compiler_params=pltpu.CompilerParams(dimension_semantics=("arbitrary", "arbitrary"),
                                             vmem_limit_bytes=VMEM_LIMIT),
        name="in_proj_conv",
    )(x, mod, n1g, w_in, qg, kg, gmat, cw, cb, gng, gnb)


def _attn_kernel(rb_ref, q_ref, k_ref, v_ref, km_ref, bias_ref, o_ref,
                 vt_ref, selb_ref, m_ref, l_ref, acc_ref):
    hp = pl.program_id(0)
    s = q_ref.shape[1]
    nb = s // MOBA_BLOCK
    blk = MOBA_BLOCK

    for j in range(nb):
        vt = v_ref[0, j * blk:(j + 1) * blk, :].astype(F32).T.astype(BF16)
        for h in range(2):
            vt_ref[h, j] = vt[h * HEAD_DIM:(h + 1) * HEAD_DIM, :]

    km = km_ref[0].astype(BF16)
    lane = lax.broadcasted_iota(jnp.int32, (blk, PAIR), 1)
    rows = lax.broadcasted_iota(jnp.int32, (nb, blk), 0).astype(F32)

    def update(h, qm, j, extra, first=False):
        kb = k_ref[0, pl.ds(pl.multiple_of(j * blk, blk), blk), :]
        st = _dot_nt(kb, qm) + extra
        m_blk = jnp.max(st, axis=0, keepdims=True)
        if first:
            m_new = m_blk
        else:
            m_old = m_ref[h]
            m_new = jnp.maximum(m_old, m_blk)
            alpha = jnp.exp(m_old - m_new)
        p = jnp.exp(st - m_new)
        psum = jnp.sum(p, axis=0, keepdims=True)
        pv = _dot(vt_ref[h, j], p.astype(BF16))
        if first:
            l_ref[h] = psum
            acc_ref[h] = pv
        else:
            l_ref[h] = alpha * l_ref[h] + psum
            acc_ref[h] = alpha * acc_ref[h] + pv
        m_ref[h] = m_new

    def q_tile(qi, carry):
        q0 = pl.multiple_of(qi * blk, blk)
        qpair = q_ref[0, pl.ds(q0, blk), :]
        qms = []
        for h in range(2):
            qm = jnp.where((lane >= HEAD_DIM) == (h == 1), qpair, jnp.zeros_like(qpair))
            qms.append(qm)
            gate = _dot_nt(km, qm)
            gate = jnp.where(rows < qi.astype(F32), gate, -jnp.inf)
            sel = jnp.zeros(gate.shape, jnp.bool_)
            for r in range(MOBA_TOPK):
                mx = jnp.max(gate, axis=0, keepdims=True)
                idx = jnp.min(jnp.where(gate == mx, rows, float(nb)), axis=0, keepdims=True)
                pick = (rows == idx) & (mx > -jnp.inf)
                sel = sel | pick
                gate = jnp.where(pick, -jnp.inf, gate)
            selb_ref[h] = jnp.where(sel, 0.0, NEG)

        for h in range(2):
            update(h, qms[h], qi, bias_ref[h, 0], first=True)
        for delta in range(1, N_NEAR):
            @pl.when(qi >= delta)
            def _():
                j = qi - delta
                for h in range(2):
                    update(h, qms[h], j, bias_ref[h, delta] + selb_ref[h, pl.ds(j, 1), :])

        def far(j, c):
            for h in range(2):
                update(h, qms[h], j, selb_ref[h, pl.ds(j, 1), :] + rb_ref[2 * hp + h, N_BUCKETS - 1])
            return c

        lax.fori_loop(0, jnp.maximum(qi - (N_NEAR - 1), 0), far, 0)

        out_t = jnp.concatenate([acc_ref[h] / l_ref[h] for h in range(2)], axis=0)
        o_ref[0, pl.ds(q0, blk), :] = out_t.T.astype(BF16)
        return carry

    lax.fori_loop(0, nb, q_tile, 0)


def _attention(rel_bias, q, k, v, kmean, bias):
    bsz, s, _ = q.shape
    nb = s // MOBA_BLOCK
    npair = N_HEADS // 2
    seq = pl.BlockSpec((1, s, PAIR), lambda hp, b: (b, 0, hp))
    return pl.pallas_call(
        _attn_kernel,
        grid=(npair, bsz),
        in_specs=[pl.BlockSpec(memory_space=pltpu.SMEM),
                  seq, seq, seq,
                  pl.BlockSpec((1, nb, PAIR), lambda hp, b: (b, 0, hp)),
                  pl.BlockSpec((2, N_NEAR, MOBA_BLOCK, MOBA_BLOCK), lambda hp, b: (hp, 0, 0, 0))],
        out_specs=seq,
        out_shape=jax.ShapeDtypeStruct((bsz, s, ATTN_WIDTH), BF16),
        scratch_shapes=[pltpu.VMEM((2, nb, HEAD_DIM, MOBA_BLOCK), BF16),
                        pltpu.VMEM((2, nb, MOBA_BLOCK), F32),
                        pltpu.VMEM((2, 1, MOBA_BLOCK), F32),
                        pltpu.VMEM((2, 1, MOBA_BLOCK), F32),
                        pltpu.VMEM((2, HEAD_DIM, MOBA_BLOCK), F32)],
        compiler_params=pltpu.CompilerParams(dimension_semantics=("arbitrary", "arbitrary"),
                                             vmem_limit_bytes=VMEM_LIMIT),
        name="moba_attention",
    )(rel_bias, q, k, v, kmean, bias)


def _ffn_kernel(x_ref, ya_ref, a_ref, mod_ref, wo_ref, n2g_ref, wu_ref, fw_ref, fb_ref, wd_ref,
                o_ref, carry_ref, hbuf):
    i = pl.program_id(1)
    tm = x_ref.shape[1]
    g1 = mod_ref[0, 2:3, :]
    sh2 = mod_ref[0, 3:4, :]
    sc2 = mod_ref[0, 4:5, :]
    g2 = mod_ref[0, 5:6, :]

    ycat = jnp.concatenate([ya_ref[0], a_ref[0]], axis=-1)
    x1 = x_ref[0] + g1 * _dot(ycat, wo_ref[...])
    ms = jnp.mean(x1 * x1, axis=-1, keepdims=True)
    xn = x1 * lax.rsqrt(ms + EPS) * n2g_ref[...]
    xb = (xn * (1.0 + sc2) + sh2).astype(BF16)

    @pl.when(i == 0)
    def _():
        carry_ref[...] = jnp.zeros(carry_ref.shape, F32)

    def conv_cols(c0):
        h = _dot(xb, wu_ref[:, c0:c0 + FFN_CHUNK])
        hbuf[0:FFN_HALO, :] = carry_ref[:, c0:c0 + FFN_CHUNK]
        hbuf[FFN_HALO:FFN_HALO + tm, :] = h
        carry_ref[:, c0:c0 + FFN_CHUNK] = h[tm - FFN_HALO:tm, :]
        out = fb_ref[:, c0:c0 + FFN_CHUNK] + fw_ref[FFN_CONV - 1:FFN_CONV, c0:c0 + FFN_CHUNK] * h
        for t in range(1, FFN_CONV):
            out = out + (fw_ref[FFN_CONV - 1 - t:FFN_CONV - t, c0:c0 + FFN_CHUNK]
                         * hbuf[pl.ds(FFN_HALO - t, tm), :])
        return out

    acc = jnp.zeros((tm, D_MODEL), F32)
    for c in range(FFN_HIDDEN // FFN_CHUNK):
        u = conv_cols(c * FFN_CHUNK)
        g = conv_cols(FFN_HIDDEN + c * FFN_CHUNK)
        act = (g * _sigmoid(g) * u).astype(BF16)
        acc = acc + _dot(act, wd_ref[c * FFN_CHUNK:(c + 1) * FFN_CHUNK, :])
    o_ref[0] = x1 + g2 * acc


def _out_ffn(x, y_attn, a, mod, w_out, n2g, w_up, fw, fb, w_down):
    bsz, s, d = x.shape
    tm = TM_FFN
    tok = lambda w: pl.BlockSpec((1, tm, w), lambda b, i: (b, i, 0))
    return pl.pallas_call(
        _ffn_kernel,
        grid=(bsz, s // tm),
        in_specs=[tok(d), tok(ATTN_WIDTH), tok(CONV_WIDTH),
                  pl.BlockSpec((1, 6, d), lambda b, i: (b, 0, 0)),
                  _resident(w_out.shape), _resident((1, d)),
                  _resident(w_up.shape), _resident(fw.shape), _resident(fb.shape),
                  _resident(w_down.shape)],
        out_specs=tok(d),
        out_shape=jax.ShapeDtypeStruct((bsz, s, d), F32),
        scratch_shapes=[pltpu.VMEM((FFN_HALO, 2 * FFN_HIDDEN), F32),
                        pltpu.VMEM((tm + FFN_HALO, FFN_CHUNK), F32)],
        compiler_params=pltpu.CompilerParams(dimension_semantics=("arbitrary", "arbitrary"),
                                             vmem_limit_bytes=VMEM_LIMIT),
        name="out_proj_ffn",
    )(x, y_attn, a, mod, w_out, n2g, w_up, fw, fb, w_down)


def _group_avg_matrix():
    half = 256
    idx = np.arange(half) // (CONV_WIDTH // CONV_GROUPS)
    return jnp.asarray((idx[:, None] == idx[None, :]).astype(np.float32) / HEAD_DIM, dtype=BF16)


def kernel(x, c, rel_bias, ada_w, ada_b, norm1_g, w_in, q_norm_g, k_norm_g, conv_dw_w, conv_dw_b,
           conv_norm_g, conv_norm_b, w_out, norm2_g, w_up, ffn_dw_w, ffn_dw_b, w_down):
    bsz, s, d = x.shape
    depth = ada_w.shape[0]
    bias = _bias_tiles(rel_bias)
    gmat = _group_avg_matrix()
    row = lambda t: t.reshape(1, -1)
    for l in range(depth):
        mod = _modulation(c, ada_w[l], ada_b[l]).reshape(bsz, 6, d)
        q, k, v, kmean, a = _in_proj(
            x, mod, row(norm1_g[l]), w_in[l].astype(BF16),
            row(jnp.tile(q_norm_g[l], N_HEADS)), row(jnp.tile(k_norm_g[l], N_HEADS)), gmat,
            conv_dw_w[l], row(conv_dw_b[l]), row(conv_norm_g[l]), row(conv_norm_b[l]))
        kmean = kmean.reshape(bsz, s // MOBA_BLOCK, ATTN_WIDTH)
        y_attn = _attention(rel_bias, q, k, v, kmean, bias)
        x = _out_ffn(x, y_attn, a, mod, w_out[l].astype(BF16), row(norm2_g[l]), w_up[l].astype(BF16),
                     ffn_dw_w[l], row(ffn_dw_b[l]), w_down[l].astype(BF16))
    return x
```

```python
import functools
import math

import numpy as np
import jax
import jax.numpy as jnp
from jax import lax
from jax.experimental import pallas as pl
from jax.experimental.pallas import tpu as pltpu

F32 = jnp.float32
BF16 = jnp.bfloat16

D_MODEL = 1024
HEAD_DIM = 64
ATTN_WIDTH = 512
N_HEADS = ATTN_WIDTH // HEAD_DIM
CONV_WIDTH = 512
CONV_GROUPS = 8
CONV_KERNEL = 31
MOBA_BLOCK = 256
MOBA_TOPK = 3
N_BUCKETS = 32
MAX_DISTANCE = 1024
FFN_HIDDEN = 2816
FFN_CONV = 3
EPS = 1e-6
NEG = -1e30

N_NEAR = 5
PAIR = 2 * HEAD_DIM
HALO = 32
FFN_HALO = 8
FFN_CHUNK = 256
TM_IN = 512
TM_FFN = 512
VMEM_LIMIT = 56 * 1024 * 1024


def _sigmoid(x):
    return 1.0 / (1.0 + jnp.exp(-x))


def _dot(a, b):
    return jnp.dot(a, b, preferred_element_type=F32)


def _dot_nt(a, b):
    return lax.dot_general(a, b, (((1,), (1,)), ((), ())), preferred_element_type=F32)


def _resident(shape):
    zeros = (0,) * len(shape)
    return pl.BlockSpec(shape, lambda *_: zeros, pipeline_mode=pl.Buffered(1))


def _mod_kernel(c_ref, w_ref, b_ref, o_ref):
    c = c_ref[...]
    sc = c * _sigmoid(c)
    o_ref[...] = _dot(sc.astype(BF16), w_ref[...].astype(BF16)) + b_ref[...]


def _modulation(c, ada_w, ada_b):
    bsz, d = c.shape
    n = ada_w.shape[1]
    tn = 1536
    return pl.pallas_call(
        _mod_kernel,
        grid=(n // tn,),
        in_specs=[pl.BlockSpec((bsz, d), lambda j: (0, 0)),
                  pl.BlockSpec((d, tn), lambda j: (0, j)),
                  pl.BlockSpec((1, tn), lambda j: (0, j))],
        out_specs=pl.BlockSpec((bsz, tn), lambda j: (0, j)),
        out_shape=jax.ShapeDtypeStruct((bsz, n), F32),
        compiler_params=pltpu.CompilerParams(dimension_semantics=("arbitrary",),
                                             vmem_limit_bytes=VMEM_LIMIT),
        name="adaln_mod",
    )(c, ada_w, ada_b.reshape(1, n))


def _bias_kernel(rb_ref, o_ref):
    h = pl.program_id(0)
    kk = lax.broadcasted_iota(jnp.int32, (MOBA_BLOCK, MOBA_BLOCK), 0)
    qq = lax.broadcasted_iota(jnp.int32, (MOBA_BLOCK, MOBA_BLOCK), 1)
    max_exact = N_BUCKETS // 2
    for delta in range(N_NEAR):
        dist = delta * MOBA_BLOCK + qq - kk
        n = jnp.maximum(dist, 0)
        nf = jnp.maximum(n, 1).astype(F32)
        large = max_exact + (jnp.log(nf / max_exact) / math.log(MAX_DISTANCE / max_exact)
                             * (N_BUCKETS - max_exact)).astype(jnp.int32)
        large = jnp.minimum(large, N_BUCKETS - 1)
        bucket = jnp.where(n < max_exact, n, large)
        bias = jnp.zeros((MOBA_BLOCK, MOBA_BLOCK), F32)
        for b in range(N_BUCKETS):
            bias = jnp.where(bucket == b, rb_ref[h, b], bias)
        if delta == 0:
            bias = jnp.where(dist >= 0, bias, NEG)
        o_ref[0, delta] = bias


def _bias_tiles(rel_bias):
    return pl.pallas_call(
        _bias_kernel,
        grid=(N_HEADS,),
        in_specs=[pl.BlockSpec(memory_space=pltpu.SMEM)],
        out_specs=pl.BlockSpec((1, N_NEAR, MOBA_BLOCK, MOBA_BLOCK), lambda h: (h, 0, 0, 0)),
        out_shape=jax.ShapeDtypeStruct((N_HEADS, N_NEAR, MOBA_BLOCK, MOBA_BLOCK), F32),
        compiler_params=pltpu.CompilerParams(dimension_semantics=("arbitrary",)),
        name="rel_bias_tiles",
    )(rel_bias)


def _group_mean(v, g_ref):
    vb = v.astype(BF16)
    g = g_ref[...]
    half = g.shape[0]
    parts = [_dot(vb[:, s:s + half], g) for s in range(0, v.shape[1], half)]
    return jnp.concatenate(parts, axis=-1)


def _in_kernel(x_ref, mod_ref, n1g_ref, w_ref, qg_ref, kg_ref, g_ref, cw_ref, cb_ref, gng_ref, gnb_ref,
               q_ref, k_ref, v_ref, km_ref, a_ref, abuf, ybuf):
    i = pl.program_id(1)
    tm = x_ref.shape[1]
    x = x_ref[0]
    ms = jnp.mean(x * x, axis=-1, keepdims=True)
    xn = x * lax.rsqrt(ms + EPS) * n1g_ref[...]
    xn = xn * (1.0 + mod_ref[0, 1:2, :]) + mod_ref[0, 0:1, :]
    xb = xn.astype(BF16)

    def proj(c):
        return _dot(xb, w_ref[:, c * ATTN_WIDTH:(c + 1) * ATTN_WIDTH])

    q = proj(0)
    qn = q * lax.rsqrt(_group_mean(q * q, g_ref) + EPS) * qg_ref[...]
    q_ref[0] = (qn * (HEAD_DIM ** -0.5)).astype(BF16)
    k = proj(1)
    kn = k * lax.rsqrt(_group_mean(k * k, g_ref) + EPS) * kg_ref[...]
    k_ref[0] = kn.astype(BF16)
    km_ref[0, 0] = jnp.concatenate(
        [jnp.mean(kn[t * MOBA_BLOCK:(t + 1) * MOBA_BLOCK], axis=0, keepdims=True)
         for t in range(tm // MOBA_BLOCK)], axis=0)
    v_ref[0] = proj(2).astype(BF16)

    glu = proj(3) * _sigmoid(proj(4))

    @pl.when(i == 0)
    def _():
        abuf[0:HALO, :] = jnp.zeros((HALO, CONV_WIDTH), F32)

    abuf[HALO:HALO + tm, :] = glu
    rc, lc = 64, 256
    base = HALO - (CONV_KERNEL - 1)
    for r in range(tm // rc):
        for l in range(CONV_WIDTH // lc):
            acc = jnp.broadcast_to(cb_ref[:, l * lc:(l + 1) * lc], (rc, lc))
            for j in range(CONV_KERNEL):
                acc = acc + cw_ref[j:j + 1, l * lc:(l + 1) * lc] * abuf[pl.ds(r * rc + base + j, rc), l * lc:(l + 1) * lc]
            ybuf[r * rc:(r + 1) * rc, l * lc:(l + 1) * lc] = acc
    abuf[0:HALO, :] = abuf[tm:tm + HALO, :]

    y = ybuf[...]
    yc = y - _group_mean(y, g_ref)
    var = _group_mean(yc * yc, g_ref)
    yn = yc * lax.rsqrt(var + EPS) * gng_ref[...] + gnb_ref[...]
    a_ref[0] = (yn * _sigmoid(yn)).astype(BF16)


def _in_proj(x, mod, n1g, w_in, qg, kg, gmat, cw, cb, gng, gnb):
    bsz, s, d = x.shape
    tm = TM_IN
    nt = s // tm
    nblk = tm // MOBA_BLOCK
    tok = lambda w: pl.BlockSpec((1, tm, w), lambda b, i: (b, i, 0))
    seq = lambda dt: jax.ShapeDtypeStruct((bsz, s, ATTN_WIDTH), dt)
    return pl.pallas_call(
        _in_kernel,
        grid=(bsz, nt),
        in_specs=[tok(d),
                  pl.BlockSpec((1, 6, d), lambda b, i: (b, 0, 0)),
                  _resident((1, d)),
                  _resident(w_in.shape),
                  _resident((1, ATTN_WIDTH)), _resident((1, ATTN_WIDTH)),
                  _resident(gmat.shape),
                  _resident(cw.shape), _resident((1, CONV_WIDTH)),
                  _resident((1, CONV_WIDTH)), _resident((1, CONV_WIDTH))],
        out_specs=[tok(ATTN_WIDTH), tok(ATTN_WIDTH), tok(ATTN_WIDTH),
                   pl.BlockSpec((1, 1, nblk, ATTN_WIDTH), lambda b, i: (b, i, 0, 0)),
                   tok(CONV_WIDTH)],
        out_shape=[seq(BF16), seq(BF16), seq(BF16),
                   jax.ShapeDtypeStruct((bsz, nt, nblk, ATTN_WIDTH), F32),
                   seq(BF16)],
        scratch_shapes=[pltpu.VMEM((tm + HALO, CONV_WIDTH), F32),
                        pltpu.VMEM((tm, CONV_WIDTH), F32)],
        compiler_params=pltpu.CompilerParams(dimension_semantics=("arbitrary", "arbitrary"),
                                             vmem_limit_bytes=VMEM_LIMIT),
        name="in_proj_conv",
    )(x, mod, n1g, w_in, qg, kg, gmat, cw, cb, gng, gnb)


def _attn_kernel(rb_ref, q_ref, k_ref, v_ref, km_ref, bias_ref, o_ref, vt_ref, s_ref, selb_ref):
    hp = pl.program_id(0)
    s = q_ref.shape[1]
    nb = s // MOBA_BLOCK
    blk = MOBA_BLOCK

    for j in range(nb):
        vt = v_ref[0, j * blk:(j + 1) * blk, :].astype(F32).T.astype(BF16)
        for h in range(2):
            vt_ref[h, j] = vt[h * HEAD_DIM:(h + 1) * HEAD_DIM, :]

    km = km_ref[0].astype(BF16)
    lane = lax.broadcasted_iota(jnp.int32, (blk, PAIR), 1)
    rows = lax.broadcasted_iota(jnp.int32, (nb, blk), 0).astype(F32)
    far_bias = [rb_ref[2 * hp + h, N_BUCKETS - 1] for h in range(2)]

    for qi in range(nb):
        q0 = qi * blk
        qpair = q_ref[0, q0:q0 + blk, :]
        qms = [jnp.where((lane >= HEAD_DIM) == (h == 1), qpair, jnp.zeros_like(qpair)) for h in range(2)]
        gated = qi > MOBA_TOPK
        if gated:
            for h in range(2):
                gate = jnp.where(rows < float(qi), _dot_nt(km, qms[h]), -jnp.inf)
                sel = jnp.zeros(gate.shape, jnp.bool_)
                for _ in range(MOBA_TOPK):
                    mx = jnp.max(gate, axis=0, keepdims=True)
                    idx = jnp.min(jnp.where(gate == mx, rows, float(nb)), axis=0, keepdims=True)
                    pick = rows == idx
                    sel = sel | pick
                    gate = jnp.where(pick, -jnp.inf, gate)
                selb_ref[h] = jnp.where(sel, 0.0, NEG)

        def scores(h, j, extra):
            j0 = j * blk if isinstance(j, int) else pl.multiple_of(j * blk, blk)
            st = _dot_nt(k_ref[0, pl.ds(j0, blk), :], qms[h]) + extra
            s_ref[qi % 2, h, j] = st
            return jnp.max(st, axis=0, keepdims=True)

        m = [None, None]
        for delta in range(min(N_NEAR, qi + 1)):
            j = qi - delta
            for h in range(2):
                extra = bias_ref[h, delta]
                if gated and delta > 0:
                    extra = extra + selb_ref[h, j:j + 1, :]
                mb = scores(h, j, extra)
                m[h] = mb if m[h] is None else jnp.maximum(m[h], mb)
        n_far = qi + 1 - N_NEAR
        if n_far > 0:
            def far(j, ms):
                return tuple(jnp.maximum(ms[h], scores(h, j, selb_ref[h, pl.ds(j, 1), :] + far_bias[h]))
                             for h in range(2))
            m = list(lax.fori_loop(0, n_far, far, tuple(m), unroll=4))

        def accumulate(j, carry):
            out = []
            for h in range(2):
                l, acc = carry[h]
                p = jnp.exp(s_ref[qi % 2, h, j] - m[h])
                out.append((l + jnp.sum(p, axis=0, keepdims=True),
                            acc + _dot(vt_ref[h, j], p.astype(BF16))))
            return tuple(out)

        init = tuple((jnp.zeros((1, blk), F32), jnp.zeros((HEAD_DIM, blk), F32)) for _ in range(2))
        res = lax.fori_loop(0, qi + 1, accumulate, init, unroll=4)
        out_t = jnp.concatenate([acc / l for (l, acc) in res], axis=0)
        o_ref[0, q0:q0 + blk, :] = out_t.T.astype(BF16)


def _attention(rel_bias, q, k, v, kmean, bias):
    bsz, s, _ = q.shape
    nb = s // MOBA_BLOCK
    npair = N_HEADS // 2
    seq = pl.BlockSpec((1, s, PAIR), lambda hp, b: (b, 0, hp))
    return pl.pallas_call(
        _attn_kernel,
        grid=(npair, bsz),
        in_specs=[pl.BlockSpec(memory_space=pltpu.SMEM),
                  seq, seq, seq,
                  pl.BlockSpec((1, nb, PAIR), lambda hp, b: (b, 0, hp)),
                  pl.BlockSpec((2, N_NEAR, MOBA_BLOCK, MOBA_BLOCK), lambda hp, b: (hp, 0, 0, 0))],
        out_specs=seq,
        out_shape=jax.ShapeDtypeStruct((bsz, s, ATTN_WIDTH), BF16),
        scratch_shapes=[pltpu.VMEM((2, nb, HEAD_DIM, MOBA_BLOCK), BF16),
                        pltpu.VMEM((2, 2, nb, MOBA_BLOCK, MOBA_BLOCK), F32),
                        pltpu.VMEM((2, nb, MOBA_BLOCK), F32)],
        compiler_params=pltpu.CompilerParams(dimension_semantics=("arbitrary", "arbitrary"),
                                             vmem_limit_bytes=VMEM_LIMIT),
        name="moba_attention",
    )(rel_bias, q, k, v, kmean, bias)


def _ffn_kernel(x_ref, ya_ref, a_ref, mod_ref, wo_ref, n2g_ref, wu_ref, fw_ref, fb_ref, wd_ref,
                o_ref, carry_ref, hbuf):
    i = pl.program_id(1)
    tm = x_ref.shape[1]
    g1 = mod_ref[0, 2:3, :]
    sh2 = mod_ref[0, 3:4, :]
    sc2 = mod_ref[0, 4:5, :]
    g2 = mod_ref[0, 5:6, :]

    ycat = jnp.concatenate([ya_ref[0], a_ref[0]], axis=-1)
    x1 = x_ref[0] + g1 * _dot(ycat, wo_ref[...])
    ms = jnp.mean(x1 * x1, axis=-1, keepdims=True)
    xn = x1 * lax.rsqrt(ms + EPS) * n2g_ref[...]
    xb = (xn * (1.0 + sc2) + sh2).astype(BF16)

    @pl.when(i == 0)
    def _():
        carry_ref[...] = jnp.zeros(carry_ref.shape, F32)

    def conv_cols(c0):
        h = _dot(xb, wu_ref[:, c0:c0 + FFN_CHUNK])
        hbuf[0:FFN_HALO, :] = carry_ref[:, c0:c0 + FFN_CHUNK]
        hbuf[FFN_HALO:FFN_HALO + tm, :] = h
        carry_ref[:, c0:c0 + FFN_CHUNK] = h[tm - FFN_HALO:tm, :]
        out = fb_ref[:, c0:c0 + FFN_CHUNK] + fw_ref[FFN_CONV - 1:FFN_CONV, c0:c0 + FFN_CHUNK] * h
        for t in range(1, FFN_CONV):
            out = out + (fw_ref[FFN_CONV - 1 - t:FFN_CONV - t, c0:c0 + FFN_CHUNK]
                         * hbuf[pl.ds(FFN_HALO - t, tm), :])
        return out

    acc = jnp.zeros((tm, D_MODEL), F32)
    for c in range(FFN_HIDDEN // FFN_CHUNK):
        u = conv_cols(c * FFN_CHUNK)
        g = conv_cols(FFN_HIDDEN + c * FFN_CHUNK)
        act = (g * _sigmoid(g) * u).astype(BF16)
        acc = acc + _dot(act, wd_ref[c * FFN_CHUNK:(c + 1) * FFN_CHUNK, :])
    o_ref[0] = x1 + g2 * acc


def _out_ffn(x, y_attn, a, mod, w_out, n2g, w_up, fw, fb, w_down):
    bsz, s, d = x.shape
    tm = TM_FFN
    tok = lambda w: pl.BlockSpec((1, tm, w), lambda b, i: (b, i, 0))
    return pl.pallas_call(
        _ffn_kernel,
        grid=(bsz, s // tm),
        in_specs=[tok(d), tok(ATTN_WIDTH), tok(CONV_WIDTH),
                  pl.BlockSpec((1, 6, d), lambda b, i: (b, 0, 0)),
                  _resident(w_out.shape), _resident((1, d)),
                  _resident(w_up.shape), _resident(fw.shape), _resident(fb.shape),
                  _resident(w_down.shape)],
        out_specs=tok(d),
        out_shape=jax.ShapeDtypeStruct((bsz, s, d), F32),
        scratch_shapes=[pltpu.VMEM((FFN_HALO, 2 * FFN_HIDDEN), F32),
                        pltpu.VMEM((tm + FFN_HALO, FFN_CHUNK), F32)],
        compiler_params=pltpu.CompilerParams(dimension_semantics=("arbitrary", "arbitrary"),
                                             vmem_limit_bytes=VMEM_LIMIT),
        name="out_proj_ffn",
    )(x, y_attn, a, mod, w_out, n2g, w_up, fw, fb, w_down)


def _group_avg_matrix():
    half = 256
    idx = np.arange(half) // (CONV_WIDTH // CONV_GROUPS)
    return jnp.asarray((idx[:, None] == idx[None, :]).astype(np.float32) / HEAD_DIM, dtype=BF16)


def kernel(x, c, rel_bias, ada_w, ada_b, norm1_g, w_in, q_norm_g, k_norm_g, conv_dw_w, conv_dw_b,
           conv_norm_g, conv_norm_b, w_out, norm2_g, w_up, ffn_dw_w, ffn_dw_b, w_down):
    bsz, s, d = x.shape
    depth = ada_w.shape[0]
    bias = _bias_tiles(rel_bias)
    gmat = _group_avg_matrix()
    row = lambda t: t.reshape(1, -1)
    for l in range(depth):
        mod = _modulation(c, ada_w[l], ada_b[l]).reshape(bsz, 6, d)
        q, k, v, kmean, a = _in_proj(
            x, mod, row(norm1_g[l]), w_in[l].astype(BF16),
            row(jnp.tile(q_norm_g[l], N_HEADS)), row(jnp.tile(k_norm_g[l], N_HEADS)), gmat,
            conv_dw_w[l], row(conv_dw_b[l]), row(conv_norm_g[l]), row(conv_norm_b[l]))
        kmean = kmean.reshape(bsz, s // MOBA_BLOCK, ATTN_WIDTH)
        y_attn = _attention(rel_bias, q, k, v, kmean, bias)
        x = _out_ffn(x, y_attn, a, mod, w_out[l].astype(BF16), row(norm2_g[l]), w_up[l].astype(BF16),
                     ffn_dw_w[l], row(ffn_dw_b[l]), w_down[l].astype(BF16))
    return x
```

```python
import functools
import math

import numpy as np
import jax
import jax.numpy as jnp
from jax import lax
from jax.experimental import pallas as pl
from jax.experimental.pallas import tpu as pltpu

F32 = jnp.float32
BF16 = jnp.bfloat16

D_MODEL = 1024
HEAD_DIM = 64
ATTN_WIDTH = 512
N_HEADS = ATTN_WIDTH // HEAD_DIM
CONV_WIDTH = 512
CONV_GROUPS = 8
CONV_KERNEL = 31
MOBA_BLOCK = 256
MOBA_TOPK = 3
N_BUCKETS = 32
MAX_DISTANCE = 1024
FFN_HIDDEN = 2816
FFN_CONV = 3
EPS = 1e-6
NEG = -1e30

N_NEAR = 5
PAIR = 2 * HEAD_DIM
HALO = 32
FFN_HALO = 8
FFN_CHUNK = 256
TM_IN = 512
TM_FFN = 512
VMEM_LIMIT = 56 * 1024 * 1024


def _sigmoid(x):
    return 1.0 / (1.0 + jnp.exp(-x))


def _dot(a, b):
    return jnp.dot(a, b, preferred_element_type=F32)


def _dot_nt(a, b):
    return lax.dot_general(a, b, (((1,), (1,)), ((), ())), preferred_element_type=F32)


def _resident(shape):
    zeros = (0,) * len(shape)
    return pl.BlockSpec(shape, lambda *_: zeros, pipeline_mode=pl.Buffered(1))


def _mod_kernel(c_ref, w_ref, b_ref, o_ref):
    c = c_ref[...]
    sc = c * _sigmoid(c)
    o_ref[...] = _dot(sc.astype(BF16), w_ref[...].astype(BF16)) + b_ref[...]


def _modulation(c, ada_w, ada_b):
    bsz, d = c.shape
    n = ada_w.shape[1]
    tn = 1536
    return pl.pallas_call(
        _mod_kernel,
        grid=(n // tn,),
        in_specs=[pl.BlockSpec((bsz, d), lambda j: (0, 0)),
                  pl.BlockSpec((d, tn), lambda j: (0, j)),
                  pl.BlockSpec((1, tn), lambda j: (0, j))],
        out_specs=pl.BlockSpec((bsz, tn), lambda j: (0, j)),
        out_shape=jax.ShapeDtypeStruct((bsz, n), F32),
        compiler_params=pltpu.CompilerParams(dimension_semantics=("arbitrary",),
                                             vmem_limit_bytes=VMEM_LIMIT),
        name="adaln_mod",
    )(c, ada_w, ada_b.reshape(1, n))


def _bias_kernel(rb_ref, o_ref):
    h = pl.program_id(0)
    kk = lax.broadcasted_iota(jnp.int32, (MOBA_BLOCK, MOBA_BLOCK), 0)
    qq = lax.broadcasted_iota(jnp.int32, (MOBA_BLOCK, MOBA_BLOCK), 1)
    max_exact = N_BUCKETS // 2
    for delta in range(N_NEAR):
        dist = delta * MOBA_BLOCK + qq - kk
        n = jnp.maximum(dist, 0)
        nf = jnp.maximum(n, 1).astype(F32)
        large = max_exact + (jnp.log(nf / max_exact) / math.log(MAX_DISTANCE / max_exact)
                             * (N_BUCKETS - max_exact)).astype(jnp.int32)
        large = jnp.minimum(large, N_BUCKETS - 1)
        bucket = jnp.where(n < max_exact, n, large)
        bias = jnp.zeros((MOBA_BLOCK, MOBA_BLOCK), F32)
        for b in range(N_BUCKETS):
            bias = jnp.where(bucket == b, rb_ref[h, b], bias)
        if delta == 0:
            bias = jnp.where(dist >= 0, bias, NEG)
        o_ref[0, delta] = bias


def _bias_tiles(rel_bias):
    return pl.pallas_call(
        _bias_kernel,
        grid=(N_HEADS,),
        in_specs=[pl.BlockSpec(memory_space=pltpu.SMEM)],
        out_specs=pl.BlockSpec((1, N_NEAR, MOBA_BLOCK, MOBA_BLOCK), lambda h: (h, 0, 0, 0)),
        out_shape=jax.ShapeDtypeStruct((N_HEADS, N_NEAR, MOBA_BLOCK, MOBA_BLOCK), F32),
        compiler_params=pltpu.CompilerParams(dimension_semantics=("arbitrary",)),
        name="rel_bias_tiles",
    )(rel_bias)


def _group_mean(v, g_ref):
    vb = v.astype(BF16)
    g = g_ref[...]
    half = g.shape[0]
    parts = [_dot(vb[:, s:s + half], g) for s in range(0, v.shape[1], half)]
    return jnp.concatenate(parts, axis=-1)


def _in_kernel(x_ref, mod_ref, n1g_ref, w_ref, qg_ref, kg_ref, g_ref, cw_ref, cb_ref, gng_ref, gnb_ref,
               q_ref, k_ref, v_ref, km_ref, a_ref, abuf, shbuf, ybuf):
    tm = x_ref.shape[1]

    @pl.when(pl.program_id(1) == 0)
    def _():
        abuf[0:HALO, :] = jnp.zeros((HALO, CONV_WIDTH), F32)

    x = x_ref[0]
    ms = jnp.mean(x * x, axis=-1, keepdims=True)
    xn = x * lax.rsqrt(ms + EPS) * n1g_ref[...]
    xn = xn * (1.0 + mod_ref[0, 1:2, :]) + mod_ref[0, 0:1, :]
    xb = xn.astype(BF16)

    def proj(c):
        return _dot(xb, w_ref[:, c * ATTN_WIDTH:(c + 1) * ATTN_WIDTH])

    glu = proj(3) * _sigmoid(proj(4))

    q = proj(0)
    qn = q * lax.rsqrt(_group_mean(q * q, g_ref) + EPS) * qg_ref[...]
    q_ref[0] = (qn * (HEAD_DIM ** -0.5)).astype(BF16)
    k = proj(1)
    kn = k * lax.rsqrt(_group_mean(k * k, g_ref) + EPS) * kg_ref[...]
    k_ref[0] = kn.astype(BF16)
    km_ref[0, 0] = jnp.concatenate(
        [jnp.mean(kn[t * MOBA_BLOCK:(t + 1) * MOBA_BLOCK], axis=0, keepdims=True)
         for t in range(tm // MOBA_BLOCK)], axis=0)
    v_ref[0] = proj(2).astype(BF16)

    abuf[HALO:HALO + tm, :] = glu
    nsh = shbuf.shape[1]
    for ph in range(1, 8):
        shbuf[ph - 1] = abuf[ph:ph + nsh, :]
    rc, lc = 64, 256
    base = HALO - (CONV_KERNEL - 1)
    for r in range(tm // rc):
        for l in range(CONV_WIDTH // lc):
            cols = slice(l * lc, (l + 1) * lc)
            acc = jnp.broadcast_to(cb_ref[:, cols], (rc, lc))
            for j in range(CONV_KERNEL):
                ph = (base + j) % 8
                row = r * rc + base + j - ph
                tap = abuf[row:row + rc, cols] if ph == 0 else shbuf[ph - 1, row:row + rc, cols]
                acc = acc + cw_ref[j:j + 1, cols] * tap
            ybuf[r * rc:(r + 1) * rc, cols] = acc
    abuf[0:HALO, :] = abuf[tm:tm + HALO, :]

    y = ybuf[...]
    yc = y - _group_mean(y, g_ref)
    var = _group_mean(yc * yc, g_ref)
    yn = yc * lax.rsqrt(var + EPS) * gng_ref[...] + gnb_ref[...]
    a_ref[0] = (yn * _sigmoid(yn)).astype(BF16)


def _in_proj(x, mod, n1g, w_in, qg, kg, gmat, cw, cb, gng, gnb):
    bsz, s, d = x.shape
    tm = TM_IN
    nt = s // tm
    nblk = tm // MOBA_BLOCK
    tok = lambda w: pl.BlockSpec((1, tm, w), lambda b, i: (b, i, 0))
    seq = lambda dt: jax.ShapeDtypeStruct((bsz, s, ATTN_WIDTH), dt)
    return pl.pallas_call(
        _in_kernel,
        grid=(bsz, nt),
        in_specs=[tok(d),
                  pl.BlockSpec((1, 6, d), lambda b, i: (b, 0, 0)),
                  _resident((1, d)),
                  _resident(w_in.shape),
                  _resident((1, ATTN_WIDTH)), _resident((1, ATTN_WIDTH)),
                  _resident(gmat.shape),
                  _resident(cw.shape), _resident((1, CONV_WIDTH)),
                  _resident((1, CONV_WIDTH)), _resident((1, CONV_WIDTH))],
        out_specs=[tok(ATTN_WIDTH), tok(ATTN_WIDTH), tok(ATTN_WIDTH),
                   pl.BlockSpec((1, 1, nblk, ATTN_WIDTH), lambda b, i: (b, i, 0, 0)),
                   tok(CONV_WIDTH)],
        out_shape=[seq(BF16), seq(BF16), seq(BF16),
                   jax.ShapeDtypeStruct((bsz, nt, nblk, ATTN_WIDTH), F32),
                   seq(BF16)],
        scratch_shapes=[pltpu.VMEM((tm + HALO, CONV_WIDTH), F32),
                        pltpu.VMEM((7, tm + HALO - 8, CONV_WIDTH), F32),
                        pltpu.VMEM((tm, CONV_WIDTH), F32)],
        compiler_params=pltpu.CompilerParams(dimension_semantics=("arbitrary", "arbitrary"),
                                             vmem_limit_bytes=VMEM_LIMIT),
        name="in_proj_conv",
    )(x, mod, n1g, w_in, qg, kg, gmat, cw, cb, gng, gnb)


def _attn_kernel(rb_ref, q_ref, k_ref, v_ref, km_ref, bias_ref, o_ref, vt_ref, s_ref, selb_ref):
    hp = pl.program_id(0)
    s = q_ref.shape[1]
    nb = s // MOBA_BLOCK
    blk = MOBA_BLOCK

    for j in range(nb):
        vt = v_ref[0, j * blk:(j + 1) * blk, :].astype(F32).T.astype(BF16)
        for h in range(2):
            vt_ref[h, j] = vt[h * HEAD_DIM:(h + 1) * HEAD_DIM, :]

    km = km_ref[0].astype(BF16)
    lane = lax.broadcasted_iota(jnp.int32, (blk, PAIR), 1)
    rows = lax.broadcasted_iota(jnp.int32, (nb, blk), 0).astype(F32)
    far_bias = [rb_ref[2 * hp + h, N_BUCKETS - 1] for h in range(2)]

    for qi in range(nb):
        q0 = qi * blk
        qpair = q_ref[0, q0:q0 + blk, :]
        qms = [jnp.where((lane >= HEAD_DIM) == (h == 1), qpair, jnp.zeros_like(qpair)) for h in range(2)]
        gated = qi > MOBA_TOPK
        if gated:
            for h in range(2):
                gate = jnp.where(rows < float(qi), _dot_nt(km, qms[h]), -jnp.inf)
                sel = jnp.zeros(gate.shape, jnp.bool_)
                for _ in range(MOBA_TOPK):
                    mx = jnp.max(gate, axis=0, keepdims=True)
                    idx = jnp.min(jnp.where(gate == mx, rows, float(nb)), axis=0, keepdims=True)
                    pick = rows == idx
                    sel = sel | pick
                    gate = jnp.where(pick, -jnp.inf, gate)
                selb_ref[h] = jnp.where(sel, 0.0, NEG)

        def scores(h, j, extra):
            j0 = j * blk if isinstance(j, int) else pl.multiple_of(j * blk, blk)
            st = _dot_nt(k_ref[0, pl.ds(j0, blk), :], qms[h]) + extra
            s_ref[qi % 2, h, j] = st
            return jnp.max(st, axis=0, keepdims=True)

        m = [None, None]
        for delta in range(min(N_NEAR, qi + 1)):
            j = qi - delta
            for h in range(2):
                extra = bias_ref[h, delta]
                if gated and delta > 0:
                    extra = extra + selb_ref[h, j:j + 1, :]
                mb = scores(h, j, extra)
                m[h] = mb if m[h] is None else jnp.maximum(m[h], mb)
        n_far = qi + 1 - N_NEAR
        if n_far > 0:
            def far(j, ms):
                return tuple(jnp.maximum(ms[h], scores(h, j, selb_ref[h, pl.ds(j, 1), :] + far_bias[h]))
                             for h in range(2))
            m = list(lax.fori_loop(0, n_far, far, tuple(m), unroll=4))

        def accumulate(j, carry):
            out = []
            for h in range(2):
                l, acc = carry[h]
                p = jnp.exp(s_ref[qi % 2, h, j] - m[h])
                out.append((l + jnp.sum(p, axis=0, keepdims=True),
                            acc + _dot(vt_ref[h, j], p.astype(BF16))))
            return tuple(out)

        init = tuple((jnp.zeros((1, blk), F32), jnp.zeros((HEAD_DIM, blk), F32)) for _ in range(2))
        res = lax.fori_loop(0, qi + 1, accumulate, init, unroll=4)
        out_t = jnp.concatenate([acc / l for (l, acc) in res], axis=0)
        o_ref[0, q0:q0 + blk, :] = out_t.T.astype(BF16)


def _attention(rel_bias, q, k, v, kmean, bias):
    bsz, s, _ = q.shape
    nb = s // MOBA_BLOCK
    npair = N_HEADS // 2
    seq = pl.BlockSpec((1, s, PAIR), lambda hp, b: (b, 0, hp))
    return pl.pallas_call(
        _attn_kernel,
        grid=(npair, bsz),
        in_specs=[pl.BlockSpec(memory_space=pltpu.SMEM),
                  seq, seq, seq,
                  pl.BlockSpec((1, nb, PAIR), lambda hp, b: (b, 0, hp)),
                  pl.BlockSpec((2, N_NEAR, MOBA_BLOCK, MOBA_BLOCK), lambda hp, b: (hp, 0, 0, 0))],
        out_specs=seq,
        out_shape=jax.ShapeDtypeStruct((bsz, s, ATTN_WIDTH), BF16),
        scratch_shapes=[pltpu.VMEM((2, nb, HEAD_DIM, MOBA_BLOCK), BF16),
                        pltpu.VMEM((2, 2, nb, MOBA_BLOCK, MOBA_BLOCK), F32),
                        pltpu.VMEM((2, nb, MOBA_BLOCK), F32)],
        compiler_params=pltpu.CompilerParams(dimension_semantics=("arbitrary", "arbitrary"),
                                             vmem_limit_bytes=VMEM_LIMIT),
        name="moba_attention",
    )(rel_bias, q, k, v, kmean, bias)


def _ffn_kernel(x_ref, ya_ref, a_ref, mod_ref, wo_ref, n2g_ref, wu_ref, fw_ref, fb_ref, wd_ref,
                o_ref, carry_ref, hbuf):
    tm = x_ref.shape[1]

    @pl.when(pl.program_id(1) == 0)
    def _():
        carry_ref[...] = jnp.zeros(carry_ref.shape, F32)

    g1 = mod_ref[0, 2:3, :]
    sh2 = mod_ref[0, 3:4, :]
    sc2 = mod_ref[0, 4:5, :]
    g2 = mod_ref[0, 5:6, :]

    ycat = jnp.concatenate([ya_ref[0], a_ref[0]], axis=-1)
    x1 = x_ref[0] + g1 * _dot(ycat, wo_ref[...])
    ms = jnp.mean(x1 * x1, axis=-1, keepdims=True)
    xn = x1 * lax.rsqrt(ms + EPS) * n2g_ref[...]
    xb = (xn * (1.0 + sc2) + sh2).astype(BF16)

    def up(c0):
        return _dot(xb, wu_ref[:, c0:c0 + FFN_CHUNK])

    def conv_cols(h, c0, slot):
        cols = slice(c0, c0 + FFN_CHUNK)
        hb = hbuf.at[slot]
        hb[0:FFN_HALO, :] = carry_ref[:, cols]
        hb[FFN_HALO:FFN_HALO + tm, :] = h
        carry_ref[:, cols] = h[tm - FFN_HALO:tm, :]
        out = fb_ref[:, cols] + fw_ref[FFN_CONV - 1:FFN_CONV, cols] * h
        for t in range(1, FFN_CONV):
            out = out + fw_ref[FFN_CONV - 1 - t:FFN_CONV - t, cols] * hb[pl.ds(FFN_HALO - t, tm), :]
        return out

    nch = FFN_HIDDEN // FFN_CHUNK
    hu, hg = up(0), up(FFN_HIDDEN)
    acc = jnp.zeros((tm, D_MODEL), F32)
    for c in range(nch):
        if c + 1 < nch:
            nxt = up((c + 1) * FFN_CHUNK), up(FFN_HIDDEN + (c + 1) * FFN_CHUNK)
        u = conv_cols(hu, c * FFN_CHUNK, 2 * (c % 2))
        g = conv_cols(hg, FFN_HIDDEN + c * FFN_CHUNK, 2 * (c % 2) + 1)
        act = (g * _sigmoid(g) * u).astype(BF16)
        acc = acc + _dot(act, wd_ref[c * FFN_CHUNK:(c + 1) * FFN_CHUNK, :])
        if c + 1 < nch:
            hu, hg = nxt
    o_ref[0] = x1 + g2 * acc


def _out_ffn(x, y_attn, a, mod, w_out, n2g, w_up, fw, fb, w_down):
    bsz, s, d = x.shape
    tm = TM_FFN
    tok = lambda w: pl.BlockSpec((1, tm, w), lambda b, i: (b, i, 0))
    return pl.pallas_call(
        _ffn_kernel,
        grid=(bsz, s // tm),
        in_specs=[tok(d), tok(ATTN_WIDTH), tok(CONV_WIDTH),
                  pl.BlockSpec((1, 6, d), lambda b, i: (b, 0, 0)),
                  _resident(w_out.shape), _resident((1, d)),
                  _resident(w_up.shape), _resident(fw.shape), _resident(fb.shape),
                  _resident(w_down.shape)],
        out_specs=tok(d),
        out_shape=jax.ShapeDtypeStruct((bsz, s, d), F32),
        scratch_shapes=[pltpu.VMEM((FFN_HALO, 2 * FFN_HIDDEN), F32),
                        pltpu.VMEM((4, tm + FFN_HALO, FFN_CHUNK), F32)],
        compiler_params=pltpu.CompilerParams(dimension_semantics=("arbitrary", "arbitrary"),
                                             vmem_limit_bytes=VMEM_LIMIT),
        name="out_proj_ffn",
    )(x, y_attn, a, mod, w_out, n2g, w_up, fw, fb, w_down)


def _group_avg_matrix():
    half = 256
    idx = np.arange(half) // (CONV_WIDTH // CONV_GROUPS)
    return jnp.asarray((idx[:, None] == idx[None, :]).astype(np.float32) / HEAD_DIM, dtype=BF16)


def kernel(x, c, rel_bias, ada_w, ada_b, norm1_g, w_in, q_norm_g, k_norm_g, conv_dw_w, conv_dw_b,
           conv_norm_g, conv_norm_b, w_out, norm2_g, w_up, ffn_dw_w, ffn_dw_b, w_down):
    bsz, s, d = x.shape
    depth = ada_w.shape[0]
    bias = _bias_tiles(rel_bias)
    gmat = _group_avg_matrix()
    row = lambda t: t.reshape(1, -1)
    for l in range(depth):
        mod = _modulation(c, ada_w[l], ada_b[l]).reshape(bsz, 6, d)
        q, k, v, kmean, a = _in_proj(
            x, mod, row(norm1_g[l]), w_in[l].astype(BF16),
            row(jnp.tile(q_norm_g[l], N_HEADS)), row(jnp.tile(k_norm_g[l], N_HEADS)), gmat,
            conv_dw_w[l], row(conv_dw_b[l]), row(conv_norm_g[l]), row(conv_norm_b[l]))
        kmean = kmean.reshape(bsz, s // MOBA_BLOCK, ATTN_WIDTH)
        y_attn = _attention(rel_bias, q, k, v, kmean, bias)
        x = _out_ffn(x, y_attn, a, mod, w_out[l].astype(BF16), row(norm2_g[l]), w_up[l].astype(BF16),
                     ffn_dw_w[l], row(ffn_dw_b[l]), w_down[l].astype(BF16))
    return x
```

```python
import functools
import math

import numpy as np
import jax
import jax.numpy as jnp
from jax import lax
from jax.experimental import pallas as pl
from jax.experimental.pallas import tpu as pltpu

F32 = jnp.float32
BF16 = jnp.bfloat16

D_MODEL = 1024
HEAD_DIM = 64
ATTN_WIDTH = 512
N_HEADS = ATTN_WIDTH // HEAD_DIM
CONV_WIDTH = 512
CONV_GROUPS = 8
CONV_KERNEL = 31
MOBA_BLOCK = 256
MOBA_TOPK = 3
N_BUCKETS = 32
MAX_DISTANCE = 1024
FFN_HIDDEN = 2816
FFN_CONV = 3
EPS = 1e-6
NEG = -1e30

N_NEAR = 5
PAIR = 2 * HEAD_DIM
V_ROWS = HEAD_DIM + 16
HALO = 32
FFN_HALO = 8
FFN_CHUNK = 256
TM_IN = 512
TM_FFN = 512
VMEM_LIMIT = 56 * 1024 * 1024


def _sigmoid(x):
    return 1.0 / (1.0 + jnp.exp(-x))


def _dot(a, b):
    return jnp.dot(a, b, preferred_element_type=F32)


def _dot_nt(a, b):
    return lax.dot_general(a, b, (((1,), (1,)), ((), ())), preferred_element_type=F32)


def _resident(shape):
    zeros = (0,) * len(shape)
    return pl.BlockSpec(shape, lambda *_: zeros, pipeline_mode=pl.Buffered(1))


def _mod_kernel(c_ref, w_ref, b_ref, o_ref):
    c = c_ref[...]
    sc = c * _sigmoid(c)
    o_ref[...] = _dot(sc.astype(BF16), w_ref[...].astype(BF16)) + b_ref[...]


def _modulation(c, ada_w, ada_b):
    bsz, d = c.shape
    n = ada_w.shape[1]
    tn = 1536
    return pl.pallas_call(
        _mod_kernel,
        grid=(n // tn,),
        in_specs=[pl.BlockSpec((bsz, d), lambda j: (0, 0)),
                  pl.BlockSpec((d, tn), lambda j: (0, j)),
                  pl.BlockSpec((1, tn), lambda j: (0, j))],
        out_specs=pl.BlockSpec((bsz, tn), lambda j: (0, j)),
        out_shape=jax.ShapeDtypeStruct((bsz, n), F32),
        compiler_params=pltpu.CompilerParams(dimension_semantics=("arbitrary",),
                                             vmem_limit_bytes=VMEM_LIMIT),
        name="adaln_mod",
    )(c, ada_w, ada_b.reshape(1, n))


def _bias_kernel(rb_ref, o_ref):
    h = pl.program_id(0)
    kk = lax.broadcasted_iota(jnp.int32, (MOBA_BLOCK, MOBA_BLOCK), 0)
    qq = lax.broadcasted_iota(jnp.int32, (MOBA_BLOCK, MOBA_BLOCK), 1)
    max_exact = N_BUCKETS // 2
    for delta in range(N_NEAR):
        dist = delta * MOBA_BLOCK + qq - kk
        n = jnp.maximum(dist, 0)
        nf = jnp.maximum(n, 1).astype(F32)
        large = max_exact + (jnp.log(nf / max_exact) / math.log(MAX_DISTANCE / max_exact)
                             * (N_BUCKETS - max_exact)).astype(jnp.int32)
        large = jnp.minimum(large, N_BUCKETS - 1)
        bucket = jnp.where(n < max_exact, n, large)
        bias = jnp.zeros((MOBA_BLOCK, MOBA_BLOCK), F32)
        for b in range(N_BUCKETS):
            bias = jnp.where(bucket == b, rb_ref[h, b], bias)
        if delta == 0:
            bias = jnp.where(dist >= 0, bias, NEG)
        o_ref[0, delta] = bias


def _bias_tiles(rel_bias):
    return pl.pallas_call(
        _bias_kernel,
        grid=(N_HEADS,),
        in_specs=[pl.BlockSpec(memory_space=pltpu.SMEM)],
        out_specs=pl.BlockSpec((1, N_NEAR, MOBA_BLOCK, MOBA_BLOCK), lambda h: (h, 0, 0, 0)),
        out_shape=jax.ShapeDtypeStruct((N_HEADS, N_NEAR, MOBA_BLOCK, MOBA_BLOCK), F32),
        compiler_params=pltpu.CompilerParams(dimension_semantics=("arbitrary",)),
        name="rel_bias_tiles",
    )(rel_bias)


def _group_mean(v, g_ref):
    vb = v.astype(BF16)
    g = g_ref[...]
    half = g.shape[0]
    parts = [_dot(vb[:, s:s + half], g) for s in range(0, v.shape[1], half)]
    return jnp.concatenate(parts, axis=-1)


def _in_kernel(x_ref, mod_ref, n1g_ref, w_ref, qg_ref, kg_ref, g_ref, cw_ref, cb_ref, gng_ref, gnb_ref,
               q_ref, k_ref, v_ref, km_ref, a_ref, abuf, shbuf, ybuf):
    tm = x_ref.shape[1]

    @pl.when(pl.program_id(1) == 0)
    def _():
        abuf[0:HALO, :] = jnp.zeros((HALO, CONV_WIDTH), F32)

    x = x_ref[0]
    ms = jnp.mean(x * x, axis=-1, keepdims=True)
    xn = x * lax.rsqrt(ms + EPS) * n1g_ref[...]
    xn = xn * (1.0 + mod_ref[0, 1:2, :]) + mod_ref[0, 0:1, :]
    xb = xn.astype(BF16)

    def proj(c):
        return _dot(xb, w_ref[:, c * ATTN_WIDTH:(c + 1) * ATTN_WIDTH])

    glu = proj(3) * _sigmoid(proj(4))

    q = proj(0)
    qn = q * lax.rsqrt(_group_mean(q * q, g_ref) + EPS) * qg_ref[...]
    q_ref[0] = (qn * (HEAD_DIM ** -0.5)).astype(BF16)
    k = proj(1)
    kn = k * lax.rsqrt(_group_mean(k * k, g_ref) + EPS) * kg_ref[...]
    k_ref[0] = kn.astype(BF16)
    km_ref[0, 0] = jnp.concatenate(
        [jnp.mean(kn[t * MOBA_BLOCK:(t + 1) * MOBA_BLOCK], axis=0, keepdims=True)
         for t in range(tm // MOBA_BLOCK)], axis=0)
    v_ref[0] = proj(2).astype(BF16)

    abuf[HALO:HALO + tm, :] = glu
    nsh = shbuf.shape[1]
    for ph in range(1, 8):
        shbuf[ph - 1] = abuf[ph:ph + nsh, :]
    rc, lc = 64, 256
    base = HALO - (CONV_KERNEL - 1)
    for r in range(tm // rc):
        for l in range(CONV_WIDTH // lc):
            cols = slice(l * lc, (l + 1) * lc)
            acc = jnp.broadcast_to(cb_ref[:, cols], (rc, lc))
            for j in range(CONV_KERNEL):
                ph = (base + j) % 8
                row = r * rc + base + j - ph
                tap = abuf[row:row + rc, cols] if ph == 0 else shbuf[ph - 1, row:row + rc, cols]
                acc = acc + cw_ref[j:j + 1, cols] * tap
            ybuf[r * rc:(r + 1) * rc, cols] = acc
    abuf[0:HALO, :] = abuf[tm:tm + HALO, :]

    y = ybuf[...]
    yc = y - _group_mean(y, g_ref)
    var = _group_mean(yc * yc, g_ref)
    yn = yc * lax.rsqrt(var + EPS) * gng_ref[...] + gnb_ref[...]
    a_ref[0] = (yn * _sigmoid(yn)).astype(BF16)


def _in_proj(x, mod, n1g, w_in, qg, kg, gmat, cw, cb, gng, gnb):
    bsz, s, d = x.shape
    tm = TM_IN
    nt = s // tm
    nblk = tm // MOBA_BLOCK
    tok = lambda w: pl.BlockSpec((1, tm, w), lambda b, i: (b, i, 0))
    seq = lambda dt: jax.ShapeDtypeStruct((bsz, s, ATTN_WIDTH), dt)
    return pl.pallas_call(
        _in_kernel,
        grid=(bsz, nt),
        in_specs=[tok(d),
                  pl.BlockSpec((1, 6, d), lambda b, i: (b, 0, 0)),
                  _resident((1, d)),
                  _resident(w_in.shape),
                  _resident((1, ATTN_WIDTH)), _resident((1, ATTN_WIDTH)),
                  _resident(gmat.shape),
                  _resident(cw.shape), _resident((1, CONV_WIDTH)),
                  _resident((1, CONV_WIDTH)), _resident((1, CONV_WIDTH))],
        out_specs=[tok(ATTN_WIDTH), tok(ATTN_WIDTH), tok(ATTN_WIDTH),
                   pl.BlockSpec((1, 1, nblk, ATTN_WIDTH), lambda b, i: (b, i, 0, 0)),
                   tok(CONV_WIDTH)],
        out_shape=[seq(BF16), seq(BF16), seq(BF16),
                   jax.ShapeDtypeStruct((bsz, nt, nblk, ATTN_WIDTH), F32),
                   seq(BF16)],
        scratch_shapes=[pltpu.VMEM((tm + HALO, CONV_WIDTH), F32),
                        pltpu.VMEM((7, tm + HALO - 8, CONV_WIDTH), F32),
                        pltpu.VMEM((tm, CONV_WIDTH), F32)],
        compiler_params=pltpu.CompilerParams(dimension_semantics=("arbitrary", "arbitrary"),
                                             vmem_limit_bytes=VMEM_LIMIT),
        name="in_proj_conv",
    )(x, mod, n1g, w_in, qg, kg, gmat, cw, cb, gng, gnb)


def _attn_kernel(rb_ref, q_ref, k_ref, v_ref, km_ref, bias_ref, o_ref, vt_ref, s_ref, p_ref, selb_ref):
    hp = pl.program_id(0)
    s = q_ref.shape[1]
    nb = s // MOBA_BLOCK
    blk = MOBA_BLOCK

    pad_rows = jnp.where(lax.broadcasted_iota(jnp.int32, (V_ROWS - HEAD_DIM, blk), 0) == 0, 1.0, 0.0).astype(BF16)
    for j in range(nb):
        vt = v_ref[0, j * blk:(j + 1) * blk, :].astype(F32).T.astype(BF16)
        for h in range(2):
            vt_ref[h, j, 0:HEAD_DIM, :] = vt[h * HEAD_DIM:(h + 1) * HEAD_DIM, :]
            vt_ref[h, j, HEAD_DIM:V_ROWS, :] = pad_rows

    km = km_ref[0].astype(BF16)
    lane = lax.broadcasted_iota(jnp.int32, (blk, PAIR), 1)
    rows = lax.broadcasted_iota(jnp.int32, (nb, blk), 0).astype(F32)
    far_bias = [rb_ref[2 * hp + h, N_BUCKETS - 1] for h in range(2)]

    m_scores = None
    m_probs = None
    for t in range(nb + 2):
        do_scores, do_probs, do_values = t < nb, 1 <= t <= nb, 2 <= t <= nb + 1
        sbuf, pbuf_w, pbuf_r = t % 2, (t - 1) % 2, t % 2
        m_probs, m_scores = m_scores, [None, None]

        if do_scores:
            qpair = q_ref[0, t * blk:(t + 1) * blk, :]
            qms = [jnp.where((lane >= HEAD_DIM) == (h == 1), qpair, jnp.zeros_like(qpair)) for h in range(2)]
            gated = t > MOBA_TOPK
            if gated:
                for h in range(2):
                    gate = jnp.where(rows < float(t), _dot_nt(km, qms[h]), -jnp.inf)
                    sel = jnp.zeros(gate.shape, jnp.bool_)
                    for _ in range(MOBA_TOPK):
                        mx = jnp.max(gate, axis=0, keepdims=True)
                        idx = jnp.min(jnp.where(gate == mx, rows, float(nb)), axis=0, keepdims=True)
                        pick = rows == idx
                        sel = sel | pick
                        gate = jnp.where(pick, -jnp.inf, gate)
                    selb_ref[h] = jnp.where(sel, 0.0, NEG)

        def block_start(j):
            return j * blk if isinstance(j, int) else pl.multiple_of(j * blk, blk)

        def scores(h, j, extra):
            st = _dot_nt(k_ref[0, pl.ds(block_start(j), blk), :], qms[h]) + extra
            s_ref[sbuf, h, j] = st
            return jnp.max(st, axis=0, keepdims=True)

        def far_scores(j, ms):
            return [jnp.maximum(ms[h], scores(h, j, selb_ref[h, pl.ds(j, 1), :] + far_bias[h])) for h in range(2)]

        def probs(j):
            for h in range(2):
                p_ref[pbuf_w, h, j] = jnp.exp(s_ref[1 - sbuf, h, j] - m_probs[h]).astype(BF16)

        def values(j, accs):
            return [accs[h] + _dot(vt_ref[h, j], p_ref[pbuf_r, h, j]) for h in range(2)]

        n_far = max(t + 1 - N_NEAR, 0) if do_scores else 0
        n_probs = t if do_probs else 0
        n_values = t - 1 if do_values else 0
        counts = [c for c, on in ((n_far, do_scores), (n_probs, do_probs), (n_values, do_values)) if on]
        n_loop = min(counts)
        accs = [jnp.zeros((V_ROWS, blk), F32) for _ in range(2)]

        if do_scores:
            for delta in range(min(N_NEAR, t + 1)):
                j = t - delta
                for h in range(2):
                    extra = bias_ref[h, delta]
                    if gated and delta > 0:
                        extra = extra + selb_ref[h, j:j + 1, :]
                    mb = scores(h, j, extra)
                    m_scores[h] = mb if m_scores[h] is None else jnp.maximum(m_scores[h], mb)

        if n_loop > 0:
            def body(j, carry):
                ms, ac = carry
                if do_scores:
                    ms = far_scores(j, ms)
                if do_probs:
                    probs(j)
                if do_values:
                    ac = values(j, ac)
                return ms, ac

            init_m = m_scores if do_scores else [jnp.zeros((1, blk), F32)] * 2
            ms, accs = lax.fori_loop(0, n_loop, body, (init_m, accs), unroll=4)
            if do_scores:
                m_scores = ms
        for j in range(n_loop, n_far):
            m_scores = far_scores(j, m_scores)
        for j in range(n_loop, n_probs):
            probs(j)
        for j in range(n_loop, n_values):
            accs = values(j, accs)

        if do_values:
            out_t = jnp.concatenate([a[0:HEAD_DIM] / a[HEAD_DIM:HEAD_DIM + 1] for a in accs], axis=0)
            o_ref[0, (t - 2) * blk:(t - 1) * blk, :] = out_t.T.astype(BF16)


def _attention(rel_bias, q, k, v, kmean, bias):
    bsz, s, _ = q.shape
    nb = s // MOBA_BLOCK
    npair = N_HEADS // 2
    seq = pl.BlockSpec((1, s, PAIR), lambda hp, b: (b, 0, hp))
    return pl.pallas_call(
        _attn_kernel,
        grid=(npair, bsz),
        in_specs=[pl.BlockSpec(memory_space=pltpu.SMEM),
                  seq, seq, seq,
                  pl.BlockSpec((1, nb, PAIR), lambda hp, b: (b, 0, hp)),
                  pl.BlockSpec((2, N_NEAR, MOBA_BLOCK, MOBA_BLOCK), lambda hp, b: (hp, 0, 0, 0))],
        out_specs=seq,
        out_shape=jax.ShapeDtypeStruct((bsz, s, ATTN_WIDTH), BF16),
        scratch_shapes=[pltpu.VMEM((2, nb, V_ROWS, MOBA_BLOCK), BF16),
                        pltpu.VMEM((2, 2, nb, MOBA_BLOCK, MOBA_BLOCK), F32),
                        pltpu.VMEM((2, 2, nb, MOBA_BLOCK, MOBA_BLOCK), BF16),
                        pltpu.VMEM((2, nb, MOBA_BLOCK), F32)],
        compiler_params=pltpu.CompilerParams(dimension_semantics=("arbitrary", "arbitrary"),
                                             vmem_limit_bytes=VMEM_LIMIT),
        name="moba_attention",
    )(rel_bias, q, k, v, kmean, bias)


def _ffn_kernel(x_ref, ya_ref, a_ref, mod_ref, wo_ref, n2g_ref, wu_ref, fw_ref, fb_ref, wd_ref,
                o_ref, carry_ref, hbuf):
    tm = x_ref.shape[1]

    @pl.when(pl.program_id(1) == 0)
    def _():
        carry_ref[...] = jnp.zeros(carry_ref.shape, F32)

    g1 = mod_ref[0, 2:3, :]
    sh2 = mod_ref[0, 3:4, :]
    sc2 = mod_ref[0, 4:5, :]
    g2 = mod_ref[0, 5:6, :]

    ycat = jnp.concatenate([ya_ref[0], a_ref[0]], axis=-1)
    x1 = x_ref[0] + g1 * _dot(ycat, wo_ref[...])
    ms = jnp.mean(x1 * x1, axis=-1, keepdims=True)
    xn = x1 * lax.rsqrt(ms + EPS) * n2g_ref[...]
    xb = (xn * (1.0 + sc2) + sh2).astype(BF16)

    def up(c0):
        return _dot(xb, wu_ref[:, c0:c0 + FFN_CHUNK])

    def conv_cols(h, c0, slot):
        cols = slice(c0, c0 + FFN_CHUNK)
        hb = hbuf.at[slot]
        hb[0:FFN_HALO, :] = carry_ref[:, cols]
        hb[FFN_HALO:FFN_HALO + tm, :] = h
        carry_ref[:, cols] = h[tm - FFN_HALO:tm, :]
        out = fb_ref[:, cols] + fw_ref[FFN_CONV - 1:FFN_CONV, cols] * h
        for t in range(1, FFN_CONV):
            out = out + fw_ref[FFN_CONV - 1 - t:FFN_CONV - t, cols] * hb[pl.ds(FFN_HALO - t, tm), :]
        return out

    nch = FFN_HIDDEN // FFN_CHUNK
    hu, hg = up(0), up(FFN_HIDDEN)
    acc = jnp.zeros((tm, D_MODEL), F32)
    for c in range(nch):
        if c + 1 < nch:
            nxt = up((c + 1) * FFN_CHUNK), up(FFN_HIDDEN + (c + 1) * FFN_CHUNK)
        u = conv_cols(hu, c * FFN_CHUNK, 2 * (c % 2))
        g = conv_cols(hg, FFN_HIDDEN + c * FFN_CHUNK, 2 * (c % 2) + 1)
        act = (g * _sigmoid(g) * u).astype(BF16)
        acc = acc + _dot(act, wd_ref[c * FFN_CHUNK:(c + 1) * FFN_CHUNK, :])
        if c + 1 < nch:
            hu, hg = nxt
    o_ref[0] = x1 + g2 * acc


def _out_ffn(x, y_attn, a, mod, w_out, n2g, w_up, fw, fb, w_down):
    bsz, s, d = x.shape
    tm = TM_FFN
    tok = lambda w: pl.BlockSpec((1, tm, w), lambda b, i: (b, i, 0))
    return pl.pallas_call(
        _ffn_kernel,
        grid=(bsz, s // tm),
        in_specs=[tok(d), tok(ATTN_WIDTH), tok(CONV_WIDTH),
                  pl.BlockSpec((1, 6, d), lambda b, i: (b, 0, 0)),
                  _resident(w_out.shape), _resident((1, d)),
                  _resident(w_up.shape), _resident(fw.shape), _resident(fb.shape),
                  _resident(w_down.shape)],
        out_specs=tok(d),
        out_shape=jax.ShapeDtypeStruct((bsz, s, d), F32),
        scratch_shapes=[pltpu.VMEM((FFN_HALO, 2 * FFN_HIDDEN), F32),
                        pltpu.VMEM((4, tm + FFN_HALO, FFN_CHUNK), F32)],
        compiler_params=pltpu.CompilerParams(dimension_semantics=("arbitrary", "arbitrary"),
                                             vmem_limit_bytes=VMEM_LIMIT),
        name="out_proj_ffn",
    )(x, y_attn, a, mod, w_out, n2g, w_up, fw, fb, w_down)


def _group_avg_matrix():
    half = 256
    idx = np.arange(half) // (CONV_WIDTH // CONV_GROUPS)
    return jnp.asarray((idx[:, None] == idx[None, :]).astype(np.float32) / HEAD_DIM, dtype=BF16)


def kernel(x, c, rel_bias, ada_w, ada_b, norm1_g, w_in, q_norm_g, k_norm_g, conv_dw_w, conv_dw_b,
           conv_norm_g, conv_norm_b, w_out, norm2_g, w_up, ffn_dw_w, ffn_dw_b, w_down):
    bsz, s, d = x.shape
    depth = ada_w.shape[0]
    bias = _bias_tiles(rel_bias)
    gmat = _group_avg_matrix()
    row = lambda t: t.reshape(1, -1)
    for l in range(depth):
        mod = _modulation(c, ada_w[l], ada_b[l]).reshape(bsz, 6, d)
        q, k, v, kmean, a = _in_proj(
            x, mod, row(norm1_g[l]), w_in[l].astype(BF16),
            row(jnp.tile(q_norm_g[l], N_HEADS)), row(jnp.tile(k_norm_g[l], N_HEADS)), gmat,
            conv_dw_w[l], row(conv_dw_b[l]), row(conv_norm_g[l]), row(conv_norm_b[l]))
        kmean = kmean.reshape(bsz, s // MOBA_BLOCK, ATTN_WIDTH)
        y_attn = _attention(rel_bias, q, k, v, kmean, bias)
        x = _out_ffn(x, y_attn, a, mod, w_out[l].astype(BF16), row(norm2_g[l]), w_up[l].astype(BF16),
                     ffn_dw_w[l], row(ffn_dw_b[l]), w_down[l].astype(BF16))
    return x
```

```python
import functools
import math

import numpy as np
import jax
import jax.numpy as jnp
from jax import lax
from jax.experimental import pallas as pl
from jax.experimental.pallas import tpu as pltpu

F32 = jnp.float32
BF16 = jnp.bfloat16

D_MODEL = 1024
HEAD_DIM = 64
ATTN_WIDTH = 512
N_HEADS = ATTN_WIDTH // HEAD_DIM
CONV_WIDTH = 512
CONV_GROUPS = 8
CONV_KERNEL = 31
MOBA_BLOCK = 256
MOBA_TOPK = 3
N_BUCKETS = 32
MAX_DISTANCE = 1024
FFN_HIDDEN = 2816
FFN_CONV = 3
EPS = 1e-6
NEG = -1e30

N_NEAR = 5
PAIR = 2 * HEAD_DIM
V_ROWS = HEAD_DIM + 16
LOOP_BLOCKS = 4
HALO = 32
FFN_HALO = 8
FFN_CHUNK = 256
TM_IN = 512
TM_FFN = 512
VMEM_LIMIT = 56 * 1024 * 1024


def _sigmoid(x):
    return 1.0 / (1.0 + jnp.exp(-x))


def _dot(a, b):
    return jnp.dot(a, b, preferred_element_type=F32)


def _dot_nt(a, b):
    return lax.dot_general(a, b, (((1,), (1,)), ((), ())), preferred_element_type=F32)


def _resident(shape):
    zeros = (0,) * len(shape)
    return pl.BlockSpec(shape, lambda *_: zeros, pipeline_mode=pl.Buffered(1))


def _mod_kernel(c_ref, w_ref, b_ref, o_ref):
    c = c_ref[...]
    sc = c * _sigmoid(c)
    o_ref[...] = _dot(sc.astype(BF16), w_ref[...].astype(BF16)) + b_ref[...]


def _modulation(c, ada_w, ada_b):
    bsz, d = c.shape
    n = ada_w.shape[1]
    tn = 1536
    return pl.pallas_call(
        _mod_kernel,
        grid=(n // tn,),
        in_specs=[pl.BlockSpec((bsz, d), lambda j: (0, 0)),
                  pl.BlockSpec((d, tn), lambda j: (0, j)),
                  pl.BlockSpec((1, tn), lambda j: (0, j))],
        out_specs=pl.BlockSpec((bsz, tn), lambda j: (0, j)),
        out_shape=jax.ShapeDtypeStruct((bsz, n), F32),
        compiler_params=pltpu.CompilerParams(dimension_semantics=("arbitrary",),
                                             vmem_limit_bytes=VMEM_LIMIT),
        name="adaln_mod",
    )(c, ada_w, ada_b.reshape(1, n))


def _bias_kernel(rb_ref, o_ref):
    h = pl.program_id(0)
    kk = lax.broadcasted_iota(jnp.int32, (MOBA_BLOCK, MOBA_BLOCK), 0)
    qq = lax.broadcasted_iota(jnp.int32, (MOBA_BLOCK, MOBA_BLOCK), 1)
    max_exact = N_BUCKETS // 2
    for delta in range(N_NEAR):
        dist = delta * MOBA_BLOCK + qq - kk
        n = jnp.maximum(dist, 0)
        nf = jnp.maximum(n, 1).astype(F32)
        large = max_exact + (jnp.log(nf / max_exact) / math.log(MAX_DISTANCE / max_exact)
                             * (N_BUCKETS - max_exact)).astype(jnp.int32)
        large = jnp.minimum(large, N_BUCKETS - 1)
        bucket = jnp.where(n < max_exact, n, large)
        bias = jnp.zeros((MOBA_BLOCK, MOBA_BLOCK), F32)
        for b in range(N_BUCKETS):
            bias = jnp.where(bucket == b, rb_ref[h, b], bias)
        if delta == 0:
            bias = jnp.where(dist >= 0, bias, NEG)
        o_ref[0, delta] = bias


def _bias_tiles(rel_bias):
    return pl.pallas_call(
        _bias_kernel,
        grid=(N_HEADS,),
        in_specs=[pl.BlockSpec(memory_space=pltpu.SMEM)],
        out_specs=pl.BlockSpec((1, N_NEAR, MOBA_BLOCK, MOBA_BLOCK), lambda h: (h, 0, 0, 0)),
        out_shape=jax.ShapeDtypeStruct((N_HEADS, N_NEAR, MOBA_BLOCK, MOBA_BLOCK), F32),
        compiler_params=pltpu.CompilerParams(dimension_semantics=("arbitrary",)),
        name="rel_bias_tiles",
    )(rel_bias)


def _group_mean(v, g_ref):
    vb = v.astype(BF16)
    g = g_ref[...]
    half = g.shape[0]
    parts = [_dot(vb[:, s:s + half], g) for s in range(0, v.shape[1], half)]
    return jnp.concatenate(parts, axis=-1)


def _in_kernel(x_ref, mod_ref, n1g_ref, w_ref, qg_ref, kg_ref, g_ref, cw_ref, cb_ref, gng_ref, gnb_ref,
               q_ref, k_ref, v_ref, km_ref, a_ref, abuf, shbuf, ybuf):
    tm = x_ref.shape[1]

    @pl.when(pl.program_id(1) == 0)
    def _():
        abuf[0:HALO, :] = jnp.zeros((HALO, CONV_WIDTH), F32)

    x = x_ref[0]
    ms = jnp.mean(x * x, axis=-1, keepdims=True)
    xn = x * lax.rsqrt(ms + EPS) * n1g_ref[...]
    xn = xn * (1.0 + mod_ref[0, 1:2, :]) + mod_ref[0, 0:1, :]
    xb = xn.astype(BF16)

    def proj(c):
        return _dot(xb, w_ref[:, c * ATTN_WIDTH:(c + 1) * ATTN_WIDTH])

    glu = proj(3) * _sigmoid(proj(4))

    q = proj(0)
    qn = q * lax.rsqrt(_group_mean(q * q, g_ref) + EPS) * qg_ref[...]
    q_ref[0] = (qn * (HEAD_DIM ** -0.5)).astype(BF16)
    k = proj(1)
    kn = k * lax.rsqrt(_group_mean(k * k, g_ref) + EPS) * kg_ref[...]
    k_ref[0] = kn.astype(BF16)
    km_ref[0, 0] = jnp.concatenate(
        [jnp.mean(kn[t * MOBA_BLOCK:(t + 1) * MOBA_BLOCK], axis=0, keepdims=True)
         for t in range(tm // MOBA_BLOCK)], axis=0)
    v_ref[0] = proj(2).astype(BF16)

    abuf[HALO:HALO + tm, :] = glu
    nsh = shbuf.shape[1]
    for ph in range(1, 8):
        shbuf[ph - 1] = abuf[ph:ph + nsh, :]
    rc, lc = 64, 256
    base = HALO - (CONV_KERNEL - 1)
    for r in range(tm // rc):
        for l in range(CONV_WIDTH // lc):
            cols = slice(l * lc, (l + 1) * lc)
            acc = jnp.broadcast_to(cb_ref[:, cols], (rc, lc))
            for j in range(CONV_KERNEL):
                ph = (base + j) % 8
                row = r * rc + base + j - ph
                tap = abuf[row:row + rc, cols] if ph == 0 else shbuf[ph - 1, row:row + rc, cols]
                acc = acc + cw_ref[j:j + 1, cols] * tap
            ybuf[r * rc:(r + 1) * rc, cols] = acc
    abuf[0:HALO, :] = abuf[tm:tm + HALO, :]

    y = ybuf[...]
    yc = y - _group_mean(y, g_ref)
    var = _group_mean(yc * yc, g_ref)
    yn = yc * lax.rsqrt(var + EPS) * gng_ref[...] + gnb_ref[...]
    a_ref[0] = (yn * _sigmoid(yn)).astype(BF16)


def _in_proj(x, mod, n1g, w_in, qg, kg, gmat, cw, cb, gng, gnb):
    bsz, s, d = x.shape
    tm = TM_IN
    nt = s // tm
    nblk = tm // MOBA_BLOCK
    tok = lambda w: pl.BlockSpec((1, tm, w), lambda b, i: (b, i, 0))
    seq = lambda dt: jax.ShapeDtypeStruct((bsz, s, ATTN_WIDTH), dt)
    return pl.pallas_call(
        _in_kernel,
        grid=(bsz, nt),
        in_specs=[tok(d),
                  pl.BlockSpec((1, 6, d), lambda b, i: (b, 0, 0)),
                  _resident((1, d)),
                  _resident(w_in.shape),
                  _resident((1, ATTN_WIDTH)), _resident((1, ATTN_WIDTH)),
                  _resident(gmat.shape),
                  _resident(cw.shape), _resident((1, CONV_WIDTH)),
                  _resident((1, CONV_WIDTH)), _resident((1, CONV_WIDTH))],
        out_specs=[tok(ATTN_WIDTH), tok(ATTN_WIDTH), tok(ATTN_WIDTH),
                   pl.BlockSpec((1, 1, nblk, ATTN_WIDTH), lambda b, i: (b, i, 0, 0)),
                   tok(CONV_WIDTH)],
        out_shape=[seq(BF16), seq(BF16), seq(BF16),
                   jax.ShapeDtypeStruct((bsz, nt, nblk, ATTN_WIDTH), F32),
                   seq(BF16)],
        scratch_shapes=[pltpu.VMEM((tm + HALO, CONV_WIDTH), F32),
                        pltpu.VMEM((7, tm + HALO - 8, CONV_WIDTH), F32),
                        pltpu.VMEM((tm, CONV_WIDTH), F32)],
        compiler_params=pltpu.CompilerParams(dimension_semantics=("arbitrary", "arbitrary"),
                                             vmem_limit_bytes=VMEM_LIMIT),
        name="in_proj_conv",
    )(x, mod, n1g, w_in, qg, kg, gmat, cw, cb, gng, gnb)


def _attn_kernel(rb_ref, q_ref, k_ref, v_ref, km_ref, bias_ref, o_ref, vt_ref, s_ref, selb_ref):
    hp = pl.program_id(0)
    s = q_ref.shape[1]
    nb = s // MOBA_BLOCK
    blk = MOBA_BLOCK

    pad_rows = jnp.where(lax.broadcasted_iota(jnp.int32, (V_ROWS - HEAD_DIM, blk), 0) == 0, 1.0, 0.0).astype(BF16)
    for j in range(nb):
        vt = v_ref[0, j * blk:(j + 1) * blk, :].astype(F32).T.astype(BF16)
        for h in range(2):
            vt_ref[h, j, 0:HEAD_DIM, :] = vt[h * HEAD_DIM:(h + 1) * HEAD_DIM, :]
            vt_ref[h, j, HEAD_DIM:V_ROWS, :] = pad_rows

    km = km_ref[0].astype(BF16)
    lane = lax.broadcasted_iota(jnp.int32, (blk, PAIR), 1)
    rows = lax.broadcasted_iota(jnp.int32, (nb, blk), 0).astype(F32)
    far_bias = [rb_ref[2 * hp + h, N_BUCKETS - 1] for h in range(2)]

    m_scores = None
    m_values = None
    for t in range(nb + 1):
        do_scores, do_values = t < nb, t >= 1
        sbuf = t % 2
        m_values, m_scores = m_scores, [None, None]

        if do_scores:
            qpair = q_ref[0, t * blk:(t + 1) * blk, :]
            qms = [jnp.where((lane >= HEAD_DIM) == (h == 1), qpair, jnp.zeros_like(qpair)) for h in range(2)]
            gated = t > MOBA_TOPK
            if gated:
                for h in range(2):
                    gate = jnp.where(rows < float(t), _dot_nt(km, qms[h]), -jnp.inf)
                    sel = jnp.zeros(gate.shape, jnp.bool_)
                    for _ in range(MOBA_TOPK):
                        mx = jnp.max(gate, axis=0, keepdims=True)
                        idx = jnp.min(jnp.where(gate == mx, rows, float(nb)), axis=0, keepdims=True)
                        pick = rows == idx
                        sel = sel | pick
                        gate = jnp.where(pick, -jnp.inf, gate)
                    selb_ref[h] = jnp.where(sel, 0.0, NEG)

        def block_start(j):
            return j * blk if isinstance(j, int) else pl.multiple_of(j * blk, blk)

        def scores(h, j, extra):
            st = _dot_nt(k_ref[0, pl.ds(block_start(j), blk), :], qms[h]) + extra
            s_ref[sbuf, h, j] = st
            return jnp.max(st, axis=0, keepdims=True)

        def far_scores(j, ms):
            return [jnp.maximum(ms[h], scores(h, j, selb_ref[h, pl.ds(j, 1), :] + far_bias[h])) for h in range(2)]

        def values(js, accs):
            out = []
            for h in range(2):
                acc = accs[h]
                for j in js:
                    p = jnp.exp(s_ref[1 - sbuf, h, j] - m_values[h]).astype(BF16)
                    acc = acc + _dot(vt_ref[h, j], p)
                out.append(acc)
            return out

        n_far = max(t + 1 - N_NEAR, 0) if do_scores else 0
        n_values = t if do_values else 0
        n_loop = (min(n_far, n_values) if do_scores and do_values else max(n_far, n_values)) // LOOP_BLOCKS
        accs = [jnp.zeros((V_ROWS, blk), F32) for _ in range(2)]

        if do_scores:
            for delta in range(min(N_NEAR, t + 1)):
                j = t - delta
                for h in range(2):
                    extra = bias_ref[h, delta]
                    if gated and delta > 0:
                        extra = extra + selb_ref[h, j:j + 1, :]
                    mb = scores(h, j, extra)
                    m_scores[h] = mb if m_scores[h] is None else jnp.maximum(m_scores[h], mb)

        if n_loop > 0:
            def body(g, carry):
                ms, ac = carry
                js = [g * LOOP_BLOCKS + u for u in range(LOOP_BLOCKS)]
                if do_scores:
                    for j in js:
                        ms = far_scores(j, ms)
                if do_values:
                    ac = values(js, ac)
                return ms, ac

            init_m = m_scores if do_scores else [jnp.zeros((1, blk), F32)] * 2
            ms, accs = lax.fori_loop(0, n_loop, body, (init_m, accs))
            if do_scores:
                m_scores = ms
        done = n_loop * LOOP_BLOCKS
        for j in range(done, n_far):
            m_scores = far_scores(j, m_scores)
        if n_values > done:
            accs = values(list(range(done, n_values)), accs)

        if do_values:
            out_t = jnp.concatenate([a[0:HEAD_DIM] / a[HEAD_DIM:HEAD_DIM + 1] for a in accs], axis=0)
            o_ref[0, (t - 1) * blk:t * blk, :] = out_t.T.astype(BF16)


def _attention(rel_bias, q, k, v, kmean, bias):
    bsz, s, _ = q.shape
    nb = s // MOBA_BLOCK
    npair = N_HEADS // 2
    seq = pl.BlockSpec((1, s, PAIR), lambda hp, b: (b, 0, hp))
    return pl.pallas_call(
        _attn_kernel,
        grid=(npair, bsz),
        in_specs=[pl.BlockSpec(memory_space=pltpu.SMEM),
                  seq, seq, seq,
                  pl.BlockSpec((1, nb, PAIR), lambda hp, b: (b, 0, hp)),
                  pl.BlockSpec((2, N_NEAR, MOBA_BLOCK, MOBA_BLOCK), lambda hp, b: (hp, 0, 0, 0))],
        out_specs=seq,
        out_shape=jax.ShapeDtypeStruct((bsz, s, ATTN_WIDTH), BF16),
        scratch_shapes=[pltpu.VMEM((2, nb, V_ROWS, MOBA_BLOCK), BF16),
                        pltpu.VMEM((2, 2, nb, MOBA_BLOCK, MOBA_BLOCK), F32),
                        pltpu.VMEM((2, nb, MOBA_BLOCK), F32)],
        compiler_params=pltpu.CompilerParams(dimension_semantics=("arbitrary", "arbitrary"),
                                             vmem_limit_bytes=VMEM_LIMIT),
        name="moba_attention",
    )(rel_bias, q, k, v, kmean, bias)


def _ffn_kernel(x_ref, ya_ref, a_ref, mod_ref, wo_ref, n2g_ref, wu_ref, fw_ref, fb_ref, wd_ref,
                o_ref, carry_ref, hbuf):
    tm = x_ref.shape[1]

    @pl.when(pl.program_id(1) == 0)
    def _():
        carry_ref[...] = jnp.zeros(carry_ref.shape, F32)

    g1 = mod_ref[0, 2:3, :]
    sh2 = mod_ref[0, 3:4, :]
    sc2 = mod_ref[0, 4:5, :]
    g2 = mod_ref[0, 5:6, :]

    ycat = jnp.concatenate([ya_ref[0], a_ref[0]], axis=-1)
    x1 = x_ref[0] + g1 * _dot(ycat, wo_ref[...])
    ms = jnp.mean(x1 * x1, axis=-1, keepdims=True)
    xn = x1 * lax.rsqrt(ms + EPS) * n2g_ref[...]
    xb = (xn * (1.0 + sc2) + sh2).astype(BF16)

    def up(c0):
        return _dot(xb, wu_ref[:, c0:c0 + FFN_CHUNK])

    def conv_cols(h, c0, slot):
        cols = slice(c0, c0 + FFN_CHUNK)
        hb = hbuf.at[slot]
        hb[0:FFN_HALO, :] = carry_ref[:, cols]
        hb[FFN_HALO:FFN_HALO + tm, :] = h
        carry_ref[:, cols] = h[tm - FFN_HALO:tm, :]
        out = fb_ref[:, cols] + fw_ref[FFN_CONV - 1:FFN_CONV, cols] * h
        for t in range(1, FFN_CONV):
            out = out + fw_ref[FFN_CONV - 1 - t:FFN_CONV - t, cols] * hb[pl.ds(FFN_HALO - t, tm), :]
        return out

    nch = FFN_HIDDEN // FFN_CHUNK
    hu, hg = up(0), up(FFN_HIDDEN)
    acc = jnp.zeros((tm, D_MODEL), F32)
    for c in range(nch):
        if c + 1 < nch:
            nxt = up((c + 1) * FFN_CHUNK), up(FFN_HIDDEN + (c + 1) * FFN_CHUNK)
        u = conv_cols(hu, c * FFN_CHUNK, 2 * (c % 2))
        g = conv_cols(hg, FFN_HIDDEN + c * FFN_CHUNK, 2 * (c % 2) + 1)
        act = (g * _sigmoid(g) * u).astype(BF16)
        acc = acc + _dot(act, wd_ref[c * FFN_CHUNK:(c + 1) * FFN_CHUNK, :])
        if c + 1 < nch:
            hu, hg = nxt
    o_ref[0] = x1 + g2 * acc


def _out_ffn(x, y_attn, a, mod, w_out, n2g, w_up, fw, fb, w_down):
    bsz, s, d = x.shape
    tm = TM_FFN
    tok = lambda w: pl.BlockSpec((1, tm, w), lambda b, i: (b, i, 0))
    return pl.pallas_call(
        _ffn_kernel,
        grid=(bsz, s // tm),
        in_specs=[tok(d), tok(ATTN_WIDTH), tok(CONV_WIDTH),
                  pl.BlockSpec((1, 6, d), lambda b, i: (b, 0, 0)),
                  _resident(w_out.shape), _resident((1, d)),
                  _resident(w_up.shape), _resident(fw.shape), _resident(fb.shape),
                  _resident(w_down.shape)],
        out_specs=tok(d),
        out_shape=jax.ShapeDtypeStruct((bsz, s, d), F32),
        scratch_shapes=[pltpu.VMEM((FFN_HALO, 2 * FFN_HIDDEN), F32),
                        pltpu.VMEM((4, tm + FFN_HALO, FFN_CHUNK), F32)],
        compiler_params=pltpu.CompilerParams(dimension_semantics=("arbitrary", "arbitrary"),
                                             vmem_limit_bytes=VMEM_LIMIT),
        name="out_proj_ffn",
    )(x, y_attn, a, mod, w_out, n2g, w_up, fw, fb, w_down)


def _group_avg_matrix():
    half = 256
    idx = np.arange(half) // (CONV_WIDTH // CONV_GROUPS)
    return jnp.asarray((idx[:, None] == idx[None, :]).astype(np.float32) / HEAD_DIM, dtype=BF16)


def kernel(x, c, rel_bias, ada_w, ada_b, norm1_g, w_in, q_norm_g, k_norm_g, conv_dw_w, conv_dw_b,
           conv_norm_g, conv_norm_b, w_out, norm2_g, w_up, ffn_dw_w, ffn_dw_b, w_down):
    bsz, s, d = x.shape
    depth = ada_w.shape[0]
    bias = _bias_tiles(rel_bias)
    gmat = _group_avg_matrix()
    row = lambda t: t.reshape(1, -1)
    for l in range(depth):
        mod = _modulation(c, ada_w[l], ada_b[l]).reshape(bsz, 6, d)
        q, k, v, kmean, a = _in_proj(
            x, mod, row(norm1_g[l]), w_in[l].astype(BF16),
            row(jnp.tile(q_norm_g[l], N_HEADS)), row(jnp.tile(k_norm_g[l], N_HEADS)), gmat,
            conv_dw_w[l], row(conv_dw_b[l]), row(conv_norm_g[l]), row(conv_norm_b[l]))
        kmean = kmean.reshape(bsz, s // MOBA_BLOCK, ATTN_WIDTH)
        y_attn = _attention(rel_bias, q, k, v, kmean, bias)
        x = _out_ffn(x, y_attn, a, mod, w_out[l].astype(BF16), row(norm2_g[l]), w_up[l].astype(BF16),
                     ffn_dw_w[l], row(ffn_dw_b[l]), w_down[l].astype(BF16))
    return x
```

```python
import functools
import math

import numpy as np
import jax
import jax.numpy as jnp
from jax import lax
from jax.experimental import pallas as pl
from jax.experimental.pallas import tpu as pltpu

F32 = jnp.float32
BF16 = jnp.bfloat16

D_MODEL = 1024
HEAD_DIM = 64
ATTN_WIDTH = 512
N_HEADS = ATTN_WIDTH // HEAD_DIM
CONV_WIDTH = 512
CONV_GROUPS = 8
CONV_KERNEL = 31
MOBA_BLOCK = 256
MOBA_TOPK = 3
N_BUCKETS = 32
MAX_DISTANCE = 1024
FFN_HIDDEN = 2816
FFN_CONV = 3
EPS = 1e-6
NEG = -1e30

N_NEAR = 5
PAIR = 2 * HEAD_DIM
V_ROWS = HEAD_DIM + 16
LOOP_BLOCKS = 4
HALO = 32
FFN_HALO = 8
FFN_CHUNKS = (512, 512, 512, 512, 512, 256)
assert sum(FFN_CHUNKS) == FFN_HIDDEN
TM_IN = 512
TM_FFN = 512
VMEM_LIMIT = 56 * 1024 * 1024


def _sigmoid(x):
    return 1.0 / (1.0 + jnp.exp(-x))


def _dot(a, b):
    return jnp.dot(a, b, preferred_element_type=F32)


def _dot_nt(a, b):
    return lax.dot_general(a, b, (((1,), (1,)), ((), ())), preferred_element_type=F32)


def _resident(shape):
    zeros = (0,) * len(shape)
    return pl.BlockSpec(shape, lambda *_: zeros, pipeline_mode=pl.Buffered(1))


def _mod_kernel(c_ref, w_ref, b_ref, o_ref):
    c = c_ref[...]
    sc = c * _sigmoid(c)
    o_ref[...] = _dot(sc.astype(BF16), w_ref[...].astype(BF16)) + b_ref[...]


def _modulation(c, ada_w, ada_b):
    bsz, d = c.shape
    n = ada_w.shape[1]
    tn = 1536
    return pl.pallas_call(
        _mod_kernel,
        grid=(n // tn,),
        in_specs=[pl.BlockSpec((bsz, d), lambda j: (0, 0)),
                  pl.BlockSpec((d, tn), lambda j: (0, j)),
                  pl.BlockSpec((1, tn), lambda j: (0, j))],
        out_specs=pl.BlockSpec((bsz, tn), lambda j: (0, j)),
        out_shape=jax.ShapeDtypeStruct((bsz, n), F32),
        compiler_params=pltpu.CompilerParams(dimension_semantics=("arbitrary",),
                                             vmem_limit_bytes=VMEM_LIMIT),
        name="adaln_mod",
    )(c, ada_w, ada_b.reshape(1, n))


def _bias_kernel(rb_ref, o_ref):
    h = pl.program_id(0)
    kk = lax.broadcasted_iota(jnp.int32, (MOBA_BLOCK, MOBA_BLOCK), 0)
    qq = lax.broadcasted_iota(jnp.int32, (MOBA_BLOCK, MOBA_BLOCK), 1)
    max_exact = N_BUCKETS // 2
    for delta in range(N_NEAR):
        dist = delta * MOBA_BLOCK + qq - kk
        n = jnp.maximum(dist, 0)
        nf = jnp.maximum(n, 1).astype(F32)
        large = max_exact + (jnp.log(nf / max_exact) / math.log(MAX_DISTANCE / max_exact)
                             * (N_BUCKETS - max_exact)).astype(jnp.int32)
        large = jnp.minimum(large, N_BUCKETS - 1)
        bucket = jnp.where(n < max_exact, n, large)
        bias = jnp.zeros((MOBA_BLOCK, MOBA_BLOCK), F32)
        for b in range(N_BUCKETS):
            bias = jnp.where(bucket == b, rb_ref[h, b], bias)
        if delta == 0:
            bias = jnp.where(dist >= 0, bias, NEG)
        o_ref[0, delta] = bias


def _bias_tiles(rel_bias):
    return pl.pallas_call(
        _bias_kernel,
        grid=(N_HEADS,),
        in_specs=[pl.BlockSpec(memory_space=pltpu.SMEM)],
        out_specs=pl.BlockSpec((1, N_NEAR, MOBA_BLOCK, MOBA_BLOCK), lambda h: (h, 0, 0, 0)),
        out_shape=jax.ShapeDtypeStruct((N_HEADS, N_NEAR, MOBA_BLOCK, MOBA_BLOCK), F32),
        compiler_params=pltpu.CompilerParams(dimension_semantics=("arbitrary",)),
        name="rel_bias_tiles",
    )(rel_bias)


def _group_mean(v, g_ref):
    vb = v.astype(BF16)
    g = g_ref[...]
    half = g.shape[0]
    parts = [_dot(vb[:, s:s + half], g) for s in range(0, v.shape[1], half)]
    return jnp.concatenate(parts, axis=-1)


def _in_kernel(x_ref, mod_ref, n1g_ref, w_ref, qg_ref, kg_ref, g_ref, cw_ref, cb_ref, gng_ref, gnb_ref,
               q_ref, k_ref, v_ref, km_ref, a_ref, abuf, shbuf, ybuf):
    tm = x_ref.shape[1]

    @pl.when(pl.program_id(1) == 0)
    def _():
        abuf[0:HALO, :] = jnp.zeros((HALO, CONV_WIDTH), F32)

    x = x_ref[0]
    ms = jnp.mean(x * x, axis=-1, keepdims=True)
    xn = x * lax.rsqrt(ms + EPS) * n1g_ref[...]
    xn = xn * (1.0 + mod_ref[0, 1:2, :]) + mod_ref[0, 0:1, :]
    xb = xn.astype(BF16)

    def proj(c):
        return _dot(xb, w_ref[:, c * ATTN_WIDTH:(c + 1) * ATTN_WIDTH])

    glu = proj(3) * _sigmoid(proj(4))

    q = proj(0)
    qn = q * lax.rsqrt(_group_mean(q * q, g_ref) + EPS) * qg_ref[...]
    q_ref[0] = (qn * (HEAD_DIM ** -0.5)).astype(BF16)
    k = proj(1)
    kn = k * lax.rsqrt(_group_mean(k * k, g_ref) + EPS) * kg_ref[...]
    k_ref[0] = kn.astype(BF16)
    km_ref[0, 0] = jnp.concatenate(
        [jnp.mean(kn[t * MOBA_BLOCK:(t + 1) * MOBA_BLOCK], axis=0, keepdims=True)
         for t in range(tm // MOBA_BLOCK)], axis=0)
    v_ref[0] = proj(2).astype(BF16)

    abuf[HALO:HALO + tm, :] = glu
    nsh = shbuf.shape[1]
    for ph in range(1, 8):
        shbuf[ph - 1] = abuf[ph:ph + nsh, :]
    rc, lc = 64, 256
    base = HALO - (CONV_KERNEL - 1)
    for r in range(tm // rc):
        for l in range(CONV_WIDTH // lc):
            cols = slice(l * lc, (l + 1) * lc)
            acc = jnp.broadcast_to(cb_ref[:, cols], (rc, lc))
            for j in range(CONV_KERNEL):
                ph = (base + j) % 8
                row = r * rc + base + j - ph
                tap = abuf[row:row + rc, cols] if ph == 0 else shbuf[ph - 1, row:row + rc, cols]
                acc = acc + cw_ref[j:j + 1, cols] * tap
            ybuf[r * rc:(r + 1) * rc, cols] = acc
    abuf[0:HALO, :] = abuf[tm:tm + HALO, :]

    y = ybuf[...]
    yc = y - _group_mean(y, g_ref)
    var = _group_mean(yc * yc, g_ref)
    yn = yc * lax.rsqrt(var + EPS) * gng_ref[...] + gnb_ref[...]
    a_ref[0] = (yn * _sigmoid(yn)).astype(BF16)


def _in_proj(x, mod, n1g, w_in, qg, kg, gmat, cw, cb, gng, gnb):
    bsz, s, d = x.shape
    tm = TM_IN
    nt = s // tm
    nblk = tm // MOBA_BLOCK
    tok = lambda w: pl.BlockSpec((1, tm, w), lambda b, i: (b, i, 0))
    seq = lambda dt: jax.ShapeDtypeStruct((bsz, s, ATTN_WIDTH), dt)
    return pl.pallas_call(
        _in_kernel,
        grid=(bsz, nt),
        in_specs=[tok(d),
                  pl.BlockSpec((1, 6, d), lambda b, i: (b, 0, 0)),
                  _resident((1, d)),
                  _resident(w_in.shape),
                  _resident((1, ATTN_WIDTH)), _resident((1, ATTN_WIDTH)),
                  _resident(gmat.shape),
                  _resident(cw.shape), _resident((1, CONV_WIDTH)),
                  _resident((1, CONV_WIDTH)), _resident((1, CONV_WIDTH))],
        out_specs=[tok(ATTN_WIDTH), tok(ATTN_WIDTH), tok(ATTN_WIDTH),
                   pl.BlockSpec((1, 1, nblk, ATTN_WIDTH), lambda b, i: (b, i, 0, 0)),
                   tok(CONV_WIDTH)],
        out_shape=[seq(BF16), seq(BF16), seq(BF16),
                   jax.ShapeDtypeStruct((bsz, nt, nblk, ATTN_WIDTH), F32),
                   seq(BF16)],
        scratch_shapes=[pltpu.VMEM((tm + HALO, CONV_WIDTH), F32),
                        pltpu.VMEM((7, tm + HALO - 8, CONV_WIDTH), F32),
                        pltpu.VMEM((tm, CONV_WIDTH), F32)],
        compiler_params=pltpu.CompilerParams(dimension_semantics=("arbitrary", "arbitrary"),
                                             vmem_limit_bytes=VMEM_LIMIT),
        name="in_proj_conv",
    )(x, mod, n1g, w_in, qg, kg, gmat, cw, cb, gng, gnb)


def _attn_kernel(rb_ref, q_ref, k_ref, v_ref, km_ref, bias_ref, o_ref, vt_ref, s_ref, selb_ref):
    hp = pl.program_id(0)
    s = q_ref.shape[1]
    nb = s // MOBA_BLOCK
    blk = MOBA_BLOCK

    pad_rows = jnp.where(lax.broadcasted_iota(jnp.int32, (V_ROWS - HEAD_DIM, blk), 0) == 0, 1.0, 0.0).astype(BF16)
    for j in range(nb):
        vt = v_ref[0, j * blk:(j + 1) * blk, :].astype(F32).T.astype(BF16)
        for h in range(2):
            vt_ref[h, j, 0:HEAD_DIM, :] = vt[h * HEAD_DIM:(h + 1) * HEAD_DIM, :]
            vt_ref[h, j, HEAD_DIM:V_ROWS, :] = pad_rows

    km = km_ref[0].astype(BF16)
    lane = lax.broadcasted_iota(jnp.int32, (blk, PAIR), 1)
    rows = lax.broadcasted_iota(jnp.int32, (nb, blk), 0).astype(F32)
    far_bias = [rb_ref[2 * hp + h, N_BUCKETS - 1] for h in range(2)]

    m_scores = None
    m_values = None
    for t in range(nb + 1):
        do_scores, do_values = t < nb, t >= 1
        sbuf = t % 2
        m_values, m_scores = m_scores, [None, None]

        if do_scores:
            qpair = q_ref[0, t * blk:(t + 1) * blk, :]
            qms = [jnp.where((lane >= HEAD_DIM) == (h == 1), qpair, jnp.zeros_like(qpair)) for h in range(2)]
            gated = t > MOBA_TOPK
            if gated:
                for h in range(2):
                    gate = jnp.where(rows < float(t), _dot_nt(km, qms[h]), -jnp.inf)
                    sel = jnp.zeros(gate.shape, jnp.bool_)
                    for _ in range(MOBA_TOPK):
                        mx = jnp.max(gate, axis=0, keepdims=True)
                        idx = jnp.min(jnp.where(gate == mx, rows, float(nb)), axis=0, keepdims=True)
                        pick = rows == idx
                        sel = sel | pick
                        gate = jnp.where(pick, -jnp.inf, gate)
                    selb_ref[h] = jnp.where(sel, 0.0, NEG)

        def block_start(j):
            return j * blk if isinstance(j, int) else pl.multiple_of(j * blk, blk)

        def scores(h, j, extra):
            st = _dot_nt(k_ref[0, pl.ds(block_start(j), blk), :], qms[h]) + extra
            s_ref[sbuf, h, j] = st
            return jnp.max(st, axis=0, keepdims=True)

        def far_scores(j, ms):
            return [jnp.maximum(ms[h], scores(h, j, selb_ref[h, pl.ds(j, 1), :] + far_bias[h])) for h in range(2)]

        def values(js, accs):
            out = []
            for h in range(2):
                acc = accs[h]
                for j in js:
                    p = jnp.exp(s_ref[1 - sbuf, h, j] - m_values[h]).astype(BF16)
                    acc = acc + _dot(vt_ref[h, j], p)
                out.append(acc)
            return out

        n_far = max(t + 1 - N_NEAR, 0) if do_scores else 0
        n_values = t if do_values else 0
        n_loop = (min(n_far, n_values) if do_scores and do_values else max(n_far, n_values)) // LOOP_BLOCKS
        accs = [jnp.zeros((V_ROWS, blk), F32) for _ in range(2)]

        if do_scores:
            for delta in range(min(N_NEAR, t + 1)):
                j = t - delta
                for h in range(2):
                    extra = bias_ref[h, delta]
                    if gated and delta > 0:
                        extra = extra + selb_ref[h, j:j + 1, :]
                    mb = scores(h, j, extra)
                    m_scores[h] = mb if m_scores[h] is None else jnp.maximum(m_scores[h], mb)

        if n_loop > 0:
            def body(g, carry):
                ms, ac = carry
                js = [g * LOOP_BLOCKS + u for u in range(LOOP_BLOCKS)]
                if do_scores:
                    for j in js:
                        ms = far_scores(j, ms)
                if do_values:
                    ac = values(js, ac)
                return ms, ac

            init_m = m_scores if do_scores else [jnp.zeros((1, blk), F32)] * 2
            ms, accs = lax.fori_loop(0, n_loop, body, (init_m, accs))
            if do_scores:
                m_scores = ms
        done = n_loop * LOOP_BLOCKS
        for j in range(done, n_far):
            m_scores = far_scores(j, m_scores)
        if n_values > done:
            accs = values(list(range(done, n_values)), accs)

        if do_values:
            out_t = jnp.concatenate([a[0:HEAD_DIM] / a[HEAD_DIM:HEAD_DIM + 1] for a in accs], axis=0)
            o_ref[0, (t - 1) * blk:t * blk, :] = out_t.T.astype(BF16)


def _attention(rel_bias, q, k, v, kmean, bias):
    bsz, s, _ = q.shape
    nb = s // MOBA_BLOCK
    npair = N_HEADS // 2
    seq = pl.BlockSpec((1, s, PAIR), lambda hp, b: (b, 0, hp))
    return pl.pallas_call(
        _attn_kernel,
        grid=(npair, bsz),
        in_specs=[pl.BlockSpec(memory_space=pltpu.SMEM),
                  seq, seq, seq,
                  pl.BlockSpec((1, nb, PAIR), lambda hp, b: (b, 0, hp)),
                  pl.BlockSpec((2, N_NEAR, MOBA_BLOCK, MOBA_BLOCK), lambda hp, b: (hp, 0, 0, 0))],
        out_specs=seq,
        out_shape=jax.ShapeDtypeStruct((bsz, s, ATTN_WIDTH), BF16),
        scratch_shapes=[pltpu.VMEM((2, nb, V_ROWS, MOBA_BLOCK), BF16),
                        pltpu.VMEM((2, 2, nb, MOBA_BLOCK, MOBA_BLOCK), F32),
                        pltpu.VMEM((2, nb, MOBA_BLOCK), F32)],
        compiler_params=pltpu.CompilerParams(dimension_semantics=("arbitrary", "arbitrary"),
                                             vmem_limit_bytes=VMEM_LIMIT),
        name="moba_attention",
    )(rel_bias, q, k, v, kmean, bias)


def _ffn_kernel(x_ref, ya_ref, a_ref, mod_ref, wo_ref, n2g_ref, wu_ref, fw_ref, fb_ref, wd_ref,
                o_ref, carry_ref, hbuf, xb_ref, act_ref):
    tm = x_ref.shape[1]

    @pl.when(pl.program_id(1) == 0)
    def _():
        carry_ref[...] = jnp.zeros(carry_ref.shape, F32)

    g1 = mod_ref[0, 2:3, :]
    sh2 = mod_ref[0, 3:4, :]
    sc2 = mod_ref[0, 4:5, :]
    g2 = mod_ref[0, 5:6, :]

    ycat = jnp.concatenate([ya_ref[0], a_ref[0]], axis=-1)
    x1 = x_ref[0] + g1 * _dot(ycat, wo_ref[...])
    ms = jnp.mean(x1 * x1, axis=-1, keepdims=True)
    xn = x1 * lax.rsqrt(ms + EPS) * n2g_ref[...]
    xb_ref[...] = (xn * (1.0 + sc2) + sh2).astype(BF16)

    def stage_in(c0, n, slot):
        cols = slice(c0, c0 + n)
        h = _dot(xb_ref[...], wu_ref[:, cols])
        prev = carry_ref[:, cols]
        for t in range(FFN_CONV):
            if t:
                hbuf[slot, t, 0:t, 0:n] = prev[FFN_HALO - t:FFN_HALO]
            hbuf[slot, t, pl.ds(t, tm), 0:n] = h
        carry_ref[:, cols] = h[tm - FFN_HALO:tm, :]

    def conv_out(c0, n, slot):
        cols = slice(c0, c0 + n)
        out = fb_ref[:, cols]
        for t in range(FFN_CONV):
            out = out + fw_ref[FFN_CONV - 1 - t:FFN_CONV - t, cols] * hbuf[slot, t, 0:tm, 0:n]
        return out

    nch = len(FFN_CHUNKS)
    starts = [sum(FFN_CHUNKS[:c]) for c in range(nch)]

    def stage(c):
        stage_in(starts[c], FFN_CHUNKS[c], 2 * (c % 2))
        stage_in(FFN_HIDDEN + starts[c], FFN_CHUNKS[c], 2 * (c % 2) + 1)

    def down(c):
        n = FFN_CHUNKS[c]
        return _dot(act_ref[c % 2, :, 0:n], wd_ref[starts[c]:starts[c] + n, :])

    stage(0)
    acc = jnp.zeros((tm, D_MODEL), F32)
    for c in range(nch):
        if c + 1 < nch:
            stage(c + 1)
        if c >= 1:
            acc = acc + down(c - 1)
        u = conv_out(starts[c], FFN_CHUNKS[c], 2 * (c % 2))
        g = conv_out(FFN_HIDDEN + starts[c], FFN_CHUNKS[c], 2 * (c % 2) + 1)
        act_ref[c % 2, :, 0:FFN_CHUNKS[c]] = (g * _sigmoid(g) * u).astype(BF16)
    acc = acc + down(nch - 1)
    o_ref[0] = x1 + g2 * acc


def _out_ffn(x, y_attn, a, mod, w_out, n2g, w_up, fw, fb, w_down):
    bsz, s, d = x.shape
    tm = TM_FFN
    tok = lambda w: pl.BlockSpec((1, tm, w), lambda b, i: (b, i, 0))
    return pl.pallas_call(
        _ffn_kernel,
        grid=(bsz, s // tm),
        in_specs=[tok(d), tok(ATTN_WIDTH), tok(CONV_WIDTH),
                  pl.BlockSpec((1, 6, d), lambda b, i: (b, 0, 0)),
                  _resident(w_out.shape), _resident((1, d)),
                  _resident(w_up.shape), _resident(fw.shape), _resident(fb.shape),
                  _resident(w_down.shape)],
        out_specs=tok(d),
        out_shape=jax.ShapeDtypeStruct((bsz, s, d), F32),
        scratch_shapes=[pltpu.VMEM((FFN_HALO, 2 * FFN_HIDDEN), F32),
                        pltpu.VMEM((4, FFN_CONV, tm + FFN_HALO, max(FFN_CHUNKS)), F32),
                        pltpu.VMEM((tm, d), BF16),
                        pltpu.VMEM((2, tm, max(FFN_CHUNKS)), BF16)],
        compiler_params=pltpu.CompilerParams(dimension_semantics=("arbitrary", "arbitrary"),
                                             vmem_limit_bytes=VMEM_LIMIT),
        name="out_proj_ffn",
    )(x, y_attn, a, mod, w_out, n2g, w_up, fw, fb, w_down)


def _group_avg_matrix():
    half = 256
    idx = np.arange(half) // (CONV_WIDTH // CONV_GROUPS)
    return jnp.asarray((idx[:, None] == idx[None, :]).astype(np.float32) / HEAD_DIM, dtype=BF16)


def kernel(x, c, rel_bias, ada_w, ada_b, norm1_g, w_in, q_norm_g, k_norm_g, conv_dw_w, conv_dw_b,
           conv_norm_g, conv_norm_b, w_out, norm2_g, w_up, ffn_dw_w, ffn_dw_b, w_down):
    bsz, s, d = x.shape
    depth = ada_w.shape[0]
    bias = _bias_tiles(rel_bias)
    gmat = _group_avg_matrix()
    row = lambda t: t.reshape(1, -1)
    for l in range(depth):
        mod = _modulation(c, ada_w[l], ada_b[l]).reshape(bsz, 6, d)
        q, k, v, kmean, a = _in_proj(
            x, mod, row(norm1_g[l]), w_in[l].astype(BF16),
            row(jnp.tile(q_norm_g[l], N_HEADS)), row(jnp.tile(k_norm_g[l], N_HEADS)), gmat,
            conv_dw_w[l], row(conv_dw_b[l]), row(conv_norm_g[l]), row(conv_norm_b[l]))
        kmean = kmean.reshape(bsz, s // MOBA_BLOCK, ATTN_WIDTH)
        y_attn = _attention(rel_bias, q, k, v, kmean, bias)
        x = _out_ffn(x, y_attn, a, mod, w_out[l].astype(BF16), row(norm2_g[l]), w_up[l].astype(BF16),
                     ffn_dw_w[l], row(ffn_dw_b[l]), w_down[l].astype(BF16))
    return x
```

```python
import functools
import math

import numpy as np
import jax
import jax.numpy as jnp
from jax import lax
from jax.experimental import pallas as pl
from jax.experimental.pallas import tpu as pltpu

F32 = jnp.float32
BF16 = jnp.bfloat16

D_MODEL = 1024
HEAD_DIM = 64
ATTN_WIDTH = 512
N_HEADS = ATTN_WIDTH // HEAD_DIM
CONV_WIDTH = 512
CONV_GROUPS = 8
CONV_KERNEL = 31
MOBA_BLOCK = 256
MOBA_TOPK = 3
N_BUCKETS = 32
MAX_DISTANCE = 1024
FFN_HIDDEN = 2816
FFN_CONV = 3
EPS = 1e-6
NEG = -1e30
LOG2E = math.log2(math.e)

N_NEAR = 5
PAIR = 2 * HEAD_DIM
V_ROWS = HEAD_DIM + 16
LOOP_BLOCKS = 4
HALO = 32
FFN_HALO = 8
FFN_CHUNKS = (512, 512, 512, 512, 512, 256)
assert sum(FFN_CHUNKS) == FFN_HIDDEN
TM_IN = 512
TM_FFN = 512
VMEM_LIMIT = 56 * 1024 * 1024


def _sigmoid(x):
    return 1.0 / (1.0 + jnp.exp(-x))


def _dot(a, b):
    return jnp.dot(a, b, preferred_element_type=F32)


def _dot_nt(a, b):
    return lax.dot_general(a, b, (((1,), (1,)), ((), ())), preferred_element_type=F32)


def _resident(shape):
    zeros = (0,) * len(shape)
    return pl.BlockSpec(shape, lambda *_: zeros, pipeline_mode=pl.Buffered(1))


def _mod_kernel(c_ref, w_ref, b_ref, o_ref):
    c = c_ref[...]
    sc = c * _sigmoid(c)
    o_ref[...] = _dot(sc.astype(BF16), w_ref[...].astype(BF16)) + b_ref[...]


def _modulation(c, ada_w, ada_b):
    bsz, d = c.shape
    n = ada_w.shape[1]
    tn = 1536
    return pl.pallas_call(
        _mod_kernel,
        grid=(n // tn,),
        in_specs=[pl.BlockSpec((bsz, d), lambda j: (0, 0)),
                  pl.BlockSpec((d, tn), lambda j: (0, j)),
                  pl.BlockSpec((1, tn), lambda j: (0, j))],
        out_specs=pl.BlockSpec((bsz, tn), lambda j: (0, j)),
        out_shape=jax.ShapeDtypeStruct((bsz, n), F32),
        compiler_params=pltpu.CompilerParams(dimension_semantics=("arbitrary",),
                                             vmem_limit_bytes=VMEM_LIMIT),
        name="adaln_mod",
    )(c, ada_w, ada_b.reshape(1, n))


def _bucket_range(delta):
    dist = np.arange(delta * MOBA_BLOCK - (MOBA_BLOCK - 1), delta * MOBA_BLOCK + MOBA_BLOCK)
    n = np.maximum(dist, 0)
    max_exact = N_BUCKETS // 2
    large = max_exact + (np.log(np.maximum(n, 1) / max_exact) / math.log(MAX_DISTANCE / max_exact)
                         * (N_BUCKETS - max_exact)).astype(np.int64)
    bucket = np.where(n < max_exact, n, np.minimum(large, N_BUCKETS - 1))
    return max(int(bucket.min()) - 1, 0), min(int(bucket.max()) + 1, N_BUCKETS - 1)


def _bias_kernel(rb_ref, o_ref):
    h = pl.program_id(0)
    kk = lax.broadcasted_iota(jnp.int32, (MOBA_BLOCK, MOBA_BLOCK), 0)
    qq = lax.broadcasted_iota(jnp.int32, (MOBA_BLOCK, MOBA_BLOCK), 1)
    max_exact = N_BUCKETS // 2
    for delta in range(N_NEAR):
        dist = delta * MOBA_BLOCK + qq - kk
        n = jnp.maximum(dist, 0)
        nf = jnp.maximum(n, 1).astype(F32)
        large = max_exact + (jnp.log(nf / max_exact) / math.log(MAX_DISTANCE / max_exact)
                             * (N_BUCKETS - max_exact)).astype(jnp.int32)
        large = jnp.minimum(large, N_BUCKETS - 1)
        bucket = jnp.where(n < max_exact, n, large)
        lo, hi = _bucket_range(delta)
        bias = jnp.full((MOBA_BLOCK, MOBA_BLOCK), rb_ref[h, lo], F32)
        for b in range(lo + 1, hi + 1):
            bias = jnp.where(bucket >= b, rb_ref[h, b], bias)
        if delta == 0:
            bias = jnp.where(dist >= 0, bias, NEG)
        o_ref[0, delta] = bias * LOG2E


def _bias_tiles(rel_bias):
    return pl.pallas_call(
        _bias_kernel,
        grid=(N_HEADS,),
        in_specs=[pl.BlockSpec(memory_space=pltpu.SMEM)],
        out_specs=pl.BlockSpec((1, N_NEAR, MOBA_BLOCK, MOBA_BLOCK), lambda h: (h, 0, 0, 0)),
        out_shape=jax.ShapeDtypeStruct((N_HEADS, N_NEAR, MOBA_BLOCK, MOBA_BLOCK), F32),
        compiler_params=pltpu.CompilerParams(dimension_semantics=("arbitrary",)),
        name="rel_bias_tiles",
    )(rel_bias)


def _group_mean(v, g_ref):
    vb = v.astype(BF16)
    g = g_ref[...]
    half = g.shape[0]
    parts = [_dot(vb[:, s:s + half], g) for s in range(0, v.shape[1], half)]
    return jnp.concatenate(parts, axis=-1)


def _in_kernel(x_ref, mod_ref, n1g_ref, w_ref, qg_ref, kg_ref, g_ref, cw_ref, cb_ref, gng_ref, gnb_ref,
               q_ref, k_ref, v_ref, km_ref, a_ref, abuf, shbuf, ybuf):
    tm = x_ref.shape[1]

    @pl.when(pl.program_id(1) == 0)
    def _():
        abuf[0:HALO, :] = jnp.zeros((HALO, CONV_WIDTH), F32)

    x = x_ref[0]
    ms = jnp.mean(x * x, axis=-1, keepdims=True)
    xn = x * lax.rsqrt(ms + EPS) * n1g_ref[...]
    xn = xn * (1.0 + mod_ref[0, 1:2, :]) + mod_ref[0, 0:1, :]
    xb = xn.astype(BF16)

    def proj(c):
        return _dot(xb, w_ref[:, c * ATTN_WIDTH:(c + 1) * ATTN_WIDTH])

    glu = proj(3) * _sigmoid(proj(4))

    q = proj(0)
    qn = q * lax.rsqrt(_group_mean(q * q, g_ref) + EPS) * qg_ref[...]
    q_ref[0] = (qn * (HEAD_DIM ** -0.5 * LOG2E)).astype(BF16)
    k = proj(1)
    kn = k * lax.rsqrt(_group_mean(k * k, g_ref) + EPS) * kg_ref[...]
    k_ref[0] = kn.astype(BF16)
    km_ref[0, 0] = jnp.concatenate(
        [jnp.mean(kn[t * MOBA_BLOCK:(t + 1) * MOBA_BLOCK], axis=0, keepdims=True)
         for t in range(tm // MOBA_BLOCK)], axis=0)
    v_ref[0] = proj(2).astype(BF16)

    abuf[HALO:HALO + tm, :] = glu
    nsh = shbuf.shape[1]
    for ph in range(1, 8):
        shbuf[ph - 1] = abuf[ph:ph + nsh, :]
    rc, lc = 64, 256
    base = HALO - (CONV_KERNEL - 1)
    for r in range(tm // rc):
        for l in range(CONV_WIDTH // lc):
            cols = slice(l * lc, (l + 1) * lc)
            acc = jnp.broadcast_to(cb_ref[:, cols], (rc, lc))
            for j in range(CONV_KERNEL):
                ph = (base + j) % 8
                row = r * rc + base + j - ph
                tap = abuf[row:row + rc, cols] if ph == 0 else shbuf[ph - 1, row:row + rc, cols]
                acc = acc + cw_ref[j:j + 1, cols] * tap
            ybuf[r * rc:(r + 1) * rc, cols] = acc
    abuf[0:HALO, :] = abuf[tm:tm + HALO, :]

    y = ybuf[...]
    yc = y - _group_mean(y, g_ref)
    var = _group_mean(yc * yc, g_ref)
    yn = yc * lax.rsqrt(var + EPS) * gng_ref[...] + gnb_ref[...]
    a_ref[0] = (yn * _sigmoid(yn)).astype(BF16)


def _in_proj(x, mod, n1g, w_in, qg, kg, gmat, cw, cb, gng, gnb):
    bsz, s, d = x.shape
    tm = TM_IN
    nt = s // tm
    nblk = tm // MOBA_BLOCK
    tok = lambda w: pl.BlockSpec((1, tm, w), lambda b, i: (b, i, 0))
    seq = lambda dt: jax.ShapeDtypeStruct((bsz, s, ATTN_WIDTH), dt)
    return pl.pallas_call(
        _in_kernel,
        grid=(bsz, nt),
        in_specs=[tok(d),
                  pl.BlockSpec((1, 6, d), lambda b, i: (b, 0, 0)),
                  _resident((1, d)),
                  _resident(w_in.shape),
                  _resident((1, ATTN_WIDTH)), _resident((1, ATTN_WIDTH)),
                  _resident(gmat.shape),
                  _resident(cw.shape), _resident((1, CONV_WIDTH)),
                  _resident((1, CONV_WIDTH)), _resident((1, CONV_WIDTH))],
        out_specs=[tok(ATTN_WIDTH), tok(ATTN_WIDTH), tok(ATTN_WIDTH),
                   pl.BlockSpec((1, 1, nblk, ATTN_WIDTH), lambda b, i: (b, i, 0, 0)),
                   tok(CONV_WIDTH)],
        out_shape=[seq(BF16), seq(BF16), seq(BF16),
                   jax.ShapeDtypeStruct((bsz, nt, nblk, ATTN_WIDTH), F32),
                   seq(BF16)],
        scratch_shapes=[pltpu.VMEM((tm + HALO, CONV_WIDTH), F32),
                        pltpu.VMEM((7, tm + HALO - 8, CONV_WIDTH), F32),
                        pltpu.VMEM((tm, CONV_WIDTH), F32)],
        compiler_params=pltpu.CompilerParams(dimension_semantics=("arbitrary", "arbitrary"),
                                             vmem_limit_bytes=VMEM_LIMIT),
        name="in_proj_conv",
    )(x, mod, n1g, w_in, qg, kg, gmat, cw, cb, gng, gnb)


def _attn_kernel(rb_ref, q_ref, k_ref, v_ref, km_ref, bias_ref, o_ref, vt_ref, s_ref, selb_ref):
    hp = pl.program_id(0)
    s = q_ref.shape[1]
    nb = s // MOBA_BLOCK
    blk = MOBA_BLOCK

    pad_rows = jnp.where(lax.broadcasted_iota(jnp.int32, (V_ROWS - HEAD_DIM, blk), 0) == 0, 1.0, 0.0).astype(BF16)
    for j in range(nb):
        vt = v_ref[0, j * blk:(j + 1) * blk, :].astype(F32).T.astype(BF16)
        for h in range(2):
            vt_ref[h, j, 0:HEAD_DIM, :] = vt[h * HEAD_DIM:(h + 1) * HEAD_DIM, :]
            vt_ref[h, j, HEAD_DIM:V_ROWS, :] = pad_rows

    km = km_ref[0].astype(BF16)
    lane = lax.broadcasted_iota(jnp.int32, (blk, PAIR), 1)
    rows = lax.broadcasted_iota(jnp.int32, (nb, blk), 0).astype(F32)
    far_bias = [rb_ref[2 * hp + h, N_BUCKETS - 1] * LOG2E for h in range(2)]

    m_scores = None
    m_values = None
    for t in range(nb + 1):
        do_scores, do_values = t < nb, t >= 1
        sbuf = t % 2
        m_values, m_scores = m_scores, [None, None]

        if do_scores:
            qpair = q_ref[0, t * blk:(t + 1) * blk, :]
            qms = [jnp.where((lane >= HEAD_DIM) == (h == 1), qpair, jnp.zeros_like(qpair)) for h in range(2)]
            gated = t > MOBA_TOPK
            if gated:
                for h in range(2):
                    gate = jnp.where(rows < float(t), _dot_nt(km, qms[h]), -jnp.inf)
                    sel = jnp.zeros(gate.shape, jnp.bool_)
                    for _ in range(MOBA_TOPK):
                        mx = jnp.max(gate, axis=0, keepdims=True)
                        idx = jnp.min(jnp.where(gate == mx, rows, float(nb)), axis=0, keepdims=True)
                        pick = rows == idx
                        sel = sel | pick
                        gate = jnp.where(pick, -jnp.inf, gate)
                    selb_ref[h] = jnp.where(sel, 0.0, NEG)

        def block_start(j):
            return j * blk if isinstance(j, int) else pl.multiple_of(j * blk, blk)

        def scores(h, j, extra):
            st = _dot_nt(k_ref[0, pl.ds(block_start(j), blk), :], qms[h]) + extra
            s_ref[sbuf, h, j] = st
            return jnp.max(st, axis=0, keepdims=True)

        def far_scores(j, ms):
            return [jnp.maximum(ms[h], scores(h, j, selb_ref[h, pl.ds(j, 1), :] + far_bias[h])) for h in range(2)]

        def values(js, accs):
            out = []
            for h in range(2):
                acc = accs[h]
                for j in js:
                    p = jnp.exp2(s_ref[1 - sbuf, h, j] - m_values[h]).astype(BF16)
                    acc = acc + _dot(vt_ref[h, j], p)
                out.append(acc)
            return out

        n_far = max(t + 1 - N_NEAR, 0) if do_scores else 0
        n_values = t if do_values else 0
        n_loop = (min(n_far, n_values) if do_scores and do_values else max(n_far, n_values)) // LOOP_BLOCKS
        accs = [jnp.zeros((V_ROWS, blk), F32) for _ in range(2)]

        if do_scores:
            for delta in range(min(N_NEAR, t + 1)):
                j = t - delta
                for h in range(2):
                    extra = bias_ref[h, delta]
                    if gated and delta > 0:
                        extra = extra + selb_ref[h, j:j + 1, :]
                    mb = scores(h, j, extra)
                    m_scores[h] = mb if m_scores[h] is None else jnp.maximum(m_scores[h], mb)

        if n_loop > 0:
            def body(g, carry):
                ms, ac = carry
                js = [g * LOOP_BLOCKS + u for u in range(LOOP_BLOCKS)]
                if do_scores:
                    for j in js:
                        ms = far_scores(j, ms)
                if do_values:
                    ac = values(js, ac)
                return ms, ac

            init_m = m_scores if do_scores else [jnp.zeros((1, blk), F32)] * 2
            ms, accs = lax.fori_loop(0, n_loop, body, (init_m, accs))
            if do_scores:
                m_scores = ms
        done = n_loop * LOOP_BLOCKS
        for j in range(done, n_far):
            m_scores = far_scores(j, m_scores)
        if n_values > done:
            accs = values(list(range(done, n_values)), accs)

        if do_values:
            out_t = jnp.concatenate([a[0:HEAD_DIM] / a[HEAD_DIM:HEAD_DIM + 1] for a in accs], axis=0)
            o_ref[0, (t - 1) * blk:t * blk, :] = out_t.T.astype(BF16)


def _attention(rel_bias, q, k, v, kmean, bias):
    bsz, s, _ = q.shape
    nb = s // MOBA_BLOCK
    npair = N_HEADS // 2
    seq = pl.BlockSpec((1, s, PAIR), lambda hp, b: (b, 0, hp))
    return pl.pallas_call(
        _attn_kernel,
        grid=(npair, bsz),
        in_specs=[pl.BlockSpec(memory_space=pltpu.SMEM),
                  seq, seq, seq,
                  pl.BlockSpec((1, nb, PAIR), lambda hp, b: (b, 0, hp)),
                  pl.BlockSpec((2, N_NEAR, MOBA_BLOCK, MOBA_BLOCK), lambda hp, b: (hp, 0, 0, 0))],
        out_specs=seq,
        out_shape=jax.ShapeDtypeStruct((bsz, s, ATTN_WIDTH), BF16),
        scratch_shapes=[pltpu.VMEM((2, nb, V_ROWS, MOBA_BLOCK), BF16),
                        pltpu.VMEM((2, 2, nb, MOBA_BLOCK, MOBA_BLOCK), F32),
                        pltpu.VMEM((2, nb, MOBA_BLOCK), F32)],
        compiler_params=pltpu.CompilerParams(dimension_semantics=("arbitrary", "arbitrary"),
                                             vmem_limit_bytes=VMEM_LIMIT),
        name="moba_attention",
    )(rel_bias, q, k, v, kmean, bias)


def _ffn_kernel(x_ref, ya_ref, a_ref, mod_ref, wo_ref, n2g_ref, wu_ref, fw_ref, fb_ref, wd_ref,
                o_ref, carry_ref, hbuf, xb_ref, act_ref):
    tm = x_ref.shape[1]

    @pl.when(pl.program_id(1) == 0)
    def _():
        carry_ref[...] = jnp.zeros(carry_ref.shape, F32)

    g1 = mod_ref[0, 2:3, :]
    sh2 = mod_ref[0, 3:4, :]
    sc2 = mod_ref[0, 4:5, :]
    g2 = mod_ref[0, 5:6, :]

    ycat = jnp.concatenate([ya_ref[0], a_ref[0]], axis=-1)
    x1 = x_ref[0] + g1 * _dot(ycat, wo_ref[...])
    ms = jnp.mean(x1 * x1, axis=-1, keepdims=True)
    xn = x1 * lax.rsqrt(ms + EPS) * n2g_ref[...]
    xb_ref[...] = (xn * (1.0 + sc2) + sh2).astype(BF16)

    def stage_in(c0, n, slot):
        cols = slice(c0, c0 + n)
        h = _dot(xb_ref[...], wu_ref[:, cols])
        prev = carry_ref[:, cols]
        for t in range(FFN_CONV):
            if t:
                hbuf[slot, t, 0:t, 0:n] = prev[FFN_HALO - t:FFN_HALO]
            hbuf[slot, t, pl.ds(t, tm), 0:n] = h
        carry_ref[:, cols] = h[tm - FFN_HALO:tm, :]

    def conv_out(c0, n, slot):
        cols = slice(c0, c0 + n)
        out = fb_ref[:, cols]
        for t in range(FFN_CONV):
            out = out + fw_ref[FFN_CONV - 1 - t:FFN_CONV - t, cols] * hbuf[slot, t, 0:tm, 0:n]
        return out

    nch = len(FFN_CHUNKS)
    starts = [sum(FFN_CHUNKS[:c]) for c in range(nch)]

    def stage(c):
        stage_in(starts[c], FFN_CHUNKS[c], 2 * (c % 2))
        stage_in(FFN_HIDDEN + starts[c], FFN_CHUNKS[c], 2 * (c % 2) + 1)

    def down(c):
        n = FFN_CHUNKS[c]
        return _dot(act_ref[c % 2, :, 0:n], wd_ref[starts[c]:starts[c] + n, :])

    stage(0)
    acc = jnp.zeros((tm, D_MODEL), F32)
    for c in range(nch):
        if c + 1 < nch:
            stage(c + 1)
        if c >= 1:
            acc = acc + down(c - 1)
        u = conv_out(starts[c], FFN_CHUNKS[c], 2 * (c % 2))
        g = conv_out(FFN_HIDDEN + starts[c], FFN_CHUNKS[c], 2 * (c % 2) + 1)
        act_ref[c % 2, :, 0:FFN_CHUNKS[c]] = (g * _sigmoid(g) * u).astype(BF16)
    acc = acc + down(nch - 1)
    o_ref[0] = x1 + g2 * acc


def _out_ffn(x, y_attn, a, mod, w_out, n2g, w_up, fw, fb, w_down):
    bsz, s, d = x.shape
    tm = TM_FFN
    tok = lambda w: pl.BlockSpec((1, tm, w), lambda b, i: (b, i, 0))
    return pl.pallas_call(
        _ffn_kernel,
        grid=(bsz, s // tm),
        in_specs=[tok(d), tok(ATTN_WIDTH), tok(CONV_WIDTH),
                  pl.BlockSpec((1, 6, d), lambda b, i: (b, 0, 0)),
                  _resident(w_out.shape), _resident((1, d)),
                  _resident(w_up.shape), _resident(fw.shape), _resident(fb.shape),
                  _resident(w_down.shape)],
        out_specs=tok(d),
        out_shape=jax.ShapeDtypeStruct((bsz, s, d), F32),
        scratch_shapes=[pltpu.VMEM((FFN_HALO, 2 * FFN_HIDDEN), F32),
                        pltpu.VMEM((4, FFN_CONV, tm + FFN_HALO, max(FFN_CHUNKS)), F32),
                        pltpu.VMEM((tm, d), BF16),
                        pltpu.VMEM((2, tm, max(FFN_CHUNKS)), BF16)],
        compiler_params=pltpu.CompilerParams(dimension_semantics=("arbitrary", "arbitrary"),
                                             vmem_limit_bytes=VMEM_LIMIT),
        name="out_proj_ffn",
    )(x, y_attn, a, mod, w_out, n2g, w_up, fw, fb, w_down)


def _group_avg_matrix():
    half = 256
    idx = np.arange(half) // (CONV_WIDTH // CONV_GROUPS)
    return jnp.asarray((idx[:, None] == idx[None, :]).astype(np.float32) / HEAD_DIM, dtype=BF16)


def kernel(x, c, rel_bias, ada_w, ada_b, norm1_g, w_in, q_norm_g, k_norm_g, conv_dw_w, conv_dw_b,
           conv_norm_g, conv_norm_b, w_out, norm2_g, w_up, ffn_dw_w, ffn_dw_b, w_down):
    bsz, s, d = x.shape
    depth = ada_w.shape[0]
    bias = _bias_tiles(rel_bias)
    gmat = _group_avg_matrix()
    row = lambda t: t.reshape(1, -1)
    for l in range(depth):
        mod = _modulation(c, ada_w[l], ada_b[l]).reshape(bsz, 6, d)
        q, k, v, kmean, a = _in_proj(
            x, mod, row(norm1_g[l]), w_in[l].astype(BF16),
            row(jnp.tile(q_norm_g[l], N_HEADS)), row(jnp.tile(k_norm_g[l], N_HEADS)), gmat,
            conv_dw_w[l], row(conv_dw_b[l]), row(conv_norm_g[l]), row(conv_norm_b[l]))
        kmean = kmean.reshape(bsz, s // MOBA_BLOCK, ATTN_WIDTH)
        y_attn = _attention(rel_bias, q, k, v, kmean, bias)
        x = _out_ffn(x, y_attn, a, mod, w_out[l].astype(BF16), row(norm2_g[l]), w_up[l].astype(BF16),
                     ffn_dw_w[l], row(ffn_dw_b[l]), w_down[l].astype(BF16))
    return x
```

```python
import functools
import math

import numpy as np
import jax
import jax.numpy as jnp
from jax import lax
from jax.experimental import pallas as pl
from jax.experimental.pallas import tpu as pltpu

F32 = jnp.float32
BF16 = jnp.bfloat16

D_MODEL = 1024
HEAD_DIM = 64
ATTN_WIDTH = 512
N_HEADS = ATTN_WIDTH // HEAD_DIM
CONV_WIDTH = 512
CONV_GROUPS = 8
CONV_KERNEL = 31
MOBA_BLOCK = 256
MOBA_TOPK = 3
N_BUCKETS = 32
MAX_DISTANCE = 1024
FFN_HIDDEN = 2816
FFN_CONV = 3
EPS = 1e-6
NEG = -1e30
LOG2E = math.log2(math.e)

N_NEAR = 5
PAIR = 2 * HEAD_DIM
V_ROWS = HEAD_DIM + 16
LOOP_BLOCKS = 4
HALO = 32
FFN_HALO = 8
FFN_CHUNKS = (512, 512, 512, 512, 512, 256)
assert sum(FFN_CHUNKS) == FFN_HIDDEN
TM_IN = 512
TM_FFN = 512
VMEM_LIMIT = 56 * 1024 * 1024


def _sigmoid(x):
    return 1.0 / (1.0 + jnp.exp(-x))


def _dot(a, b):
    return jnp.dot(a, b, preferred_element_type=F32)


def _dot_nt(a, b):
    return lax.dot_general(a, b, (((1,), (1,)), ((), ())), preferred_element_type=F32)


def _resident(shape):
    zeros = (0,) * len(shape)
    return pl.BlockSpec(shape, lambda *_: zeros, pipeline_mode=pl.Buffered(1))


def _mod_kernel(c_ref, w_ref, b_ref, o_ref):
    c = c_ref[...]
    sc = c * _sigmoid(c)
    o_ref[...] = _dot(sc.astype(BF16), w_ref[...].astype(BF16)) + b_ref[...]


def _modulation(c, ada_w, ada_b):
    bsz, d = c.shape
    n = ada_w.shape[1]
    tn = 1536
    return pl.pallas_call(
        _mod_kernel,
        grid=(n // tn,),
        in_specs=[pl.BlockSpec((bsz, d), lambda j: (0, 0)),
                  pl.BlockSpec((d, tn), lambda j: (0, j)),
                  pl.BlockSpec((1, tn), lambda j: (0, j))],
        out_specs=pl.BlockSpec((bsz, tn), lambda j: (0, j)),
        out_shape=jax.ShapeDtypeStruct((bsz, n), F32),
        compiler_params=pltpu.CompilerParams(dimension_semantics=("arbitrary",),
                                             vmem_limit_bytes=VMEM_LIMIT),
        name="adaln_mod",
    )(c, ada_w, ada_b.reshape(1, n))


def _bucket_range(delta):
    dist = np.arange(delta * MOBA_BLOCK - (MOBA_BLOCK - 1), delta * MOBA_BLOCK + MOBA_BLOCK)
    n = np.maximum(dist, 0)
    max_exact = N_BUCKETS // 2
    large = max_exact + (np.log(np.maximum(n, 1) / max_exact) / math.log(MAX_DISTANCE / max_exact)
                         * (N_BUCKETS - max_exact)).astype(np.int64)
    bucket = np.where(n < max_exact, n, np.minimum(large, N_BUCKETS - 1))
    return max(int(bucket.min()) - 1, 0), min(int(bucket.max()) + 1, N_BUCKETS - 1)


def _bias_kernel(rb_ref, o_ref):
    h = pl.program_id(0)
    kk = lax.broadcasted_iota(jnp.int32, (MOBA_BLOCK, MOBA_BLOCK), 0)
    qq = lax.broadcasted_iota(jnp.int32, (MOBA_BLOCK, MOBA_BLOCK), 1)
    max_exact = N_BUCKETS // 2
    for delta in range(N_NEAR):
        dist = delta * MOBA_BLOCK + qq - kk
        n = jnp.maximum(dist, 0)
        nf = jnp.maximum(n, 1).astype(F32)
        large = max_exact + (jnp.log(nf / max_exact) / math.log(MAX_DISTANCE / max_exact)
                             * (N_BUCKETS - max_exact)).astype(jnp.int32)
        large = jnp.minimum(large, N_BUCKETS - 1)
        bucket = jnp.where(n < max_exact, n, large)
        lo, hi = _bucket_range(delta)
        bias = jnp.full((MOBA_BLOCK, MOBA_BLOCK), rb_ref[h, lo], F32)
        for b in range(lo + 1, hi + 1):
            bias = jnp.where(bucket >= b, rb_ref[h, b], bias)
        if delta == 0:
            bias = jnp.where(dist >= 0, bias, NEG)
        o_ref[0, delta] = bias * LOG2E


def _bias_tiles(rel_bias):
    return pl.pallas_call(
        _bias_kernel,
        grid=(N_HEADS,),
        in_specs=[pl.BlockSpec(memory_space=pltpu.SMEM)],
        out_specs=pl.BlockSpec((1, N_NEAR, MOBA_BLOCK, MOBA_BLOCK), lambda h: (h, 0, 0, 0)),
        out_shape=jax.ShapeDtypeStruct((N_HEADS, N_NEAR, MOBA_BLOCK, MOBA_BLOCK), F32),
        compiler_params=pltpu.CompilerParams(dimension_semantics=("arbitrary",)),
        name="rel_bias_tiles",
    )(rel_bias)


def _group_mean(v, g_ref):
    vb = v.astype(BF16)
    g = g_ref[...]
    half = g.shape[0]
    parts = [_dot(vb[:, s:s + half], g) for s in range(0, v.shape[1], half)]
    return jnp.concatenate(parts, axis=-1)


def _in_kernel(x_ref, mod_ref, n1g_ref, w_ref, qg_ref, kg_ref, g_ref, cw_ref, cb_ref, gng_ref, gnb_ref,
               q_ref, k_ref, v_ref, km_ref, a_ref, abuf, shbuf, ybuf):
    tm = x_ref.shape[1]

    @pl.when(pl.program_id(1) == 0)
    def _():
        abuf[0:HALO, :] = jnp.zeros((HALO, CONV_WIDTH), F32)

    x = x_ref[0]
    ms = jnp.mean(x * x, axis=-1, keepdims=True)
    xn = x * lax.rsqrt(ms + EPS) * n1g_ref[...]
    xn = xn * (1.0 + mod_ref[0, 1:2, :]) + mod_ref[0, 0:1, :]
    xb = xn.astype(BF16)

    def proj(c):
        return _dot(xb, w_ref[:, c * ATTN_WIDTH:(c + 1) * ATTN_WIDTH])

    glu = proj(3) * _sigmoid(proj(4))

    q = proj(0)
    qn = q * lax.rsqrt(_group_mean(q * q, g_ref) + EPS) * qg_ref[...]
    q_ref[0] = (qn * (HEAD_DIM ** -0.5 * LOG2E)).astype(BF16)
    k = proj(1)
    kn = k * lax.rsqrt(_group_mean(k * k, g_ref) + EPS) * kg_ref[...]
    k_ref[0] = kn.astype(BF16)
    km_ref[0, 0] = jnp.concatenate(
        [jnp.mean(kn[t * MOBA_BLOCK:(t + 1) * MOBA_BLOCK], axis=0, keepdims=True)
         for t in range(tm // MOBA_BLOCK)], axis=0)
    v_ref[0] = proj(2).astype(BF16)

    abuf[HALO:HALO + tm, :] = glu
    nsh = shbuf.shape[1]
    for ph in range(1, 8):
        shbuf[ph - 1] = abuf[ph:ph + nsh, :]
    rc, lc = 64, 256
    base = HALO - (CONV_KERNEL - 1)
    for r in range(tm // rc):
        for l in range(CONV_WIDTH // lc):
            cols = slice(l * lc, (l + 1) * lc)
            acc = jnp.broadcast_to(cb_ref[:, cols], (rc // 8, 8, lc))
            for j in range(CONV_KERNEL):
                ph = (base + j) % 8
                row = r * rc + base + j - ph
                tap = abuf[row:row + rc, cols] if ph == 0 else shbuf[ph - 1, row:row + rc, cols]
                acc = acc + cw_ref[j, :, cols] * tap.reshape(rc // 8, 8, lc)
            ybuf[r * rc:(r + 1) * rc, cols] = acc.reshape(rc, lc)
    abuf[0:HALO, :] = abuf[tm:tm + HALO, :]

    y = ybuf[...]
    yc = y - _group_mean(y, g_ref)
    var = _group_mean(yc * yc, g_ref)
    yn = yc * lax.rsqrt(var + EPS) * gng_ref[...] + gnb_ref[...]
    a_ref[0] = (yn * _sigmoid(yn)).astype(BF16)


def _in_proj(x, mod, n1g, w_in, qg, kg, gmat, cw, cb, gng, gnb):
    bsz, s, d = x.shape
    tm = TM_IN
    nt = s // tm
    nblk = tm // MOBA_BLOCK
    tok = lambda w: pl.BlockSpec((1, tm, w), lambda b, i: (b, i, 0))
    seq = lambda dt: jax.ShapeDtypeStruct((bsz, s, ATTN_WIDTH), dt)
    return pl.pallas_call(
        _in_kernel,
        grid=(bsz, nt),
        in_specs=[tok(d),
                  pl.BlockSpec((1, 6, d), lambda b, i: (b, 0, 0)),
                  _resident((1, d)),
                  _resident(w_in.shape),
                  _resident((1, ATTN_WIDTH)), _resident((1, ATTN_WIDTH)),
                  _resident(gmat.shape),
                  _resident(cw.shape), _resident((1, CONV_WIDTH)),
                  _resident((1, CONV_WIDTH)), _resident((1, CONV_WIDTH))],
        out_specs=[tok(ATTN_WIDTH), tok(ATTN_WIDTH), tok(ATTN_WIDTH),
                   pl.BlockSpec((1, 1, nblk, ATTN_WIDTH), lambda b, i: (b, i, 0, 0)),
                   tok(CONV_WIDTH)],
        out_shape=[seq(BF16), seq(BF16), seq(BF16),
                   jax.ShapeDtypeStruct((bsz, nt, nblk, ATTN_WIDTH), F32),
                   seq(BF16)],
        scratch_shapes=[pltpu.VMEM((tm + HALO, CONV_WIDTH), F32),
                        pltpu.VMEM((7, tm + HALO - 8, CONV_WIDTH), F32),
                        pltpu.VMEM((tm, CONV_WIDTH), F32)],
        compiler_params=pltpu.CompilerParams(dimension_semantics=("arbitrary", "arbitrary"),
                                             vmem_limit_bytes=VMEM_LIMIT),
        name="in_proj_conv",
    )(x, mod, n1g, w_in, qg, kg, gmat, cw, cb, gng, gnb)


def _attn_kernel(rb_ref, q_ref, k_ref, v_ref, km_ref, bias_ref, o_ref, vt_ref, s_ref, selb_ref):
    hp = pl.program_id(0)
    s = q_ref.shape[1]
    nb = s // MOBA_BLOCK
    blk = MOBA_BLOCK

    pad_rows = jnp.where(lax.broadcasted_iota(jnp.int32, (V_ROWS - HEAD_DIM, blk), 0) == 0, 1.0, 0.0).astype(BF16)
    for j in range(nb):
        vt = v_ref[0, j * blk:(j + 1) * blk, :].astype(F32).T.astype(BF16)
        for h in range(2):
            vt_ref[h, j, 0:HEAD_DIM, :] = vt[h * HEAD_DIM:(h + 1) * HEAD_DIM, :]
            vt_ref[h, j, HEAD_DIM:V_ROWS, :] = pad_rows

    km = km_ref[0].astype(BF16)
    lane = lax.broadcasted_iota(jnp.int32, (blk, PAIR), 1)
    rows = lax.broadcasted_iota(jnp.int32, (nb, blk), 0).astype(F32)
    far_bias = [rb_ref[2 * hp + h, N_BUCKETS - 1] * LOG2E for h in range(2)]

    m_scores = None
    m_values = None
    for t in range(nb + 1):
        do_scores, do_values = t < nb, t >= 1
        sbuf = t % 2
        m_values, m_scores = m_scores, [None, None]

        if do_scores:
            qpair = q_ref[0, t * blk:(t + 1) * blk, :]
            qms = [jnp.where((lane >= HEAD_DIM) == (h == 1), qpair, jnp.zeros_like(qpair)) for h in range(2)]
            gated = t > MOBA_TOPK
            if gated:
                for h in range(2):
                    gate = jnp.where(rows < float(t), _dot_nt(km, qms[h]), -jnp.inf)
                    sel = jnp.zeros(gate.shape, jnp.bool_)
                    for _ in range(MOBA_TOPK):
                        mx = jnp.max(gate, axis=0, keepdims=True)
                        idx = jnp.min(jnp.where(gate == mx, rows, float(nb)), axis=0, keepdims=True)
                        pick = rows == idx
                        sel = sel | pick
                        gate = jnp.where(pick, -jnp.inf, gate)
                    selb_ref[h] = jnp.where(sel, 0.0, NEG)

        def block_start(j):
            return j * blk if isinstance(j, int) else pl.multiple_of(j * blk, blk)

        def scores(h, j, extra):
            st = _dot_nt(k_ref[0, pl.ds(block_start(j), blk), :], qms[h]) + extra
            s_ref[sbuf, h, j] = st
            return jnp.max(st, axis=0, keepdims=True)

        def far_scores(j, ms):
            return [jnp.maximum(ms[h], scores(h, j, selb_ref[h, pl.ds(j, 1), :] + far_bias[h])) for h in range(2)]

        def values(js, accs):
            out = []
            for h in range(2):
                acc = accs[h]
                for j in js:
                    p = jnp.exp2(s_ref[1 - sbuf, h, j] - m_values[h]).astype(BF16)
                    acc = acc + _dot(vt_ref[h, j], p)
                out.append(acc)
            return out

        n_far = max(t + 1 - N_NEAR, 0) if do_scores else 0
        n_values = t if do_values else 0
        n_loop = (min(n_far, n_values) if do_scores and do_values else max(n_far, n_values)) // LOOP_BLOCKS
        accs = [jnp.zeros((V_ROWS, blk), F32) for _ in range(2)]

        if do_scores:
            for delta in range(min(N_NEAR, t + 1)):
                j = t - delta
                for h in range(2):
                    extra = bias_ref[h, delta]
                    if gated and delta > 0:
                        extra = extra + selb_ref[h, j:j + 1, :]
                    mb = scores(h, j, extra)
                    m_scores[h] = mb if m_scores[h] is None else jnp.maximum(m_scores[h], mb)

        if n_loop > 0:
            def body(g, carry):
                ms, ac = carry
                js = [g * LOOP_BLOCKS + u for u in range(LOOP_BLOCKS)]
                if do_scores:
                    for j in js:
                        ms = far_scores(j, ms)
                if do_values:
                    ac = values(js, ac)
                return ms, ac

            init_m = m_scores if do_scores else [jnp.zeros((1, blk), F32)] * 2
            ms, accs = lax.fori_loop(0, n_loop, body, (init_m, accs))
            if do_scores:
                m_scores = ms
        done = n_loop * LOOP_BLOCKS
        for j in range(done, n_far):
            m_scores = far_scores(j, m_scores)
        if n_values > done:
            accs = values(list(range(done, n_values)), accs)

        if do_values:
            out_t = jnp.concatenate([a[0:HEAD_DIM] / a[HEAD_DIM:HEAD_DIM + 1] for a in accs], axis=0)
            o_ref[0, (t - 1) * blk:t * blk, :] = out_t.T.astype(BF16)


def _attention(rel_bias, q, k, v, kmean, bias):
    bsz, s, _ = q.shape
    nb = s // MOBA_BLOCK
    npair = N_HEADS // 2
    seq = pl.BlockSpec((1, s, PAIR), lambda hp, b: (b, 0, hp))
    return pl.pallas_call(
        _attn_kernel,
        grid=(npair, bsz),
        in_specs=[pl.BlockSpec(memory_space=pltpu.SMEM),
                  seq, seq, seq,
                  pl.BlockSpec((1, nb, PAIR), lambda hp, b: (b, 0, hp)),
                  pl.BlockSpec((2, N_NEAR, MOBA_BLOCK, MOBA_BLOCK), lambda hp, b: (hp, 0, 0, 0))],
        out_specs=seq,
        out_shape=jax.ShapeDtypeStruct((bsz, s, ATTN_WIDTH), BF16),
        scratch_shapes=[pltpu.VMEM((2, nb, V_ROWS, MOBA_BLOCK), BF16),
                        pltpu.VMEM((2, 2, nb, MOBA_BLOCK, MOBA_BLOCK), F32),
                        pltpu.VMEM((2, nb, MOBA_BLOCK), F32)],
        compiler_params=pltpu.CompilerParams(dimension_semantics=("arbitrary", "arbitrary"),
                                             vmem_limit_bytes=VMEM_LIMIT),
        name="moba_attention",
    )(rel_bias, q, k, v, kmean, bias)


def _ffn_kernel(x_ref, ya_ref, a_ref, mod_ref, wo_ref, n2g_ref, wu_ref, fw_ref, fb_ref, wd_ref,
                o_ref, carry_ref, hbuf, x1_ref, xb_ref, act_ref):
    tm = x_ref.shape[1]

    @pl.when(pl.program_id(1) == 0)
    def _():
        carry_ref[...] = jnp.zeros(carry_ref.shape, F32)

    g1 = mod_ref[0, 2:3, :]
    sh2 = mod_ref[0, 3:4, :]
    sc2 = mod_ref[0, 4:5, :]
    g2 = mod_ref[0, 5:6, :]

    halves = [slice(r * (tm // 2), (r + 1) * (tm // 2)) for r in range(2)]
    ycat = jnp.concatenate([ya_ref[0], a_ref[0]], axis=-1)
    y_mix = [_dot(ycat[rows], wo_ref[...]) for rows in halves]
    for rows, ym in zip(halves, y_mix):
        x1 = x_ref[0, rows, :] + g1 * ym
        ms = jnp.mean(x1 * x1, axis=-1, keepdims=True)
        xn = x1 * lax.rsqrt(ms + EPS) * n2g_ref[...]
        x1_ref[rows, :] = x1
        xb_ref[rows, :] = (xn * (1.0 + sc2) + sh2).astype(BF16)

    def stage_in(c0, n, slot, row_parts):
        cols = slice(c0, c0 + n)
        prev = carry_ref[:, cols]
        for t in range(1, FFN_CONV):
            hbuf[slot, t, 0:t, 0:n] = prev[FFN_HALO - t:FFN_HALO]
        for rows in row_parts:
            h = _dot(xb_ref[rows, :], wu_ref[:, cols])
            for t in range(FFN_CONV):
                hbuf[slot, t, pl.ds(t + rows.start, rows.stop - rows.start), 0:n] = h
        carry_ref[:, cols] = h[h.shape[0] - FFN_HALO:, :]

    def conv_out(c0, n, slot):
        cols = slice(c0, c0 + n)
        out = fb_ref[:, cols]
        for t in range(FFN_CONV):
            out = out + fw_ref[FFN_CONV - 1 - t:FFN_CONV - t, cols] * hbuf[slot, t, 0:tm, 0:n]
        return out

    nch = len(FFN_CHUNKS)
    starts = [sum(FFN_CHUNKS[:c]) for c in range(nch)]

    def stage(c):
        row_parts = halves if c == 0 else [slice(0, tm)]
        stage_in(starts[c], FFN_CHUNKS[c], 2 * (c % 2), row_parts)
        stage_in(FFN_HIDDEN + starts[c], FFN_CHUNKS[c], 2 * (c % 2) + 1, row_parts)

    def down(c):
        n = FFN_CHUNKS[c]
        return _dot(act_ref[c % 2, :, 0:n], wd_ref[starts[c]:starts[c] + n, :])

    stage(0)
    acc = jnp.zeros((tm, D_MODEL), F32)
    for c in range(nch):
        if c + 1 < nch:
            stage(c + 1)
        if c >= 1:
            acc = acc + down(c - 1)
        u = conv_out(starts[c], FFN_CHUNKS[c], 2 * (c % 2))
        g = conv_out(FFN_HIDDEN + starts[c], FFN_CHUNKS[c], 2 * (c % 2) + 1)
        act_ref[c % 2, :, 0:FFN_CHUNKS[c]] = (g * _sigmoid(g) * u).astype(BF16)
    acc = acc + down(nch - 1)
    o_ref[0] = x1_ref[...] + g2 * acc


def _out_ffn(x, y_attn, a, mod, w_out, n2g, w_up, fw, fb, w_down):
    bsz, s, d = x.shape
    tm = TM_FFN
    tok = lambda w: pl.BlockSpec((1, tm, w), lambda b, i: (b, i, 0))
    return pl.pallas_call(
        _ffn_kernel,
        grid=(bsz, s // tm),
        in_specs=[tok(d), tok(ATTN_WIDTH), tok(CONV_WIDTH),
                  pl.BlockSpec((1, 6, d), lambda b, i: (b, 0, 0)),
                  _resident(w_out.shape), _resident((1, d)),
                  _resident(w_up.shape), _resident(fw.shape), _resident(fb.shape),
                  _resident(w_down.shape)],
        out_specs=tok(d),
        out_shape=jax.ShapeDtypeStruct((bsz, s, d), F32),
        scratch_shapes=[pltpu.VMEM((FFN_HALO, 2 * FFN_HIDDEN), F32),
                        pltpu.VMEM((4, FFN_CONV, tm + FFN_HALO, max(FFN_CHUNKS)), F32),
                        pltpu.VMEM((tm, d), F32),
                        pltpu.VMEM((tm, d), BF16),
                        pltpu.VMEM((2, tm, max(FFN_CHUNKS)), BF16)],
        compiler_params=pltpu.CompilerParams(dimension_semantics=("arbitrary", "arbitrary"),
                                             vmem_limit_bytes=VMEM_LIMIT),
        name="out_proj_ffn",
    )(x, y_attn, a, mod, w_out, n2g, w_up, fw, fb, w_down)


def _group_avg_matrix():
    half = 256
    idx = np.arange(half) // (CONV_WIDTH // CONV_GROUPS)
    return jnp.asarray((idx[:, None] == idx[None, :]).astype(np.float32) / HEAD_DIM, dtype=BF16)


def kernel(x, c, rel_bias, ada_w, ada_b, norm1_g, w_in, q_norm_g, k_norm_g, conv_dw_w, conv_dw_b,
           conv_norm_g, conv_norm_b, w_out, norm2_g, w_up, ffn_dw_w, ffn_dw_b, w_down):
    bsz, s, d = x.shape
    depth = ada_w.shape[0]
    bias = _bias_tiles(rel_bias)
    gmat = _group_avg_matrix()
    row = lambda t: t.reshape(1, -1)
    for l in range(depth):
        mod = _modulation(c, ada_w[l], ada_b[l]).reshape(bsz, 6, d)
        q, k, v, kmean, a = _in_proj(
            x, mod, row(norm1_g[l]), w_in[l].astype(BF16),
            row(jnp.tile(q_norm_g[l], N_HEADS)), row(jnp.tile(k_norm_g[l], N_HEADS)), gmat,
            jnp.broadcast_to(conv_dw_w[l][:, None, :], (CONV_KERNEL, 8, CONV_WIDTH)),
            row(conv_dw_b[l]), row(conv_norm_g[l]), row(conv_norm_b[l]))
        kmean = kmean.reshape(bsz, s // MOBA_BLOCK, ATTN_WIDTH)
        y_attn = _attention(rel_bias, q, k, v, kmean, bias)
        x = _out_ffn(x, y_attn, a, mod, w_out[l].astype(BF16), row(norm2_g[l]), w_up[l].astype(BF16),
                     ffn_dw_w[l], row(ffn_dw_b[l]), w_down[l].astype(BF16))
    return x
```

```python
import functools
import math

import numpy as np
import jax
import jax.numpy as jnp
from jax import lax
from jax.experimental import pallas as pl
from jax.experimental.pallas import tpu as pltpu

F32 = jnp.float32
BF16 = jnp.bfloat16

D_MODEL = 1024
HEAD_DIM = 64
ATTN_WIDTH = 512
N_HEADS = ATTN_WIDTH // HEAD_DIM
CONV_WIDTH = 512
CONV_GROUPS = 8
CONV_KERNEL = 31
MOBA_BLOCK = 256
MOBA_TOPK = 3
N_BUCKETS = 32
MAX_DISTANCE = 1024
FFN_HIDDEN = 2816
FFN_CONV = 3
EPS = 1e-6
NEG = -1e30
LOG2E = math.log2(math.e)

N_NEAR = 5
PAIR = 2 * HEAD_DIM
V_ROWS = HEAD_DIM + 16
LOOP_BLOCKS = 4
HALO = 32
FFN_HALO = 8
FFN_CHUNKS = (512, 512, 512, 512, 512, 256)
assert sum(FFN_CHUNKS) == FFN_HIDDEN
TM_IN = 512
TM_FFN = 512
VMEM_LIMIT = 56 * 1024 * 1024


def _sigmoid(x):
    return 1.0 / (1.0 + jnp.exp(-x))


def _dot(a, b):
    return jnp.dot(a, b, preferred_element_type=F32)


def _dot_nt(a, b):
    return lax.dot_general(a, b, (((1,), (1,)), ((), ())), preferred_element_type=F32)


def _resident(shape):
    zeros = (0,) * len(shape)
    return pl.BlockSpec(shape, lambda *_: zeros, pipeline_mode=pl.Buffered(1))


def _mod_kernel(c_ref, w_ref, b_ref, o_ref):
    c = c_ref[...]
    sc = c * _sigmoid(c)
    o_ref[...] = _dot(sc.astype(BF16), w_ref[...].astype(BF16)) + b_ref[...]


def _modulation(c, ada_w, ada_b):
    bsz, d = c.shape
    n = ada_w.shape[1]
    tn = 1536
    return pl.pallas_call(
        _mod_kernel,
        grid=(n // tn,),
        in_specs=[pl.BlockSpec((bsz, d), lambda j: (0, 0)),
                  pl.BlockSpec((d, tn), lambda j: (0, j)),
                  pl.BlockSpec((1, tn), lambda j: (0, j))],
        out_specs=pl.BlockSpec((bsz, tn), lambda j: (0, j)),
        out_shape=jax.ShapeDtypeStruct((bsz, n), F32),
        compiler_params=pltpu.CompilerParams(dimension_semantics=("arbitrary",),
                                             vmem_limit_bytes=VMEM_LIMIT),
        name="adaln_mod",
    )(c, ada_w, ada_b.reshape(1, n))


def _bucket_range(delta):
    dist = np.arange(delta * MOBA_BLOCK - (MOBA_BLOCK - 1), delta * MOBA_BLOCK + MOBA_BLOCK)
    n = np.maximum(dist, 0)
    max_exact = N_BUCKETS // 2
    large = max_exact + (np.log(np.maximum(n, 1) / max_exact) / math.log(MAX_DISTANCE / max_exact)
                         * (N_BUCKETS - max_exact)).astype(np.int64)
    bucket = np.where(n < max_exact, n, np.minimum(large, N_BUCKETS - 1))
    return max(int(bucket.min()) - 1, 0), min(int(bucket.max()) + 1, N_BUCKETS - 1)


def _bias_kernel(rb_ref, o_ref):
    h = pl.program_id(0)
    kk = lax.broadcasted_iota(jnp.int32, (MOBA_BLOCK, MOBA_BLOCK), 0)
    qq = lax.broadcasted_iota(jnp.int32, (MOBA_BLOCK, MOBA_BLOCK), 1)
    max_exact = N_BUCKETS // 2
    for delta in range(N_NEAR):
        dist = delta * MOBA_BLOCK + qq - kk
        n = jnp.maximum(dist, 0)
        nf = jnp.maximum(n, 1).astype(F32)
        large = max_exact + (jnp.log(nf / max_exact) / math.log(MAX_DISTANCE / max_exact)
                             * (N_BUCKETS - max_exact)).astype(jnp.int32)
        large = jnp.minimum(large, N_BUCKETS - 1)
        bucket = jnp.where(n < max_exact, n, large)
        lo, hi = _bucket_range(delta)
        bias = jnp.full((MOBA_BLOCK, MOBA_BLOCK), rb_ref[h, lo], F32)
        for b in range(lo + 1, hi + 1):
            bias = jnp.where(bucket >= b, rb_ref[h, b], bias)
        if delta == 0:
            bias = jnp.where(dist >= 0, bias, NEG)
        o_ref[0, delta] = bias * LOG2E


def _bias_tiles(rel_bias):
    return pl.pallas_call(
        _bias_kernel,
        grid=(N_HEADS,),
        in_specs=[pl.BlockSpec(memory_space=pltpu.SMEM)],
        out_specs=pl.BlockSpec((1, N_NEAR, MOBA_BLOCK, MOBA_BLOCK), lambda h: (h, 0, 0, 0)),
        out_shape=jax.ShapeDtypeStruct((N_HEADS, N_NEAR, MOBA_BLOCK, MOBA_BLOCK), F32),
        compiler_params=pltpu.CompilerParams(dimension_semantics=("arbitrary",)),
        name="rel_bias_tiles",
    )(rel_bias)


def _group_mean(v, g_ref):
    vb = v.astype(BF16)
    g = g_ref[...]
    half = g.shape[0]
    parts = [_dot(vb[:, s:s + half], g) for s in range(0, v.shape[1], half)]
    return jnp.concatenate(parts, axis=-1)


def _in_kernel(x_ref, mod_ref, n1g_ref, w_ref, qg_ref, kg_ref, g_ref, cw_ref, cb_ref, gng_ref, gnb_ref,
               q_ref, k_ref, v_ref, km_ref, a_ref, abuf, shbuf, ybuf):
    tm = x_ref.shape[1]

    @pl.when(pl.program_id(1) == 0)
    def _():
        abuf[0:HALO, :] = jnp.zeros((HALO, CONV_WIDTH), F32)

    x = x_ref[0]
    ms = jnp.mean(x * x, axis=-1, keepdims=True)
    xn = x * lax.rsqrt(ms + EPS) * n1g_ref[...]
    xn = xn * (1.0 + mod_ref[0, 1:2, :]) + mod_ref[0, 0:1, :]
    xb = xn.astype(BF16)

    def proj(c):
        return _dot(xb, w_ref[:, c * ATTN_WIDTH:(c + 1) * ATTN_WIDTH])

    glu = proj(3) * _sigmoid(proj(4))

    q = proj(0)
    qn = q * lax.rsqrt(_group_mean(q * q, g_ref) + EPS) * qg_ref[...]
    q_ref[0] = (qn * (HEAD_DIM ** -0.5 * LOG2E)).astype(BF16)
    k = proj(1)
    kn = k * lax.rsqrt(_group_mean(k * k, g_ref) + EPS) * kg_ref[...]
    k_ref[0] = kn.astype(BF16)
    km_ref[0, 0] = jnp.concatenate(
        [jnp.mean(kn[t * MOBA_BLOCK:(t + 1) * MOBA_BLOCK], axis=0, keepdims=True)
         for t in range(tm // MOBA_BLOCK)], axis=0)
    v_ref[0] = proj(2).astype(BF16)

    abuf[HALO:HALO + tm, :] = glu
    nsh = shbuf.shape[1]
    for ph in range(1, 8):
        shbuf[ph - 1] = abuf[ph:ph + nsh, :]
    rc, lc = 64, 256
    base = HALO - (CONV_KERNEL - 1)
    for r in range(tm // rc):
        for l in range(CONV_WIDTH // lc):
            cols = slice(l * lc, (l + 1) * lc)
            acc = jnp.broadcast_to(cb_ref[:, cols], (rc // 8, 8, lc))
            for j in range(CONV_KERNEL):
                ph = (base + j) % 8
                row = r * rc + base + j - ph
                tap = abuf[row:row + rc, cols] if ph == 0 else shbuf[ph - 1, row:row + rc, cols]
                acc = acc + cw_ref[j, :, cols] * tap.reshape(rc // 8, 8, lc)
            ybuf[r * rc:(r + 1) * rc, cols] = acc.reshape(rc, lc)
    abuf[0:HALO, :] = abuf[tm:tm + HALO, :]

    y = ybuf[...]
    yc = y - _group_mean(y, g_ref)
    var = _group_mean(yc * yc, g_ref)
    yn = yc * lax.rsqrt(var + EPS) * gng_ref[...] + gnb_ref[...]
    a_ref[0] = (yn * _sigmoid(yn)).astype(BF16)


def _in_proj(x, mod, n1g, w_in, qg, kg, gmat, cw, cb, gng, gnb):
    bsz, s, d = x.shape
    tm = TM_IN
    nt = s // tm
    nblk = tm // MOBA_BLOCK
    tok = lambda w: pl.BlockSpec((1, tm, w), lambda b, i: (b, i, 0))
    seq = lambda dt: jax.ShapeDtypeStruct((bsz, s, ATTN_WIDTH), dt)
    return pl.pallas_call(
        _in_kernel,
        grid=(bsz, nt),
        in_specs=[tok(d),
                  pl.BlockSpec((1, 6, d), lambda b, i: (b, 0, 0)),
                  _resident((1, d)),
                  _resident(w_in.shape),
                  _resident((1, ATTN_WIDTH)), _resident((1, ATTN_WIDTH)),
                  _resident(gmat.shape),
                  _resident(cw.shape), _resident((1, CONV_WIDTH)),
                  _resident((1, CONV_WIDTH)), _resident((1, CONV_WIDTH))],
        out_specs=[tok(ATTN_WIDTH), tok(ATTN_WIDTH), tok(ATTN_WIDTH),
                   pl.BlockSpec((1, 1, nblk, ATTN_WIDTH), lambda b, i: (b, i, 0, 0)),
                   tok(CONV_WIDTH)],
        out_shape=[seq(BF16), seq(BF16), seq(BF16),
                   jax.ShapeDtypeStruct((bsz, nt, nblk, ATTN_WIDTH), F32),
                   seq(BF16)],
        scratch_shapes=[pltpu.VMEM((tm + HALO, CONV_WIDTH), F32),
                        pltpu.VMEM((7, tm + HALO - 8, CONV_WIDTH), F32),
                        pltpu.VMEM((tm, CONV_WIDTH), F32)],
        compiler_params=pltpu.CompilerParams(dimension_semantics=("arbitrary", "arbitrary"),
                                             vmem_limit_bytes=VMEM_LIMIT),
        name="in_proj_conv",
    )(x, mod, n1g, w_in, qg, kg, gmat, cw, cb, gng, gnb)


def _attn_kernel(rb_ref, q_ref, k_ref, v_ref, km_ref, bias_ref, o_ref, vt_ref, s_ref, selb_ref):
    hp = pl.program_id(0)
    s = q_ref.shape[1]
    nb = s // MOBA_BLOCK
    blk = MOBA_BLOCK

    pad_rows = jnp.where(lax.broadcasted_iota(jnp.int32, (V_ROWS - HEAD_DIM, blk), 0) == 0, 1.0, 0.0).astype(BF16)
    for j in range(nb):
        vt = v_ref[0, j * blk:(j + 1) * blk, :].astype(F32).T.astype(BF16)
        for h in range(2):
            vt_ref[h, j, 0:HEAD_DIM, :] = vt[h * HEAD_DIM:(h + 1) * HEAD_DIM, :]
            vt_ref[h, j, HEAD_DIM:V_ROWS, :] = pad_rows

    km = km_ref[0].astype(BF16)
    lane = lax.broadcasted_iota(jnp.int32, (blk, PAIR), 1)
    rows = lax.broadcasted_iota(jnp.int32, (nb, blk), 0).astype(F32)
    far_bias = [rb_ref[2 * hp + h, N_BUCKETS - 1] * LOG2E for h in range(2)]
    eye = jnp.where(lax.broadcasted_iota(jnp.int32, (PAIR, PAIR), 0)
                    == lax.broadcasted_iota(jnp.int32, (PAIR, PAIR), 1), 1.0, 0.0).astype(BF16)

    m_scores = None
    m_values = None
    for t in range(nb + 1):
        do_scores, do_values = t < nb, t >= 1
        sbuf = t % 2
        m_values, m_scores = m_scores, [None, None]

        if do_scores:
            qpair = q_ref[0, t * blk:(t + 1) * blk, :]
            qms = [_dot_nt(eye, jnp.where((lane >= HEAD_DIM) == (h == 1), qpair, jnp.zeros_like(qpair))).astype(BF16)
                   for h in range(2)]
            gated = t > MOBA_TOPK
            if gated:
                for h in range(2):
                    gate = jnp.where(rows < float(t), _dot(km, qms[h]), -jnp.inf)
                    sel = jnp.zeros(gate.shape, jnp.bool_)
                    for _ in range(MOBA_TOPK):
                        mx = jnp.max(gate, axis=0, keepdims=True)
                        idx = jnp.min(jnp.where(gate == mx, rows, float(nb)), axis=0, keepdims=True)
                        pick = rows == idx
                        sel = sel | pick
                        gate = jnp.where(pick, -jnp.inf, gate)
                    selb_ref[h] = jnp.where(sel, 0.0, NEG)

        def block_start(j):
            return j * blk if isinstance(j, int) else pl.multiple_of(j * blk, blk)

        def scores(h, j, extra):
            st = _dot(k_ref[0, pl.ds(block_start(j), blk), :], qms[h]) + extra
            s_ref[sbuf, h, j] = st
            return jnp.max(st, axis=0, keepdims=True)

        def far_scores(j, ms):
            return [jnp.maximum(ms[h], scores(h, j, selb_ref[h, pl.ds(j, 1), :] + far_bias[h])) for h in range(2)]

        def values(js, accs):
            out = []
            for h in range(2):
                acc = accs[h]
                for j in js:
                    p = jnp.exp2(s_ref[1 - sbuf, h, j] - m_values[h]).astype(BF16)
                    acc = acc + _dot(vt_ref[h, j], p)
                out.append(acc)
            return out

        n_far = max(t + 1 - N_NEAR, 0) if do_scores else 0
        n_values = t if do_values else 0
        n_loop = (min(n_far, n_values) if do_scores and do_values else max(n_far, n_values)) // LOOP_BLOCKS
        accs = [jnp.zeros((V_ROWS, blk), F32) for _ in range(2)]

        if do_scores:
            for delta in range(min(N_NEAR, t + 1)):
                j = t - delta
                for h in range(2):
                    extra = bias_ref[h, delta]
                    if gated and delta > 0:
                        extra = extra + selb_ref[h, j:j + 1, :]
                    mb = scores(h, j, extra)
                    m_scores[h] = mb if m_scores[h] is None else jnp.maximum(m_scores[h], mb)

        if n_loop > 0:
            def body(g, carry):
                ms, ac = carry
                js = [g * LOOP_BLOCKS + u for u in range(LOOP_BLOCKS)]
                if do_scores:
                    for j in js:
                        ms = far_scores(j, ms)
                if do_values:
                    ac = values(js, ac)
                return ms, ac

            init_m = m_scores if do_scores else [jnp.zeros((1, blk), F32)] * 2
            ms, accs = lax.fori_loop(0, n_loop, body, (init_m, accs))
            if do_scores:
                m_scores = ms
        done = n_loop * LOOP_BLOCKS
        for j in range(done, n_far):
            m_scores = far_scores(j, m_scores)
        if n_values > done:
            accs = values(list(range(done, n_values)), accs)

        if do_values:
            out_t = jnp.concatenate([a[0:HEAD_DIM] / a[HEAD_DIM:HEAD_DIM + 1] for a in accs], axis=0)
            o_ref[0, (t - 1) * blk:t * blk, :] = out_t.T.astype(BF16)


def _attention(rel_bias, q, k, v, kmean, bias):
    bsz, s, _ = q.shape
    nb = s // MOBA_BLOCK
    npair = N_HEADS // 2
    seq = pl.BlockSpec((1, s, PAIR), lambda hp, b: (b, 0, hp))
    return pl.pallas_call(
        _attn_kernel,
        grid=(npair, bsz),
        in_specs=[pl.BlockSpec(memory_space=pltpu.SMEM),
                  seq, seq, seq,
                  pl.BlockSpec((1, nb, PAIR), lambda hp, b: (b, 0, hp)),
                  pl.BlockSpec((2, N_NEAR, MOBA_BLOCK, MOBA_BLOCK), lambda hp, b: (hp, 0, 0, 0))],
        out_specs=seq,
        out_shape=jax.ShapeDtypeStruct((bsz, s, ATTN_WIDTH), BF16),
        scratch_shapes=[pltpu.VMEM((2, nb, V_ROWS, MOBA_BLOCK), BF16),
                        pltpu.VMEM((2, 2, nb, MOBA_BLOCK, MOBA_BLOCK), F32),
                        pltpu.VMEM((2, nb, MOBA_BLOCK), F32)],
        compiler_params=pltpu.CompilerParams(dimension_semantics=("arbitrary", "arbitrary"),
                                             vmem_limit_bytes=VMEM_LIMIT),
        name="moba_attention",
    )(rel_bias, q, k, v, kmean, bias)


def _ffn_kernel(x_ref, ya_ref, a_ref, mod_ref, wo_ref, n2g_ref, wu_ref, fw_ref, fb_ref, wd_ref,
                o_ref, carry_ref, hbuf, x1_ref, xb_ref, act_ref):
    tm = x_ref.shape[1]

    @pl.when(pl.program_id(1) == 0)
    def _():
        carry_ref[...] = jnp.zeros(carry_ref.shape, F32)

    g1 = mod_ref[0, 2:3, :]
    sh2 = mod_ref[0, 3:4, :]
    sc2 = mod_ref[0, 4:5, :]
    g2 = mod_ref[0, 5:6, :]

    halves = [slice(r * (tm // 2), (r + 1) * (tm // 2)) for r in range(2)]
    ycat = jnp.concatenate([ya_ref[0], a_ref[0]], axis=-1)
    y_mix = [_dot(ycat[rows], wo_ref[...]) for rows in halves]
    for rows, ym in zip(halves, y_mix):
        x1 = x_ref[0, rows, :] + g1 * ym
        ms = jnp.mean(x1 * x1, axis=-1, keepdims=True)
        xn = x1 * lax.rsqrt(ms + EPS) * n2g_ref[...]
        x1_ref[rows, :] = x1
        xb_ref[rows, :] = (xn * (1.0 + sc2) + sh2).astype(BF16)

    def stage_in(c0, n, slot, row_parts):
        cols = slice(c0, c0 + n)
        prev = carry_ref[:, cols]
        for t in range(1, FFN_CONV):
            hbuf[slot, t, 0:t, 0:n] = prev[FFN_HALO - t:FFN_HALO]
        for rows in row_parts:
            h = _dot(xb_ref[rows, :], wu_ref[:, cols])
            for t in range(FFN_CONV):
                hbuf[slot, t, pl.ds(t + rows.start, rows.stop - rows.start), 0:n] = h
        carry_ref[:, cols] = h[h.shape[0] - FFN_HALO:, :]

    def conv_out(c0, n, slot):
        cols = slice(c0, c0 + n)
        out = fb_ref[:, cols]
        for t in range(FFN_CONV):
            out = out + fw_ref[FFN_CONV - 1 - t:FFN_CONV - t, cols] * hbuf[slot, t, 0:tm, 0:n]
        return out

    nch = len(FFN_CHUNKS)
    starts = [sum(FFN_CHUNKS[:c]) for c in range(nch)]

    def stage(c):
        row_parts = halves if c == 0 else [slice(0, tm)]
        stage_in(starts[c], FFN_CHUNKS[c], 2 * (c % 2), row_parts)
        stage_in(FFN_HIDDEN + starts[c], FFN_CHUNKS[c], 2 * (c % 2) + 1, row_parts)

    def down(c):
        n = FFN_CHUNKS[c]
        return _dot(act_ref[c % 2, :, 0:n], wd_ref[starts[c]:starts[c] + n, :])

    stage(0)
    acc = jnp.zeros((tm, D_MODEL), F32)
    for c in range(nch):
        if c + 1 < nch:
            stage(c + 1)
        if c >= 1:
            acc = acc + down(c - 1)
        u = conv_out(starts[c], FFN_CHUNKS[c], 2 * (c % 2))
        g = conv_out(FFN_HIDDEN + starts[c], FFN_CHUNKS[c], 2 * (c % 2) + 1)
        act_ref[c % 2, :, 0:FFN_CHUNKS[c]] = (g * _sigmoid(g) * u).astype(BF16)
    acc = acc + down(nch - 1)
    o_ref[0] = x1_ref[...] + g2 * acc


def _out_ffn(x, y_attn, a, mod, w_out, n2g, w_up, fw, fb, w_down):
    bsz, s, d = x.shape
    tm = TM_FFN
    tok = lambda w: pl.BlockSpec((1, tm, w), lambda b, i: (b, i, 0))
    return pl.pallas_call(
        _ffn_kernel,
        grid=(bsz, s // tm),
        in_specs=[tok(d), tok(ATTN_WIDTH), tok(CONV_WIDTH),
                  pl.BlockSpec((1, 6, d), lambda b, i: (b, 0, 0)),
                  _resident(w_out.shape), _resident((1, d)),
                  _resident(w_up.shape), _resident(fw.shape), _resident(fb.shape),
                  _resident(w_down.shape)],
        out_specs=tok(d),
        out_shape=jax.ShapeDtypeStruct((bsz, s, d), F32),
        scratch_shapes=[pltpu.VMEM((FFN_HALO, 2 * FFN_HIDDEN), F32),
                        pltpu.VMEM((4, FFN_CONV, tm + FFN_HALO, max(FFN_CHUNKS)), F32),
                        pltpu.VMEM((tm, d), F32),
                        pltpu.VMEM((tm, d), BF16),
                        pltpu.VMEM((2, tm, max(FFN_CHUNKS)), BF16)],
        compiler_params=pltpu.CompilerParams(dimension_semantics=("arbitrary", "arbitrary"),
                                             vmem_limit_bytes=VMEM_LIMIT),
        name="out_proj_ffn",
    )(x, y_attn, a, mod, w_out, n2g, w_up, fw, fb, w_down)


def _group_avg_matrix():
    half = 256
    idx = np.arange(half) // (CONV_WIDTH // CONV_GROUPS)
    return jnp.asarray((idx[:, None] == idx[None, :]).astype(np.float32) / HEAD_DIM, dtype=BF16)


def kernel(x, c, rel_bias, ada_w, ada_b, norm1_g, w_in, q_norm_g, k_norm_g, conv_dw_w, conv_dw_b,
           conv_norm_g, conv_norm_b, w_out, norm2_g, w_up, ffn_dw_w, ffn_dw_b, w_down):
    bsz, s, d = x.shape
    depth = ada_w.shape[0]
    bias = _bias_tiles(rel_bias)
    gmat = _group_avg_matrix()
    row = lambda t: t.reshape(1, -1)
    for l in range(depth):
        mod = _modulation(c, ada_w[l], ada_b[l]).reshape(bsz, 6, d)
        q, k, v, kmean, a = _in_proj(
            x, mod, row(norm1_g[l]), w_in[l].astype(BF16),
            row(jnp.tile(q_norm_g[l], N_HEADS)), row(jnp.tile(k_norm_g[l], N_HEADS)), gmat,
            jnp.broadcast_to(conv_dw_w[l][:, None, :], (CONV_KERNEL, 8, CONV_WIDTH)),
            row(conv_dw_b[l]), row(conv_norm_g[l]), row(conv_norm_b[l]))
        kmean = kmean.reshape(bsz, s // MOBA_BLOCK, ATTN_WIDTH)
        y_attn = _attention(rel_bias, q, k, v, kmean, bias)
        x = _out_ffn(x, y_attn, a, mod, w_out[l].astype(BF16), row(norm2_g[l]), w_up[l].astype(BF16),
                     ffn_dw_w[l], row(ffn_dw_b[l]), w_down[l].astype(BF16))
    return x
```

```python
import functools
import math

import numpy as np
import jax
import jax.numpy as jnp
from jax import lax
from jax.experimental import pallas as pl
from jax.experimental.pallas import tpu as pltpu

F32 = jnp.float32
BF16 = jnp.bfloat16

D_MODEL = 1024
HEAD_DIM = 64
ATTN_WIDTH = 512
N_HEADS = ATTN_WIDTH // HEAD_DIM
CONV_WIDTH = 512
CONV_GROUPS = 8
CONV_KERNEL = 31
MOBA_BLOCK = 256
MOBA_TOPK = 3
N_BUCKETS = 32
MAX_DISTANCE = 1024
FFN_HIDDEN = 2816
FFN_CONV = 3
EPS = 1e-6
NEG = -1e30
LOG2E = math.log2(math.e)

N_NEAR = 5
PAIR = 2 * HEAD_DIM
V_ROWS = HEAD_DIM + 16
LOOP_BLOCKS = 4
HALO = 32
FFN_HALO = 8
FFN_CHUNKS = (512, 512, 512, 512, 512, 256)
assert sum(FFN_CHUNKS) == FFN_HIDDEN
TM_IN = 512
TM_FFN = 512
VMEM_LIMIT = 56 * 1024 * 1024


def _sigmoid(x):
    return 1.0 / (1.0 + jnp.exp(-x))


def _dot(a, b):
    return jnp.dot(a, b, preferred_element_type=F32)


def _dot_nt(a, b):
    return lax.dot_general(a, b, (((1,), (1,)), ((), ())), preferred_element_type=F32)


def _resident(shape):
    zeros = (0,) * len(shape)
    return pl.BlockSpec(shape, lambda *_: zeros, pipeline_mode=pl.Buffered(1))


def _mod_kernel(c_ref, w_ref, b_ref, o_ref):
    c = c_ref[...]
    sc = c * _sigmoid(c)
    o_ref[...] = _dot(sc.astype(BF16), w_ref[...].astype(BF16)) + b_ref[...]


def _modulation(c, ada_w, ada_b):
    bsz, d = c.shape
    n = ada_w.shape[1]
    tn = 1536
    return pl.pallas_call(
        _mod_kernel,
        grid=(n // tn,),
        in_specs=[pl.BlockSpec((bsz, d), lambda j: (0, 0)),
                  pl.BlockSpec((d, tn), lambda j: (0, j)),
                  pl.BlockSpec((1, tn), lambda j: (0, j))],
        out_specs=pl.BlockSpec((bsz, tn), lambda j: (0, j)),
        out_shape=jax.ShapeDtypeStruct((bsz, n), F32),
        compiler_params=pltpu.CompilerParams(dimension_semantics=("arbitrary",),
                                             vmem_limit_bytes=VMEM_LIMIT),
        name="adaln_mod",
    )(c, ada_w, ada_b.reshape(1, n))


def _bucket_range(delta):
    dist = np.arange(delta * MOBA_BLOCK - (MOBA_BLOCK - 1), delta * MOBA_BLOCK + MOBA_BLOCK)
    n = np.maximum(dist, 0)
    max_exact = N_BUCKETS // 2
    large = max_exact + (np.log(np.maximum(n, 1) / max_exact) / math.log(MAX_DISTANCE / max_exact)
                         * (N_BUCKETS - max_exact)).astype(np.int64)
    bucket = np.where(n < max_exact, n, np.minimum(large, N_BUCKETS - 1))
    return max(int(bucket.min()) - 1, 0), min(int(bucket.max()) + 1, N_BUCKETS - 1)


def _bias_kernel(rb_ref, o_ref):
    h = pl.program_id(0)
    kk = lax.broadcasted_iota(jnp.int32, (MOBA_BLOCK, MOBA_BLOCK), 0)
    qq = lax.broadcasted_iota(jnp.int32, (MOBA_BLOCK, MOBA_BLOCK), 1)
    max_exact = N_BUCKETS // 2
    for delta in range(N_NEAR):
        dist = delta * MOBA_BLOCK + qq - kk
        n = jnp.maximum(dist, 0)
        nf = jnp.maximum(n, 1).astype(F32)
        large = max_exact + (jnp.log(nf / max_exact) / math.log(MAX_DISTANCE / max_exact)
                             * (N_BUCKETS - max_exact)).astype(jnp.int32)
        large = jnp.minimum(large, N_BUCKETS - 1)
        bucket = jnp.where(n < max_exact, n, large)
        lo, hi = _bucket_range(delta)
        bias = jnp.full((MOBA_BLOCK, MOBA_BLOCK), rb_ref[h, lo], F32)
        for b in range(lo + 1, hi + 1):
            bias = jnp.where(bucket >= b, rb_ref[h, b], bias)
        if delta == 0:
            bias = jnp.where(dist >= 0, bias, NEG)
        o_ref[0, delta] = bias * LOG2E


def _bias_tiles(rel_bias):
    return pl.pallas_call(
        _bias_kernel,
        grid=(N_HEADS,),
        in_specs=[pl.BlockSpec(memory_space=pltpu.SMEM)],
        out_specs=pl.BlockSpec((1, N_NEAR, MOBA_BLOCK, MOBA_BLOCK), lambda h: (h, 0, 0, 0)),
        out_shape=jax.ShapeDtypeStruct((N_HEADS, N_NEAR, MOBA_BLOCK, MOBA_BLOCK), F32),
        compiler_params=pltpu.CompilerParams(dimension_semantics=("arbitrary",)),
        name="rel_bias_tiles",
    )(rel_bias)


def _group_mean(v, g_ref):
    vb = v.astype(BF16)
    g = g_ref[...]
    half = g.shape[0]
    parts = [_dot(vb[:, s:s + half], g) for s in range(0, v.shape[1], half)]
    return jnp.concatenate(parts, axis=-1)


def _in_kernel(x_ref, mod_ref, n1g_ref, w_ref, qg_ref, kg_ref, g_ref, cw_ref, cb_ref, gng_ref, gnb_ref,
               q_ref, k_ref, v_ref, km_ref, a_ref, abuf, shbuf, ybuf):
    tm = x_ref.shape[1]

    @pl.when(pl.program_id(1) == 0)
    def _():
        abuf[0:HALO, :] = jnp.zeros((HALO, CONV_WIDTH), F32)

    x = x_ref[0]
    ms = jnp.mean(x * x, axis=-1, keepdims=True)
    xn = x * lax.rsqrt(ms + EPS) * n1g_ref[...]
    xn = xn * (1.0 + mod_ref[0, 1:2, :]) + mod_ref[0, 0:1, :]
    xb = xn.astype(BF16)

    def proj(c):
        return _dot(xb, w_ref[:, c * ATTN_WIDTH:(c + 1) * ATTN_WIDTH])

    glu = proj(3) * _sigmoid(proj(4))

    def store_pairs(ref, val):
        for p in range(ATTN_WIDTH // PAIR):
            ref[0, p] = val[:, p * PAIR:(p + 1) * PAIR]

    q = proj(0)
    qn = q * lax.rsqrt(_group_mean(q * q, g_ref) + EPS) * qg_ref[...]
    store_pairs(q_ref, (qn * (HEAD_DIM ** -0.5 * LOG2E)).astype(BF16))
    k = proj(1)
    kn = k * lax.rsqrt(_group_mean(k * k, g_ref) + EPS) * kg_ref[...]
    store_pairs(k_ref, kn.astype(BF16))
    km_ref[0, 0] = jnp.concatenate(
        [jnp.mean(kn[t * MOBA_BLOCK:(t + 1) * MOBA_BLOCK], axis=0, keepdims=True)
         for t in range(tm // MOBA_BLOCK)], axis=0)
    store_pairs(v_ref, proj(2).astype(BF16))

    abuf[HALO:HALO + tm, :] = glu
    nsh = shbuf.shape[1]
    for ph in range(1, 8):
        shbuf[ph - 1] = abuf[ph:ph + nsh, :]
    rc, lc = 64, 256
    base = HALO - (CONV_KERNEL - 1)
    for r in range(tm // rc):
        for l in range(CONV_WIDTH // lc):
            cols = slice(l * lc, (l + 1) * lc)
            acc = jnp.broadcast_to(cb_ref[:, cols], (rc // 8, 8, lc))
            for j in range(CONV_KERNEL):
                ph = (base + j) % 8
                row = r * rc + base + j - ph
                tap = abuf[row:row + rc, cols] if ph == 0 else shbuf[ph - 1, row:row + rc, cols]
                acc = acc + cw_ref[j, :, cols] * tap.reshape(rc // 8, 8, lc)
            ybuf[r * rc:(r + 1) * rc, cols] = acc.reshape(rc, lc)
    abuf[0:HALO, :] = abuf[tm:tm + HALO, :]

    y = ybuf[...]
    yc = y - _group_mean(y, g_ref)
    var = _group_mean(yc * yc, g_ref)
    yn = yc * lax.rsqrt(var + EPS) * gng_ref[...] + gnb_ref[...]
    a_ref[0] = (yn * _sigmoid(yn)).astype(BF16)


def _in_proj(x, mod, n1g, w_in, qg, kg, gmat, cw, cb, gng, gnb):
    bsz, s, d = x.shape
    tm = TM_IN
    nt = s // tm
    nblk = tm // MOBA_BLOCK
    tok = lambda w: pl.BlockSpec((1, tm, w), lambda b, i: (b, i, 0))
    npair = ATTN_WIDTH // PAIR
    pairs = pl.BlockSpec((1, npair, tm, PAIR), lambda b, i: (b, 0, i, 0))
    pair_seq = jax.ShapeDtypeStruct((bsz, npair, s, PAIR), BF16)
    return pl.pallas_call(
        _in_kernel,
        grid=(bsz, nt),
        in_specs=[tok(d),
                  pl.BlockSpec((1, 6, d), lambda b, i: (b, 0, 0)),
                  _resident((1, d)),
                  _resident(w_in.shape),
                  _resident((1, ATTN_WIDTH)), _resident((1, ATTN_WIDTH)),
                  _resident(gmat.shape),
                  _resident(cw.shape), _resident((1, CONV_WIDTH)),
                  _resident((1, CONV_WIDTH)), _resident((1, CONV_WIDTH))],
        out_specs=[pairs, pairs, pairs,
                   pl.BlockSpec((1, 1, nblk, ATTN_WIDTH), lambda b, i: (b, i, 0, 0)),
                   tok(CONV_WIDTH)],
        out_shape=[pair_seq, pair_seq, pair_seq,
                   jax.ShapeDtypeStruct((bsz, nt, nblk, ATTN_WIDTH), F32),
                   jax.ShapeDtypeStruct((bsz, s, CONV_WIDTH), BF16)],
        scratch_shapes=[pltpu.VMEM((tm + HALO, CONV_WIDTH), F32),
                        pltpu.VMEM((7, tm + HALO - 8, CONV_WIDTH), F32),
                        pltpu.VMEM((tm, CONV_WIDTH), F32)],
        compiler_params=pltpu.CompilerParams(dimension_semantics=("arbitrary", "arbitrary"),
                                             vmem_limit_bytes=VMEM_LIMIT),
        name="in_proj_conv",
    )(x, mod, n1g, w_in, qg, kg, gmat, cw, cb, gng, gnb)


def _attn_kernel(rb_ref, q_ref, k_ref, v_ref, km_ref, bias_ref, o_ref, vt_ref, s_ref, selb_ref):
    hp = pl.program_id(0)
    s = q_ref.shape[2]
    nb = s // MOBA_BLOCK
    blk = MOBA_BLOCK

    pad_rows = jnp.where(lax.broadcasted_iota(jnp.int32, (V_ROWS - HEAD_DIM, blk), 0) == 0, 1.0, 0.0).astype(BF16)
    for j in range(nb):
        vt = v_ref[0, 0, j * blk:(j + 1) * blk, :].astype(F32).T.astype(BF16)
        for h in range(2):
            vt_ref[h, j, 0:HEAD_DIM, :] = vt[h * HEAD_DIM:(h + 1) * HEAD_DIM, :]
            vt_ref[h, j, HEAD_DIM:V_ROWS, :] = pad_rows

    km = km_ref[0].astype(BF16)
    lane = lax.broadcasted_iota(jnp.int32, (blk, PAIR), 1)
    rows = lax.broadcasted_iota(jnp.int32, (nb, blk), 0).astype(F32)
    far_bias = [rb_ref[2 * hp + h, N_BUCKETS - 1] * LOG2E for h in range(2)]

    m_scores = None
    m_values = None
    for t in range(nb + 1):
        do_scores, do_values = t < nb, t >= 1
        sbuf = t % 2
        m_values, m_scores = m_scores, [None, None]

        if do_scores:
            qpair = q_ref[0, 0, t * blk:(t + 1) * blk, :]
            qms = [jnp.where((lane >= HEAD_DIM) == (h == 1), qpair, jnp.zeros_like(qpair)) for h in range(2)]
            gated = t > MOBA_TOPK
            if gated:
                for h in range(2):
                    gate = jnp.where(rows < float(t), _dot_nt(km, qms[h]), -jnp.inf)
                    sel = jnp.zeros(gate.shape, jnp.bool_)
                    for _ in range(MOBA_TOPK):
                        mx = jnp.max(gate, axis=0, keepdims=True)
                        idx = jnp.min(jnp.where(gate == mx, rows, float(nb)), axis=0, keepdims=True)
                        pick = rows == idx
                        sel = sel | pick
                        gate = jnp.where(pick, -jnp.inf, gate)
                    selb_ref[h] = jnp.where(sel, 0.0, NEG)

        def block_start(j):
            return j * blk if isinstance(j, int) else pl.multiple_of(j * blk, blk)

        def scores(h, j, extra):
            st = _dot_nt(k_ref[0, 0, pl.ds(block_start(j), blk), :], qms[h]) + extra
            s_ref[sbuf, h, j] = st
            return jnp.max(st, axis=0, keepdims=True)

        def far_scores(j, ms):
            return [jnp.maximum(ms[h], scores(h, j, selb_ref[h, pl.ds(j, 1), :] + far_bias[h])) for h in range(2)]

        def values(js, accs):
            out = []
            for h in range(2):
                acc = accs[h]
                for j in js:
                    p = jnp.exp2(s_ref[1 - sbuf, h, j] - m_values[h]).astype(BF16)
                    acc = acc + _dot(vt_ref[h, j], p)
                out.append(acc)
            return out

        n_far = max(t + 1 - N_NEAR, 0) if do_scores else 0
        n_values = t if do_values else 0
        n_loop = (min(n_far, n_values) if do_scores and do_values else max(n_far, n_values)) // LOOP_BLOCKS
        accs = [jnp.zeros((V_ROWS, blk), F32) for _ in range(2)]

        if do_scores:
            for delta in range(min(N_NEAR, t + 1)):
                j = t - delta
                for h in range(2):
                    extra = bias_ref[h, delta]
                    if gated and delta > 0:
                        extra = extra + selb_ref[h, j:j + 1, :]
                    mb = scores(h, j, extra)
                    m_scores[h] = mb if m_scores[h] is None else jnp.maximum(m_scores[h], mb)

        if n_loop > 0:
            def body(g, carry):
                ms, ac = carry
                js = [g * LOOP_BLOCKS + u for u in range(LOOP_BLOCKS)]
                if do_scores:
                    for j in js:
                        ms = far_scores(j, ms)
                if do_values:
                    ac = values(js, ac)
                return ms, ac

            init_m = m_scores if do_scores else [jnp.zeros((1, blk), F32)] * 2
            ms, accs = lax.fori_loop(0, n_loop, body, (init_m, accs))
            if do_scores:
                m_scores = ms
        done = n_loop * LOOP_BLOCKS
        for j in range(done, n_far):
            m_scores = far_scores(j, m_scores)
        if n_values > done:
            accs = values(list(range(done, n_values)), accs)

        if do_values:
            out_t = jnp.concatenate([a[0:HEAD_DIM] / a[HEAD_DIM:HEAD_DIM + 1] for a in accs], axis=0)
            o_ref[0, 0, (t - 1) * blk:t * blk, :] = out_t.T.astype(BF16)


def _attention(rel_bias, q, k, v, kmean, bias):
    bsz, npair, s, _ = q.shape
    nb = s // MOBA_BLOCK
    seq = pl.BlockSpec((1, 1, s, PAIR), lambda hp, b: (b, hp, 0, 0))
    return pl.pallas_call(
        _attn_kernel,
        grid=(npair, bsz),
        in_specs=[pl.BlockSpec(memory_space=pltpu.SMEM),
                  seq, seq, seq,
                  pl.BlockSpec((1, nb, PAIR), lambda hp, b: (b, 0, hp)),
                  pl.BlockSpec((2, N_NEAR, MOBA_BLOCK, MOBA_BLOCK), lambda hp, b: (hp, 0, 0, 0))],
        out_specs=seq,
        out_shape=jax.ShapeDtypeStruct((bsz, npair, s, PAIR), BF16),
        scratch_shapes=[pltpu.VMEM((2, nb, V_ROWS, MOBA_BLOCK), BF16),
                        pltpu.VMEM((2, 2, nb, MOBA_BLOCK, MOBA_BLOCK), F32),
                        pltpu.VMEM((2, nb, MOBA_BLOCK), F32)],
        compiler_params=pltpu.CompilerParams(dimension_semantics=("arbitrary", "arbitrary"),
                                             vmem_limit_bytes=VMEM_LIMIT),
        name="moba_attention",
    )(rel_bias, q, k, v, kmean, bias)


def _ffn_kernel(x_ref, ya_ref, a_ref, mod_ref, wo_ref, n2g_ref, wu_ref, fw_ref, fb_ref, wd_ref,
                o_ref, carry_ref, hbuf, x1_ref, xb_ref, act_ref):
    tm = x_ref.shape[1]

    @pl.when(pl.program_id(1) == 0)
    def _():
        carry_ref[...] = jnp.zeros(carry_ref.shape, F32)

    g1 = mod_ref[0, 2:3, :]
    sh2 = mod_ref[0, 3:4, :]
    sc2 = mod_ref[0, 4:5, :]
    g2 = mod_ref[0, 5:6, :]

    halves = [slice(r * (tm // 2), (r + 1) * (tm // 2)) for r in range(2)]
    ycat = jnp.concatenate([ya_ref[0, p] for p in range(ya_ref.shape[1])] + [a_ref[0]], axis=-1)
    y_mix = [_dot(ycat[rows], wo_ref[...]) for rows in halves]
    for rows, ym in zip(halves, y_mix):
        x1 = x_ref[0, rows, :] + g1 * ym
        ms = jnp.mean(x1 * x1, axis=-1, keepdims=True)
        xn = x1 * lax.rsqrt(ms + EPS) * n2g_ref[...]
        x1_ref[rows, :] = x1
        xb_ref[rows, :] = (xn * (1.0 + sc2) + sh2).astype(BF16)

    def stage_in(c0, n, slot, row_parts):
        cols = slice(c0, c0 + n)
        prev = carry_ref[:, cols]
        for t in range(1, FFN_CONV):
            hbuf[slot, t, 0:t, 0:n] = prev[FFN_HALO - t:FFN_HALO]
        for rows in row_parts:
            h = _dot(xb_ref[rows, :], wu_ref[:, cols])
            for t in range(FFN_CONV):
                hbuf[slot, t, pl.ds(t + rows.start, rows.stop - rows.start), 0:n] = h
        carry_ref[:, cols] = h[h.shape[0] - FFN_HALO:, :]

    def conv_out(c0, n, slot):
        cols = slice(c0, c0 + n)
        out = fb_ref[:, cols]
        for t in range(FFN_CONV):
            out = out + fw_ref[FFN_CONV - 1 - t:FFN_CONV - t, cols] * hbuf[slot, t, 0:tm, 0:n]
        return out

    nch = len(FFN_CHUNKS)
    starts = [sum(FFN_CHUNKS[:c]) for c in range(nch)]

    def stage(c):
        row_parts = halves if c == 0 else [slice(0, tm)]
        stage_in(starts[c], FFN_CHUNKS[c], 2 * (c % 2), row_parts)
        stage_in(FFN_HIDDEN + starts[c], FFN_CHUNKS[c], 2 * (c % 2) + 1, row_parts)

    def down(c):
        n = FFN_CHUNKS[c]
        return _dot(act_ref[c % 2, :, 0:n], wd_ref[starts[c]:starts[c] + n, :])

    stage(0)
    acc = jnp.zeros((tm, D_MODEL), F32)
    for c in range(nch):
        if c + 1 < nch:
            stage(c + 1)
        if c >= 1:
            acc = acc + down(c - 1)
        u = conv_out(starts[c], FFN_CHUNKS[c], 2 * (c % 2))
        g = conv_out(FFN_HIDDEN + starts[c], FFN_CHUNKS[c], 2 * (c % 2) + 1)
        act_ref[c % 2, :, 0:FFN_CHUNKS[c]] = (g * _sigmoid(g) * u).astype(BF16)
    acc = acc + down(nch - 1)
    o_ref[0] = x1_ref[...] + g2 * acc


def _out_ffn(x, y_attn, a, mod, w_out, n2g, w_up, fw, fb, w_down):
    bsz, s, d = x.shape
    tm = TM_FFN
    tok = lambda w: pl.BlockSpec((1, tm, w), lambda b, i: (b, i, 0))
    return pl.pallas_call(
        _ffn_kernel,
        grid=(bsz, s // tm),
        in_specs=[tok(d),
                  pl.BlockSpec((1, y_attn.shape[1], tm, PAIR), lambda b, i: (b, 0, i, 0)),
                  tok(CONV_WIDTH),
                  pl.BlockSpec((1, 6, d), lambda b, i: (b, 0, 0)),
                  _resident(w_out.shape), _resident((1, d)),
                  _resident(w_up.shape), _resident(fw.shape), _resident(fb.shape),
                  _resident(w_down.shape)],
        out_specs=tok(d),
        out_shape=jax.ShapeDtypeStruct((bsz, s, d), F32),
        scratch_shapes=[pltpu.VMEM((FFN_HALO, 2 * FFN_HIDDEN), F32),
                        pltpu.VMEM((4, FFN_CONV, tm + FFN_HALO, max(FFN_CHUNKS)), F32),
                        pltpu.VMEM((tm, d), F32),
                        pltpu.VMEM((tm, d), BF16),
                        pltpu.VMEM((2, tm, max(FFN_CHUNKS)), BF16)],
        compiler_params=pltpu.CompilerParams(dimension_semantics=("arbitrary", "arbitrary"),
                                             vmem_limit_bytes=VMEM_LIMIT),
        name="out_proj_ffn",
    )(x, y_attn, a, mod, w_out, n2g, w_up, fw, fb, w_down)


def _group_avg_matrix():
    half = 256
    idx = np.arange(half) // (CONV_WIDTH // CONV_GROUPS)
    return jnp.asarray((idx[:, None] == idx[None, :]).astype(np.float32) / HEAD_DIM, dtype=BF16)


def kernel(x, c, rel_bias, ada_w, ada_b, norm1_g, w_in, q_norm_g, k_norm_g, conv_dw_w, conv_dw_b,
           conv_norm_g, conv_norm_b, w_out, norm2_g, w_up, ffn_dw_w, ffn_dw_b, w_down):
    bsz, s, d = x.shape
    depth = ada_w.shape[0]
    bias = _bias_tiles(rel_bias)
    gmat = _group_avg_matrix()
    row = lambda t: t.reshape(1, -1)
    for l in range(depth):
        mod = _modulation(c, ada_w[l], ada_b[l]).reshape(bsz, 6, d)
        q, k, v, kmean, a = _in_proj(
            x, mod, row(norm1_g[l]), w_in[l].astype(BF16),
            row(jnp.tile(q_norm_g[l], N_HEADS)), row(jnp.tile(k_norm_g[l], N_HEADS)), gmat,
            jnp.broadcast_to(conv_dw_w[l][:, None, :], (CONV_KERNEL, 8, CONV_WIDTH)),
            row(conv_dw_b[l]), row(conv_norm_g[l]), row(conv_norm_b[l]))
        kmean = kmean.reshape(bsz, s // MOBA_BLOCK, ATTN_WIDTH)
        y_attn = _attention(rel_bias, q, k, v, kmean, bias)
        x = _out_ffn(x, y_attn, a, mod, w_out[l].astype(BF16), row(norm2_g[l]), w_up[l].astype(BF16),
                     ffn_dw_w[l], row(ffn_dw_b[l]), w_down[l].astype(BF16))
    return x
```

```python
import functools
import math

import numpy as np
import jax
import jax.numpy as jnp
from jax import lax
from jax.experimental import pallas as pl
from jax.experimental.pallas import tpu as pltpu

F32 = jnp.float32
BF16 = jnp.bfloat16

D_MODEL = 1024
HEAD_DIM = 64
ATTN_WIDTH = 512
N_HEADS = ATTN_WIDTH // HEAD_DIM
CONV_WIDTH = 512
CONV_GROUPS = 8
CONV_KERNEL = 31
MOBA_BLOCK = 256
MOBA_TOPK = 3
N_BUCKETS = 32
MAX_DISTANCE = 1024
FFN_HIDDEN = 2816
FFN_CONV = 3
EPS = 1e-6
NEG = -1e30
LOG2E = math.log2(math.e)

N_NEAR = 5
PAIR = 2 * HEAD_DIM
V_ROWS = HEAD_DIM + 16
LOOP_BLOCKS = 4
HALO = 32
FFN_HALO = 8
FFN_CHUNKS = (512, 512, 512, 512, 512, 256)
assert sum(FFN_CHUNKS) == FFN_HIDDEN
TM_IN = 512
TM_FFN = 512
VMEM_LIMIT = 56 * 1024 * 1024


def _sigmoid(x):
    return 1.0 / (1.0 + jnp.exp(-x))


def _dot(a, b):
    return jnp.dot(a, b, preferred_element_type=F32)


def _dot_nt(a, b):
    return lax.dot_general(a, b, (((1,), (1,)), ((), ())), preferred_element_type=F32)


def _resident(shape):
    zeros = (0,) * len(shape)
    return pl.BlockSpec(shape, lambda *_: zeros, pipeline_mode=pl.Buffered(1))


def _mod_kernel(c_ref, w_ref, b_ref, o_ref):
    c = c_ref[...]
    sc = c * _sigmoid(c)
    o_ref[...] = _dot(sc.astype(BF16), w_ref[...].astype(BF16)) + b_ref[...]


def _modulation(c, ada_w, ada_b):
    bsz, d = c.shape
    n = ada_w.shape[1]
    tn = 1536
    return pl.pallas_call(
        _mod_kernel,
        grid=(n // tn,),
        in_specs=[pl.BlockSpec((bsz, d), lambda j: (0, 0)),
                  pl.BlockSpec((d, tn), lambda j: (0, j)),
                  pl.BlockSpec((1, tn), lambda j: (0, j))],
        out_specs=pl.BlockSpec((bsz, tn), lambda j: (0, j)),
        out_shape=jax.ShapeDtypeStruct((bsz, n), F32),
        compiler_params=pltpu.CompilerParams(dimension_semantics=("arbitrary",),
                                             vmem_limit_bytes=VMEM_LIMIT),
        name="adaln_mod",
    )(c, ada_w, ada_b.reshape(1, n))


def _bucket_range(delta):
    dist = np.arange(delta * MOBA_BLOCK - (MOBA_BLOCK - 1), delta * MOBA_BLOCK + MOBA_BLOCK)
    n = np.maximum(dist, 0)
    max_exact = N_BUCKETS // 2
    large = max_exact + (np.log(np.maximum(n, 1) / max_exact) / math.log(MAX_DISTANCE / max_exact)
                         * (N_BUCKETS - max_exact)).astype(np.int64)
    bucket = np.where(n < max_exact, n, np.minimum(large, N_BUCKETS - 1))
    return max(int(bucket.min()) - 1, 0), min(int(bucket.max()) + 1, N_BUCKETS - 1)


def _bias_kernel(rb_ref, o_ref):
    h = pl.program_id(0)
    blk = MOBA_BLOCK
    kk = lax.broadcasted_iota(jnp.int32, (blk, blk), 0)
    qq = lax.broadcasted_iota(jnp.int32, (blk, blk), 1)
    col = lax.broadcasted_iota(jnp.int32, (8, 2 * blk), 1)
    max_exact = N_BUCKETS // 2
    for delta in range(N_NEAR):
        dist = delta * blk + col - (blk - 1)
        n = jnp.maximum(dist, 0)
        nf = jnp.maximum(n, 1).astype(F32)
        large = max_exact + (jnp.log(nf / max_exact) / math.log(MAX_DISTANCE / max_exact)
                             * (N_BUCKETS - max_exact)).astype(jnp.int32)
        large = jnp.minimum(large, N_BUCKETS - 1)
        bucket = jnp.where(n < max_exact, n, large)
        lo, hi = _bucket_range(delta)
        line = jnp.full(col.shape, rb_ref[h, lo], F32)
        for b in range(lo + 1, hi + 1):
            line = jnp.where(bucket >= b, rb_ref[h, b], line)
        lines = jnp.broadcast_to(line[0:1, :], (blk, 2 * blk))
        bias = pltpu.roll(lines, blk + 1, 1, stride=1, stride_axis=0)[:, 0:blk]
        if delta == 0:
            bias = jnp.where(qq >= kk, bias, NEG)
        o_ref[0, delta] = bias * LOG2E


def _bias_tiles(rel_bias):
    return pl.pallas_call(
        _bias_kernel,
        grid=(N_HEADS,),
        in_specs=[pl.BlockSpec(memory_space=pltpu.SMEM)],
        out_specs=pl.BlockSpec((1, N_NEAR, MOBA_BLOCK, MOBA_BLOCK), lambda h: (h, 0, 0, 0)),
        out_shape=jax.ShapeDtypeStruct((N_HEADS, N_NEAR, MOBA_BLOCK, MOBA_BLOCK), F32),
        compiler_params=pltpu.CompilerParams(dimension_semantics=("arbitrary",)),
        name="rel_bias_tiles",
    )(rel_bias)


def _group_mean(v, g_ref):
    vb = v.astype(BF16)
    g = g_ref[...]
    half = g.shape[0]
    parts = [_dot(vb[:, s:s + half], g) for s in range(0, v.shape[1], half)]
    return jnp.concatenate(parts, axis=-1)


def _in_kernel(x_ref, mod_ref, n1g_ref, w_ref, qg_ref, kg_ref, g_ref, cw_ref, cb_ref, gng_ref, gnb_ref,
               q_ref, k_ref, v_ref, km_ref, a_ref, abuf, shbuf, ybuf):
    tm = x_ref.shape[1]

    @pl.when(pl.program_id(1) == 0)
    def _():
        abuf[0:HALO, :] = jnp.zeros((HALO, CONV_WIDTH), F32)

    x = x_ref[0]
    ms = jnp.mean(x * x, axis=-1, keepdims=True)
    xn = x * lax.rsqrt(ms + EPS) * n1g_ref[...]
    xn = xn * (1.0 + mod_ref[0, 1:2, :]) + mod_ref[0, 0:1, :]
    xb = xn.astype(BF16)

    def proj(c):
        return _dot(xb, w_ref[:, c * ATTN_WIDTH:(c + 1) * ATTN_WIDTH])

    glu = proj(3) * _sigmoid(proj(4))

    def store_pairs(ref, val):
        for p in range(ATTN_WIDTH // PAIR):
            ref[0, p] = val[:, p * PAIR:(p + 1) * PAIR]

    q = proj(0)
    qn = q * lax.rsqrt(_group_mean(q * q, g_ref) + EPS) * qg_ref[...]
    store_pairs(q_ref, (qn * (HEAD_DIM ** -0.5 * LOG2E)).astype(BF16))
    k = proj(1)
    kn = k * lax.rsqrt(_group_mean(k * k, g_ref) + EPS) * kg_ref[...]
    store_pairs(k_ref, kn.astype(BF16))
    km_ref[0, 0] = jnp.concatenate(
        [jnp.mean(kn[t * MOBA_BLOCK:(t + 1) * MOBA_BLOCK], axis=0, keepdims=True)
         for t in range(tm // MOBA_BLOCK)], axis=0)
    store_pairs(v_ref, proj(2).astype(BF16))

    abuf[HALO:HALO + tm, :] = glu
    nsh = shbuf.shape[1]
    for ph in range(1, 8):
        shbuf[ph - 1] = abuf[ph:ph + nsh, :]
    rc, lc = 64, 256
    base = HALO - (CONV_KERNEL - 1)
    for r in range(tm // rc):
        for l in range(CONV_WIDTH // lc):
            cols = slice(l * lc, (l + 1) * lc)
            acc = jnp.broadcast_to(cb_ref[:, cols], (rc // 8, 8, lc))
            for j in range(CONV_KERNEL):
                ph = (base + j) % 8
                row = r * rc + base + j - ph
                tap = abuf[row:row + rc, cols] if ph == 0 else shbuf[ph - 1, row:row + rc, cols]
                acc = acc + cw_ref[j, :, cols] * tap.reshape(rc // 8, 8, lc)
            ybuf[r * rc:(r + 1) * rc, cols] = acc.reshape(rc, lc)
    abuf[0:HALO, :] = abuf[tm:tm + HALO, :]

    y = ybuf[...]
    yc = y - _group_mean(y, g_ref)
    var = _group_mean(yc * yc, g_ref)
    yn = yc * lax.rsqrt(var + EPS) * gng_ref[...] + gnb_ref[...]
    a_ref[0] = (yn * _sigmoid(yn)).astype(BF16)


def _in_proj(x, mod, n1g, w_in, qg, kg, gmat, cw, cb, gng, gnb):
    bsz, s, d = x.shape
    tm = TM_IN
    nt = s // tm
    nblk = tm // MOBA_BLOCK
    tok = lambda w: pl.BlockSpec((1, tm, w), lambda b, i: (b, i, 0))
    npair = ATTN_WIDTH // PAIR
    pairs = pl.BlockSpec((1, npair, tm, PAIR), lambda b, i: (b, 0, i, 0))
    pair_seq = jax.ShapeDtypeStruct((bsz, npair, s, PAIR), BF16)
    return pl.pallas_call(
        _in_kernel,
        grid=(bsz, nt),
        in_specs=[tok(d),
                  pl.BlockSpec((1, 6, d), lambda b, i: (b, 0, 0)),
                  _resident((1, d)),
                  _resident(w_in.shape),
                  _resident((1, ATTN_WIDTH)), _resident((1, ATTN_WIDTH)),
                  _resident(gmat.shape),
                  _resident(cw.shape), _resident((1, CONV_WIDTH)),
                  _resident((1, CONV_WIDTH)), _resident((1, CONV_WIDTH))],
        out_specs=[pairs, pairs, pairs,
                   pl.BlockSpec((1, 1, nblk, ATTN_WIDTH), lambda b, i: (b, i, 0, 0)),
                   tok(CONV_WIDTH)],
        out_shape=[pair_seq, pair_seq, pair_seq,
                   jax.ShapeDtypeStruct((bsz, nt, nblk, ATTN_WIDTH), F32),
                   jax.ShapeDtypeStruct((bsz, s, CONV_WIDTH), BF16)],
        scratch_shapes=[pltpu.VMEM((tm + HALO, CONV_WIDTH), F32),
                        pltpu.VMEM((7, tm + HALO - 8, CONV_WIDTH), F32),
                        pltpu.VMEM((tm, CONV_WIDTH), F32)],
        compiler_params=pltpu.CompilerParams(dimension_semantics=("arbitrary", "arbitrary"),
                                             vmem_limit_bytes=VMEM_LIMIT),
        name="in_proj_conv",
    )(x, mod, n1g, w_in, qg, kg, gmat, cw, cb, gng, gnb)


def _attn_kernel(rb_ref, q_ref, k_ref, v_ref, km_ref, bias_ref, o_ref, vt_ref, s_ref, selb_ref):
    hp = pl.program_id(0)
    s = q_ref.shape[2]
    nb = s // MOBA_BLOCK
    blk = MOBA_BLOCK

    pad_rows = jnp.where(lax.broadcasted_iota(jnp.int32, (V_ROWS - HEAD_DIM, blk), 0) == 0, 1.0, 0.0).astype(BF16)
    for j in range(nb):
        vt = v_ref[0, 0, j * blk:(j + 1) * blk, :].astype(F32).T.astype(BF16)
        for h in range(2):
            vt_ref[h, j, 0:HEAD_DIM, :] = vt[h * HEAD_DIM:(h + 1) * HEAD_DIM, :]
            vt_ref[h, j, HEAD_DIM:V_ROWS, :] = pad_rows

    km = km_ref[0].astype(BF16)
    lane = lax.broadcasted_iota(jnp.int32, (blk, PAIR), 1)
    rows = lax.broadcasted_iota(jnp.int32, (nb, blk), 0).astype(F32)
    far_bias = [rb_ref[2 * hp + h, N_BUCKETS - 1] * LOG2E for h in range(2)]

    m_scores = None
    m_values = None
    for t in range(nb + 1):
        do_scores, do_values = t < nb, t >= 1
        sbuf = t % 2
        m_values, m_scores = m_scores, [None, None]

        if do_scores:
            qpair = q_ref[0, 0, t * blk:(t + 1) * blk, :]
            qms = [jnp.where((lane >= HEAD_DIM) == (h == 1), qpair, jnp.zeros_like(qpair)) for h in range(2)]
            gated = t > MOBA_TOPK
            if gated:
                for h in range(2):
                    gate = jnp.where(rows < float(t), _dot_nt(km, qms[h]), -jnp.inf)
                    sel = jnp.zeros(gate.shape, jnp.bool_)
                    for _ in range(MOBA_TOPK):
                        mx = jnp.max(gate, axis=0, keepdims=True)
                        idx = jnp.min(jnp.where(gate == mx, rows, float(nb)), axis=0, keepdims=True)
                        pick = rows == idx
                        sel = sel | pick
                        gate = jnp.where(pick, -jnp.inf, gate)
                    selb_ref[h] = jnp.where(sel, 0.0, NEG)

        def block_start(j):
            return j * blk if isinstance(j, int) else pl.multiple_of(j * blk, blk)

        def scores(h, j, extra):
            st = _dot_nt(k_ref[0, 0, pl.ds(block_start(j), blk), :], qms[h]) + extra
            s_ref[sbuf, h, j] = st
            return jnp.max(st, axis=0, keepdims=True)

        def far_scores(j, ms):
            return [jnp.maximum(ms[h], scores(h, j, selb_ref[h, pl.ds(j, 1), :] + far_bias[h])) for h in range(2)]

        def values(js, accs):
            out = []
            for h in range(2):
                acc = accs[h]
                for j in js:
                    p = jnp.exp2(s_ref[1 - sbuf, h, j] - m_values[h]).astype(BF16)
                    acc = acc + _dot(vt_ref[h, j], p)
                out.append(acc)
            return out

        n_far = max(t + 1 - N_NEAR, 0) if do_scores else 0
        n_values = t if do_values else 0
        n_loop = (min(n_far, n_values) if do_scores and do_values else max(n_far, n_values)) // LOOP_BLOCKS
        accs = [jnp.zeros((V_ROWS, blk), F32) for _ in range(2)]

        if do_scores:
            for delta in range(min(N_NEAR, t + 1)):
                j = t - delta
                for h in range(2):
                    extra = bias_ref[h, delta]
                    if gated and delta > 0:
                        extra = extra + selb_ref[h, j:j + 1, :]
                    mb = scores(h, j, extra)
                    m_scores[h] = mb if m_scores[h] is None else jnp.maximum(m_scores[h], mb)

        if n_loop > 0:
            def body(g, carry):
                ms, ac = carry
                js = [g * LOOP_BLOCKS + u for u in range(LOOP_BLOCKS)]
                if do_scores:
                    for j in js:
                        ms = far_scores(j, ms)
                if do_values:
                    ac = values(js, ac)
                return ms, ac

            init_m = m_scores if do_scores else [jnp.zeros((1, blk), F32)] * 2
            ms, accs = lax.fori_loop(0, n_loop, body, (init_m, accs))
            if do_scores:
                m_scores = ms
        done = n_loop * LOOP_BLOCKS
        for j in range(done, n_far):
            m_scores = far_scores(j, m_scores)
        if n_values > done:
            accs = values(list(range(done, n_values)), accs)

        if do_values:
            out_t = jnp.concatenate([a[0:HEAD_DIM] / a[HEAD_DIM:HEAD_DIM + 1] for a in accs], axis=0)
            o_ref[0, 0, (t - 1) * blk:t * blk, :] = out_t.T.astype(BF16)


def _attention(rel_bias, q, k, v, kmean, bias):
    bsz, npair, s, _ = q.shape
    nb = s // MOBA_BLOCK
    seq = pl.BlockSpec((1, 1, s, PAIR), lambda hp, b: (b, hp, 0, 0))
    return pl.pallas_call(
        _attn_kernel,
        grid=(npair, bsz),
        in_specs=[pl.BlockSpec(memory_space=pltpu.SMEM),
                  seq, seq, seq,
                  pl.BlockSpec((1, nb, PAIR), lambda hp, b: (b, 0, hp)),
                  pl.BlockSpec((2, N_NEAR, MOBA_BLOCK, MOBA_BLOCK), lambda hp, b: (hp, 0, 0, 0))],
        out_specs=seq,
        out_shape=jax.ShapeDtypeStruct((bsz, npair, s, PAIR), BF16),
        scratch_shapes=[pltpu.VMEM((2, nb, V_ROWS, MOBA_BLOCK), BF16),
                        pltpu.VMEM((2, 2, nb, MOBA_BLOCK, MOBA_BLOCK), F32),
                        pltpu.VMEM((2, nb, MOBA_BLOCK), F32)],
        compiler_params=pltpu.CompilerParams(dimension_semantics=("arbitrary", "arbitrary"),
                                             vmem_limit_bytes=VMEM_LIMIT),
        name="moba_attention",
    )(rel_bias, q, k, v, kmean, bias)


def _ffn_kernel(x_ref, ya_ref, a_ref, mod_ref, wo_ref, n2g_ref, wu_ref, fw_ref, fb_ref, wd_ref,
                o_ref, carry_ref, hbuf, x1_ref, xb_ref, act_ref):
    tm = x_ref.shape[1]

    @pl.when(pl.program_id(1) == 0)
    def _():
        carry_ref[...] = jnp.zeros(carry_ref.shape, F32)

    g1 = mod_ref[0, 2:3, :]
    sh2 = mod_ref[0, 3:4, :]
    sc2 = mod_ref[0, 4:5, :]
    g2 = mod_ref[0, 5:6, :]

    halves = [slice(r * (tm // 2), (r + 1) * (tm // 2)) for r in range(2)]
    ycat = jnp.concatenate([ya_ref[0, p] for p in range(ya_ref.shape[1])] + [a_ref[0]], axis=-1)
    y_mix = [_dot(ycat[rows], wo_ref[...]) for rows in halves]
    for rows, ym in zip(halves, y_mix):
        x1 = x_ref[0, rows, :] + g1 * ym
        ms = jnp.mean(x1 * x1, axis=-1, keepdims=True)
        xn = x1 * lax.rsqrt(ms + EPS) * n2g_ref[...]
        x1_ref[rows, :] = x1
        xb_ref[rows, :] = (xn * (1.0 + sc2) + sh2).astype(BF16)

    def stage_in(c0, n, slot, row_parts):
        cols = slice(c0, c0 + n)
        prev = carry_ref[:, cols]
        for t in range(1, FFN_CONV):
            hbuf[slot, t, 0:t, 0:n] = prev[FFN_HALO - t:FFN_HALO]
        for rows in row_parts:
            h = _dot(xb_ref[rows, :], wu_ref[:, cols])
            for t in range(FFN_CONV):
                hbuf[slot, t, pl.ds(t + rows.start, rows.stop - rows.start), 0:n] = h
        carry_ref[:, cols] = h[h.shape[0] - FFN_HALO:, :]

    def conv_out(c0, n, slot):
        cols = slice(c0, c0 + n)
        out = fb_ref[:, cols]
        for t in range(FFN_CONV):
            out = out + fw_ref[FFN_CONV - 1 - t:FFN_CONV - t, cols] * hbuf[slot, t, 0:tm, 0:n]
        return out

    nch = len(FFN_CHUNKS)
    starts = [sum(FFN_CHUNKS[:c]) for c in range(nch)]

    def stage(c):
        row_parts = halves if c == 0 else [slice(0, tm)]
        stage_in(starts[c], FFN_CHUNKS[c], 2 * (c % 2), row_parts)
        stage_in(FFN_HIDDEN + starts[c], FFN_CHUNKS[c], 2 * (c % 2) + 1, row_parts)

    def down(c):
        n = FFN_CHUNKS[c]
        return _dot(act_ref[c % 2, :, 0:n], wd_ref[starts[c]:starts[c] + n, :])

    stage(0)
    acc = jnp.zeros((tm, D_MODEL), F32)
    for c in range(nch):
        if c + 1 < nch:
            stage(c + 1)
        if c >= 1:
            acc = acc + down(c - 1)
        u = conv_out(starts[c], FFN_CHUNKS[c], 2 * (c % 2))
        g = conv_out(FFN_HIDDEN + starts[c], FFN_CHUNKS[c], 2 * (c % 2) + 1)
        act_ref[c % 2, :, 0:FFN_CHUNKS[c]] = (g * _sigmoid(g) * u).astype(BF16)
    acc = acc + down(nch - 1)
    o_ref[0] = x1_ref[...] + g2 * acc


def _out_ffn(x, y_attn, a, mod, w_out, n2g, w_up, fw, fb, w_down):
    bsz, s, d = x.shape
    tm = TM_FFN
    tok = lambda w: pl.BlockSpec((1, tm, w), lambda b, i: (b, i, 0))
    return pl.pallas_call(
        _ffn_kernel,
        grid=(bsz, s // tm),
        in_specs=[tok(d),
                  pl.BlockSpec((1, y_attn.shape[1], tm, PAIR), lambda b, i: (b, 0, i, 0)),
                  tok(CONV_WIDTH),
                  pl.BlockSpec((1, 6, d), lambda b, i: (b, 0, 0)),
                  _resident(w_out.shape), _resident((1, d)),
                  _resident(w_up.shape), _resident(fw.shape), _resident(fb.shape),
                  _resident(w_down.shape)],
        out_specs=tok(d),
        out_shape=jax.ShapeDtypeStruct((bsz, s, d), F32),
        scratch_shapes=[pltpu.VMEM((FFN_HALO, 2 * FFN_HIDDEN), F32),
                        pltpu.VMEM((4, FFN_CONV, tm + FFN_HALO, max(FFN_CHUNKS)), F32),
                        pltpu.VMEM((tm, d), F32),
                        pltpu.VMEM((tm, d), BF16),
                        pltpu.VMEM((2, tm, max(FFN_CHUNKS)), BF16)],
        compiler_params=pltpu.CompilerParams(dimension_semantics=("arbitrary", "arbitrary"),
                                             vmem_limit_bytes=VMEM_LIMIT),
        name="out_proj_ffn",
    )(x, y_attn, a, mod, w_out, n2g, w_up, fw, fb, w_down)


def _group_avg_matrix():
    half = 256
    idx = np.arange(half) // (CONV_WIDTH // CONV_GROUPS)
    return jnp.asarray((idx[:, None] == idx[None, :]).astype(np.float32) / HEAD_DIM, dtype=BF16)


def kernel(x, c, rel_bias, ada_w, ada_b, norm1_g, w_in, q_norm_g, k_norm_g, conv_dw_w, conv_dw_b,
           conv_norm_g, conv_norm_b, w_out, norm2_g, w_up, ffn_dw_w, ffn_dw_b, w_down):
    bsz, s, d = x.shape
    depth = ada_w.shape[0]
    bias = _bias_tiles(rel_bias)
    gmat = _group_avg_matrix()
    row = lambda t: t.reshape(1, -1)
    for l in range(depth):
        mod = _modulation(c, ada_w[l], ada_b[l]).reshape(bsz, 6, d)
        q, k, v, kmean, a = _in_proj(
            x, mod, row(norm1_g[l]), w_in[l].astype(BF16),
            row(jnp.tile(q_norm_g[l], N_HEADS)), row(jnp.tile(k_norm_g[l], N_HEADS)), gmat,
            jnp.broadcast_to(conv_dw_w[l][:, None, :], (CONV_KERNEL, 8, CONV_WIDTH)),
            row(conv_dw_b[l]), row(conv_norm_g[l]), row(conv_norm_b[l]))
        kmean = kmean.reshape(bsz, s // MOBA_BLOCK, ATTN_WIDTH)
        y_attn = _attention(rel_bias, q, k, v, kmean, bias)
        x = _out_ffn(x, y_attn, a, mod, w_out[l].astype(BF16), row(norm2_g[l]), w_up[l].astype(BF16),
                     ffn_dw_w[l], row(ffn_dw_b[l]), w_down[l].astype(BF16))
    return x
```

```python
import functools
import math

import numpy as np
import jax
import jax.numpy as jnp
from jax import lax
from jax.experimental import pallas as pl
from jax.experimental.pallas import tpu as pltpu

F32 = jnp.float32
BF16 = jnp.bfloat16

D_MODEL = 1024
HEAD_DIM = 64
ATTN_WIDTH = 512
N_HEADS = ATTN_WIDTH // HEAD_DIM
CONV_WIDTH = 512
CONV_GROUPS = 8
CONV_KERNEL = 31
MOBA_BLOCK = 256
MOBA_TOPK = 3
N_BUCKETS = 32
MAX_DISTANCE = 1024
FFN_HIDDEN = 2816
FFN_CONV = 3
EPS = 1e-6
NEG = -1e30
LOG2E = math.log2(math.e)

N_NEAR = 5
PAIR = 2 * HEAD_DIM
V_ROWS = HEAD_DIM + 16
LOOP_BLOCKS = 4
HALO = 32
FFN_HALO = 8
FFN_CHUNKS = (512, 512, 512, 512, 512, 256)
assert sum(FFN_CHUNKS) == FFN_HIDDEN
TM_IN = 512
TM_FFN = 512
VMEM_LIMIT = 56 * 1024 * 1024


def _sigmoid(x):
    return 1.0 / (1.0 + jnp.exp(-x))


def _dot(a, b):
    return jnp.dot(a, b, preferred_element_type=F32)


def _dot_nt(a, b):
    return lax.dot_general(a, b, (((1,), (1,)), ((), ())), preferred_element_type=F32)


def _resident(shape):
    zeros = (0,) * len(shape)
    return pl.BlockSpec(shape, lambda *_: zeros, pipeline_mode=pl.Buffered(1))


def _mod_kernel(c_ref, w_ref, b_ref, o_ref):
    c = c_ref[...]
    sc = c * _sigmoid(c)
    o_ref[...] = _dot(sc.astype(BF16), w_ref[...].astype(BF16)) + b_ref[...]


def _modulation(c, ada_w, ada_b):
    bsz, d = c.shape
    n = ada_w.shape[1]
    tn = 1536
    return pl.pallas_call(
        _mod_kernel,
        grid=(n // tn,),
        in_specs=[pl.BlockSpec((bsz, d), lambda j: (0, 0)),
                  pl.BlockSpec((d, tn), lambda j: (0, j)),
                  pl.BlockSpec((1, tn), lambda j: (0, j))],
        out_specs=pl.BlockSpec((bsz, tn), lambda j: (0, j)),
        out_shape=jax.ShapeDtypeStruct((bsz, n), F32),
        compiler_params=pltpu.CompilerParams(dimension_semantics=("arbitrary",),
                                             vmem_limit_bytes=VMEM_LIMIT),
        name="adaln_mod",
    )(c, ada_w, ada_b.reshape(1, n))


def _bucket_range(delta):
    dist = np.arange(delta * MOBA_BLOCK - (MOBA_BLOCK - 1), delta * MOBA_BLOCK + MOBA_BLOCK)
    n = np.maximum(dist, 0)
    max_exact = N_BUCKETS // 2
    large = max_exact + (np.log(np.maximum(n, 1) / max_exact) / math.log(MAX_DISTANCE / max_exact)
                         * (N_BUCKETS - max_exact)).astype(np.int64)
    bucket = np.where(n < max_exact, n, np.minimum(large, N_BUCKETS - 1))
    return max(int(bucket.min()) - 1, 0), min(int(bucket.max()) + 1, N_BUCKETS - 1)


def _bias_kernel(rb_ref, o_ref):
    h = pl.program_id(0)
    blk = MOBA_BLOCK
    kk = lax.broadcasted_iota(jnp.int32, (blk, blk), 0)
    qq = lax.broadcasted_iota(jnp.int32, (blk, blk), 1)
    col = lax.broadcasted_iota(jnp.int32, (8, 2 * blk), 1)
    max_exact = N_BUCKETS // 2
    for delta in range(N_NEAR):
        dist = delta * blk + col - (blk - 1)
        n = jnp.maximum(dist, 0)
        nf = jnp.maximum(n, 1).astype(F32)
        large = max_exact + (jnp.log(nf / max_exact) / math.log(MAX_DISTANCE / max_exact)
                             * (N_BUCKETS - max_exact)).astype(jnp.int32)
        large = jnp.minimum(large, N_BUCKETS - 1)
        bucket = jnp.where(n < max_exact, n, large)
        lo, hi = _bucket_range(delta)
        line = jnp.full(col.shape, rb_ref[h, lo], F32)
        for b in range(lo + 1, hi + 1):
            line = jnp.where(bucket >= b, rb_ref[h, b], line)
        lines = jnp.broadcast_to(line[0:1, :], (blk, 2 * blk))
        bias = pltpu.roll(lines, blk + 1, 1, stride=1, stride_axis=0)[:, 0:blk]
        if delta == 0:
            bias = jnp.where(qq >= kk, bias, NEG)
        o_ref[0, delta] = bias * LOG2E


def _bias_tiles(rel_bias):
    return pl.pallas_call(
        _bias_kernel,
        grid=(N_HEADS,),
        in_specs=[pl.BlockSpec(memory_space=pltpu.SMEM)],
        out_specs=pl.BlockSpec((1, N_NEAR, MOBA_BLOCK, MOBA_BLOCK), lambda h: (h, 0, 0, 0)),
        out_shape=jax.ShapeDtypeStruct((N_HEADS, N_NEAR, MOBA_BLOCK, MOBA_BLOCK), F32),
        compiler_params=pltpu.CompilerParams(dimension_semantics=("arbitrary",)),
        name="rel_bias_tiles",
    )(rel_bias)


def _group_mean(v, g_ref):
    vb = v.astype(BF16)
    g = g_ref[...]
    half = g.shape[0]
    parts = [_dot(vb[:, s:s + half], g) for s in range(0, v.shape[1], half)]
    return jnp.concatenate(parts, axis=-1)


def _in_kernel(x_ref, mod_ref, n1g_ref, w_ref, qg_ref, kg_ref, g_ref, cw_ref, cb_ref, gng_ref, gnb_ref,
               q_ref, k_ref, v_ref, km_ref, a_ref, abuf, shbuf, ybuf):
    tm = x_ref.shape[1]

    @pl.when(pl.program_id(1) == 0)
    def _():
        abuf[0:HALO, :] = jnp.zeros((HALO, CONV_WIDTH), F32)

    x = x_ref[0]
    ms = jnp.mean(x * x, axis=-1, keepdims=True)
    xn = x * lax.rsqrt(ms + EPS) * n1g_ref[...]
    xn = xn * (1.0 + mod_ref[0, 1:2, :]) + mod_ref[0, 0:1, :]
    xb = xn.astype(BF16)

    def proj(c):
        return _dot(xb, w_ref[:, c * ATTN_WIDTH:(c + 1) * ATTN_WIDTH])

    glu = proj(3) * _sigmoid(proj(4))

    def store_pairs(ref, val):
        for p in range(ATTN_WIDTH // PAIR):
            ref[0, p] = val[:, p * PAIR:(p + 1) * PAIR]

    q = proj(0)
    qn = q * lax.rsqrt(_group_mean(q * q, g_ref) + EPS) * qg_ref[...]
    store_pairs(q_ref, (qn * (HEAD_DIM ** -0.5 * LOG2E)).astype(BF16))
    k = proj(1)
    kn = k * lax.rsqrt(_group_mean(k * k, g_ref) + EPS) * kg_ref[...]
    store_pairs(k_ref, kn.astype(BF16))
    km_ref[0, 0] = jnp.concatenate(
        [jnp.mean(kn[t * MOBA_BLOCK:(t + 1) * MOBA_BLOCK], axis=0, keepdims=True)
         for t in range(tm // MOBA_BLOCK)], axis=0)
    store_pairs(v_ref, proj(2).astype(BF16))

    abuf[HALO:HALO + tm, :] = glu
    nsh = shbuf.shape[1]
    for ph in range(1, 8):
        shbuf[ph - 1] = abuf[ph:ph + nsh, :]
    rc, lc = 64, 256
    base = HALO - (CONV_KERNEL - 1)
    for r in range(tm // rc):
        for l in range(CONV_WIDTH // lc):
            cols = slice(l * lc, (l + 1) * lc)
            acc = jnp.broadcast_to(cb_ref[:, cols], (rc // 8, 8, lc))
            for j in range(CONV_KERNEL):
                ph = (base + j) % 8
                row = r * rc + base + j - ph
                tap = abuf[row:row + rc, cols] if ph == 0 else shbuf[ph - 1, row:row + rc, cols]
                acc = acc + cw_ref[j, :, cols] * tap.reshape(rc // 8, 8, lc)
            ybuf[r * rc:(r + 1) * rc, cols] = acc.reshape(rc, lc)
    abuf[0:HALO, :] = abuf[tm:tm + HALO, :]

    y = ybuf[...]
    yc = y - _group_mean(y, g_ref)
    var = _group_mean(yc * yc, g_ref)
    yn = yc * lax.rsqrt(var + EPS) * gng_ref[...] + gnb_ref[...]
    a_ref[0] = (yn * _sigmoid(yn)).astype(BF16)


def _in_proj(x, mod, n1g, w_in, qg, kg, gmat, cw, cb, gng, gnb):
    bsz, s, d = x.shape
    tm = TM_IN
    nt = s // tm
    nblk = tm // MOBA_BLOCK
    tok = lambda w: pl.BlockSpec((1, tm, w), lambda b, i: (b, i, 0))
    npair = ATTN_WIDTH // PAIR
    pairs = pl.BlockSpec((1, npair, tm, PAIR), lambda b, i: (b, 0, i, 0))
    pair_seq = jax.ShapeDtypeStruct((bsz, npair, s, PAIR), BF16)
    return pl.pallas_call(
        _in_kernel,
        grid=(bsz, nt),
        in_specs=[tok(d),
                  pl.BlockSpec((1, 6, d), lambda b, i: (b, 0, 0)),
                  _resident((1, d)),
                  _resident(w_in.shape),
                  _resident((1, ATTN_WIDTH)), _resident((1, ATTN_WIDTH)),
                  _resident(gmat.shape),
                  _resident(cw.shape), _resident((1, CONV_WIDTH)),
                  _resident((1, CONV_WIDTH)), _resident((1, CONV_WIDTH))],
        out_specs=[pairs, pairs, pairs,
                   pl.BlockSpec((1, 1, nblk, ATTN_WIDTH), lambda b, i: (b, i, 0, 0)),
                   tok(CONV_WIDTH)],
        out_shape=[pair_seq, pair_seq, pair_seq,
                   jax.ShapeDtypeStruct((bsz, nt, nblk, ATTN_WIDTH), F32),
                   jax.ShapeDtypeStruct((bsz, s, CONV_WIDTH), BF16)],
        scratch_shapes=[pltpu.VMEM((tm + HALO, CONV_WIDTH), F32),
                        pltpu.VMEM((7, tm + HALO - 8, CONV_WIDTH), F32),
                        pltpu.VMEM((tm, CONV_WIDTH), F32)],
        compiler_params=pltpu.CompilerParams(dimension_semantics=("arbitrary", "arbitrary"),
                                             vmem_limit_bytes=VMEM_LIMIT),
        name="in_proj_conv",
    )(x, mod, n1g, w_in, qg, kg, gmat, cw, cb, gng, gnb)


def _attn_kernel(rb_ref, q_ref, k_ref, v_ref, km_ref, bias_ref, o_ref, vt_ref, s_ref, selb_ref):
    hp = pl.program_id(0)
    s = q_ref.shape[2]
    nb = s // MOBA_BLOCK
    blk = MOBA_BLOCK

    pad_rows = jnp.where(lax.broadcasted_iota(jnp.int32, (V_ROWS - HEAD_DIM, blk), 0) == 0, 1.0, 0.0).astype(BF16)
    for j in range(nb):
        vt = v_ref[0, 0, j * blk:(j + 1) * blk, :].astype(F32).T.astype(BF16)
        for h in range(2):
            vt_ref[h, j, 0:HEAD_DIM, :] = vt[h * HEAD_DIM:(h + 1) * HEAD_DIM, :]
            vt_ref[h, j, HEAD_DIM:V_ROWS, :] = pad_rows

    km = km_ref[0].astype(BF16)
    lane = lax.broadcasted_iota(jnp.int32, (blk, PAIR), 1)
    rows = lax.broadcasted_iota(jnp.int32, (nb, blk), 0).astype(F32)
    far_bias = [rb_ref[2 * hp + h, N_BUCKETS - 1] * LOG2E for h in range(2)]

    m_scores = None
    m_values = None
    for t in range(nb + 1):
        do_scores, do_values = t < nb, t >= 1
        sbuf = t % 2
        m_values, m_scores = m_scores, [None, None]

        if do_scores:
            qpair = q_ref[0, 0, t * blk:(t + 1) * blk, :]
            qms = [jnp.where((lane >= HEAD_DIM) == (h == 1), qpair, jnp.zeros_like(qpair)) for h in range(2)]
            gated = t > MOBA_TOPK
            if gated:
                for h in range(2):
                    gate = jnp.where(rows < float(t), _dot_nt(km, qms[h]), -jnp.inf)
                    sel = jnp.zeros(gate.shape, jnp.bool_)
                    for _ in range(MOBA_TOPK):
                        mx = jnp.max(gate, axis=0, keepdims=True)
                        idx = jnp.min(jnp.where(gate == mx, rows, float(nb)), axis=0, keepdims=True)
                        pick = rows == idx
                        sel = sel | pick
                        gate = jnp.where(pick, -jnp.inf, gate)
                    selb_ref[h] = jnp.where(sel, 0.0, NEG)

        def block_start(j):
            return j * blk if isinstance(j, int) else pl.multiple_of(j * blk, blk)

        def scores(h, j0, extras, m):
            n = len(extras)
            st = _dot_nt(k_ref[0, 0, pl.ds(block_start(j0), n * blk), :], qms[h])
            for u, extra in enumerate(extras):
                sb = st[u * blk:(u + 1) * blk] + extra
                s_ref[sbuf, h, j0 + u] = sb
                mb = jnp.max(sb, axis=0, keepdims=True)
                m = mb if m is None else jnp.maximum(m, mb)
            return m

        def far_scores(j0, n, ms):
            return [scores(h, j0, [selb_ref[h, pl.ds(j0 + u, 1), :] + far_bias[h] for u in range(n)], ms[h])
                    for h in range(2)]

        def values(js, accs):
            out = []
            for h in range(2):
                acc = accs[h]
                for j in js:
                    p = jnp.exp2(s_ref[1 - sbuf, h, j] - m_values[h]).astype(BF16)
                    acc = acc + _dot(vt_ref[h, j], p)
                out.append(acc)
            return out

        n_far = max(t + 1 - N_NEAR, 0) if do_scores else 0
        n_values = t if do_values else 0
        n_loop = (min(n_far, n_values) if do_scores and do_values else max(n_far, n_values)) // LOOP_BLOCKS
        accs = [jnp.zeros((V_ROWS, blk), F32) for _ in range(2)]

        if do_scores:
            n_near = min(N_NEAR, t + 1)
            for h in range(2):
                extras = []
                for j in range(t - n_near + 1, t + 1):
                    extra = bias_ref[h, t - j]
                    if gated and j < t:
                        extra = extra + selb_ref[h, j:j + 1, :]
                    extras.append(extra)
                m_scores[h] = scores(h, t - n_near + 1, extras, None)

        if n_loop > 0:
            def body(g, carry):
                ms, ac = carry
                if do_scores:
                    ms = far_scores(g * LOOP_BLOCKS, LOOP_BLOCKS, ms)
                if do_values:
                    ac = values([g * LOOP_BLOCKS + u for u in range(LOOP_BLOCKS)], ac)
                return ms, ac

            init_m = m_scores if do_scores else [jnp.zeros((1, blk), F32)] * 2
            ms, accs = lax.fori_loop(0, n_loop, body, (init_m, accs))
            if do_scores:
                m_scores = ms
        done = n_loop * LOOP_BLOCKS
        if n_far > done:
            m_scores = far_scores(done, n_far - done, m_scores)
        if n_values > done:
            accs = values(list(range(done, n_values)), accs)

        if do_values:
            out_t = jnp.concatenate([a[0:HEAD_DIM] / a[HEAD_DIM:HEAD_DIM + 1] for a in accs], axis=0)
            o_ref[0, 0, (t - 1) * blk:t * blk, :] = out_t.T.astype(BF16)


def _attention(rel_bias, q, k, v, kmean, bias):
    bsz, npair, s, _ = q.shape
    nb = s // MOBA_BLOCK
    seq = pl.BlockSpec((1, 1, s, PAIR), lambda hp, b: (b, hp, 0, 0))
    return pl.pallas_call(
        _attn_kernel,
        grid=(npair, bsz),
        in_specs=[pl.BlockSpec(memory_space=pltpu.SMEM),
                  seq, seq, seq,
                  pl.BlockSpec((1, nb, PAIR), lambda hp, b: (b, 0, hp)),
                  pl.BlockSpec((2, N_NEAR, MOBA_BLOCK, MOBA_BLOCK), lambda hp, b: (hp, 0, 0, 0))],
        out_specs=seq,
        out_shape=jax.ShapeDtypeStruct((bsz, npair, s, PAIR), BF16),
        scratch_shapes=[pltpu.VMEM((2, nb, V_ROWS, MOBA_BLOCK), BF16),
                        pltpu.VMEM((2, 2, nb, MOBA_BLOCK, MOBA_BLOCK), F32),
                        pltpu.VMEM((2, nb, MOBA_BLOCK), F32)],
        compiler_params=pltpu.CompilerParams(dimension_semantics=("arbitrary", "arbitrary"),
                                             vmem_limit_bytes=VMEM_LIMIT),
        name="moba_attention",
    )(rel_bias, q, k, v, kmean, bias)


def _ffn_kernel(x_ref, ya_ref, a_ref, mod_ref, wo_ref, n2g_ref, wu_ref, fw_ref, fb_ref, wd_ref,
                o_ref, carry_ref, hbuf, x1_ref, xb_ref, act_ref):
    tm = x_ref.shape[1]

    @pl.when(pl.program_id(1) == 0)
    def _():
        carry_ref[...] = jnp.zeros(carry_ref.shape, F32)

    g1 = mod_ref[0, 2:3, :]
    sh2 = mod_ref[0, 3:4, :]
    sc2 = mod_ref[0, 4:5, :]
    g2 = mod_ref[0, 5:6, :]

    halves = [slice(r * (tm // 2), (r + 1) * (tm // 2)) for r in range(2)]
    ycat = jnp.concatenate([ya_ref[0, p] for p in range(ya_ref.shape[1])] + [a_ref[0]], axis=-1)
    y_mix = [_dot(ycat[rows], wo_ref[...]) for rows in halves]
    for rows, ym in zip(halves, y_mix):
        x1 = x_ref[0, rows, :] + g1 * ym
        ms = jnp.mean(x1 * x1, axis=-1, keepdims=True)
        xn = x1 * lax.rsqrt(ms + EPS) * n2g_ref[...]
        x1_ref[rows, :] = x1
        xb_ref[rows, :] = (xn * (1.0 + sc2) + sh2).astype(BF16)

    def stage_in(c0, n, slot, row_parts):
        cols = slice(c0, c0 + n)
        prev = carry_ref[:, cols]
        for t in range(1, FFN_CONV):
            hbuf[slot, t, 0:t, 0:n] = prev[FFN_HALO - t:FFN_HALO]
        for rows in row_parts:
            h = _dot(xb_ref[rows, :], wu_ref[:, cols])
            for t in range(FFN_CONV):
                hbuf[slot, t, pl.ds(t + rows.start, rows.stop - rows.start), 0:n] = h
        carry_ref[:, cols] = h[h.shape[0] - FFN_HALO:, :]

    def conv_out(c0, n, slot):
        cols = slice(c0, c0 + n)
        out = fb_ref[:, cols]
        for t in range(FFN_CONV):
            out = out + fw_ref[FFN_CONV - 1 - t:FFN_CONV - t, cols] * hbuf[slot, t, 0:tm, 0:n]
        return out

    nch = len(FFN_CHUNKS)
    starts = [sum(FFN_CHUNKS[:c]) for c in range(nch)]

    def stage(c):
        row_parts = halves if c == 0 else [slice(0, tm)]
        stage_in(starts[c], FFN_CHUNKS[c], 2 * (c % 2), row_parts)
        stage_in(FFN_HIDDEN + starts[c], FFN_CHUNKS[c], 2 * (c % 2) + 1, row_parts)

    def down(c):
        n = FFN_CHUNKS[c]
        return _dot(act_ref[c % 2, :, 0:n], wd_ref[starts[c]:starts[c] + n, :])

    stage(0)
    acc = jnp.zeros((tm, D_MODEL), F32)
    for c in range(nch):
        if c + 1 < nch:
            stage(c + 1)
        if c >= 1:
            acc = acc + down(c - 1)
        u = conv_out(starts[c], FFN_CHUNKS[c], 2 * (c % 2))
        g = conv_out(FFN_HIDDEN + starts[c], FFN_CHUNKS[c], 2 * (c % 2) + 1)
        act_ref[c % 2, :, 0:FFN_CHUNKS[c]] = (g * _sigmoid(g) * u).astype(BF16)
    acc = acc + down(nch - 1)
    o_ref[0] = x1_ref[...] + g2 * acc


def _out_ffn(x, y_attn, a, mod, w_out, n2g, w_up, fw, fb, w_down):
    bsz, s, d = x.shape
    tm = TM_FFN
    tok = lambda w: pl.BlockSpec((1, tm, w), lambda b, i: (b, i, 0))
    return pl.pallas_call(
        _ffn_kernel,
        grid=(bsz, s // tm),
        in_specs=[tok(d),
                  pl.BlockSpec((1, y_attn.shape[1], tm, PAIR), lambda b, i: (b, 0, i, 0)),
                  tok(CONV_WIDTH),
                  pl.BlockSpec((1, 6, d), lambda b, i: (b, 0, 0)),
                  _resident(w_out.shape), _resident((1, d)),
                  _resident(w_up.shape), _resident(fw.shape), _resident(fb.shape),
                  _resident(w_down.shape)],
        out_specs=tok(d),
        out_shape=jax.ShapeDtypeStruct((bsz, s, d), F32),
        scratch_shapes=[pltpu.VMEM((FFN_HALO, 2 * FFN_HIDDEN), F32),
                        pltpu.VMEM((4, FFN_CONV, tm + FFN_HALO, max(FFN_CHUNKS)), F32),
                        pltpu.VMEM((tm, d), F32),
                        pltpu.VMEM((tm, d), BF16),
                        pltpu.VMEM((2, tm, max(FFN_CHUNKS)), BF16)],
        compiler_params=pltpu.CompilerParams(dimension_semantics=("arbitrary", "arbitrary"),
                                             vmem_limit_bytes=VMEM_LIMIT),
        name="out_proj_ffn",
    )(x, y_attn, a, mod, w_out, n2g, w_up, fw, fb, w_down)


def _group_avg_matrix():
    half = 256
    idx = np.arange(half) // (CONV_WIDTH // CONV_GROUPS)
    return jnp.asarray((idx[:, None] == idx[None, :]).astype(np.float32) / HEAD_DIM, dtype=BF16)


def kernel(x, c, rel_bias, ada_w, ada_b, norm1_g, w_in, q_norm_g, k_norm_g, conv_dw_w, conv_dw_b,
           conv_norm_g, conv_norm_b, w_out, norm2_g, w_up, ffn_dw_w, ffn_dw_b, w_down):
    bsz, s, d = x.shape
    depth = ada_w.shape[0]
    bias = _bias_tiles(rel_bias)
    gmat = _group_avg_matrix()
    row = lambda t: t.reshape(1, -1)
    for l in range(depth):
        mod = _modulation(c, ada_w[l], ada_b[l]).reshape(bsz, 6, d)
        q, k, v, kmean, a = _in_proj(
            x, mod, row(norm1_g[l]), w_in[l].astype(BF16),
            row(jnp.tile(q_norm_g[l], N_HEADS)), row(jnp.tile(k_norm_g[l], N_HEADS)), gmat,
            jnp.broadcast_to(conv_dw_w[l][:, None, :], (CONV_KERNEL, 8, CONV_WIDTH)),
            row(conv_dw_b[l]), row(conv_norm_g[l]), row(conv_norm_b[l]))
        kmean = kmean.reshape(bsz, s // MOBA_BLOCK, ATTN_WIDTH)
        y_attn = _attention(rel_bias, q, k, v, kmean, bias)
        x = _out_ffn(x, y_attn, a, mod, w_out[l].astype(BF16), row(norm2_g[l]), w_up[l].astype(BF16),
                     ffn_dw_w[l], row(ffn_dw_b[l]), w_down[l].astype(BF16))
    return x
```

```python
import math

import numpy as np
import jax
import jax.numpy as jnp
from jax import lax
from jax.experimental import pallas as pl
from jax.experimental.pallas import tpu as pltpu

F32 = jnp.float32
BF16 = jnp.bfloat16

D_MODEL = 1024
HEAD_DIM = 64
ATTN_WIDTH = 512
N_HEADS = ATTN_WIDTH // HEAD_DIM
CONV_WIDTH = 512
CONV_GROUPS = 8
CONV_KERNEL = 31
MOBA_BLOCK = 256
MOBA_TOPK = 3
N_BUCKETS = 32
MAX_DISTANCE = 1024
FFN_HIDDEN = 2816
FFN_CONV = 3
EPS = 1e-6
NEG = -1e30
LOG2E = math.log2(math.e)

N_NEAR = 5
PAIR = 2 * HEAD_DIM
V_ROWS = HEAD_DIM + 16
LOOP_BLOCKS = 4
HALO = 32
FFN_HALO = 8
FFN_CHUNKS = (512, 512, 512, 512, 512, 256)
assert sum(FFN_CHUNKS) == FFN_HIDDEN
TM_IN = 512
TM_FFN = 512
CONV_TILE = (64, 256)
MOD_TILE = 1536
MXU_TILE = 256
VMEM_LIMIT = 56 * 1024 * 1024


def _sigmoid(x):
    return 1.0 / (1.0 + jnp.exp(-x))


def _dot(a, b):
    return jnp.dot(a, b, preferred_element_type=F32)


def _dot_nt(a, b):
    return lax.dot_general(a, b, (((1,), (1,)), ((), ())), preferred_element_type=F32)


def _resident(shape):
    zeros = (0,) * len(shape)
    return pl.BlockSpec(shape, lambda *_: zeros, pipeline_mode=pl.Buffered(1))


def _mod_kernel(c_ref, w_ref, b_ref, o_ref):
    c = c_ref[...]
    sc = c * _sigmoid(c)
    o_ref[...] = _dot(sc.astype(BF16), w_ref[...].astype(BF16)) + b_ref[...]


def _modulation(c, ada_w, ada_b):
    bsz, d = c.shape
    n = ada_w.shape[1]
    tn = MOD_TILE
    return pl.pallas_call(
        _mod_kernel,
        grid=(n // tn,),
        in_specs=[pl.BlockSpec((bsz, d), lambda j: (0, 0)),
                  pl.BlockSpec((d, tn), lambda j: (0, j)),
                  pl.BlockSpec((1, tn), lambda j: (0, j))],
        out_specs=pl.BlockSpec((bsz, tn), lambda j: (0, j)),
        out_shape=jax.ShapeDtypeStruct((bsz, n), F32),
        compiler_params=pltpu.CompilerParams(dimension_semantics=("arbitrary",),
                                             vmem_limit_bytes=VMEM_LIMIT),
        name="adaln_mod",
    )(c, ada_w, ada_b.reshape(1, n))


def _bucket_range(delta):
    dist = np.arange(delta * MOBA_BLOCK - (MOBA_BLOCK - 1), delta * MOBA_BLOCK + MOBA_BLOCK)
    n = np.maximum(dist, 0)
    max_exact = N_BUCKETS // 2
    large = max_exact + (np.log(np.maximum(n, 1) / max_exact) / math.log(MAX_DISTANCE / max_exact)
                         * (N_BUCKETS - max_exact)).astype(np.int64)
    bucket = np.where(n < max_exact, n, np.minimum(large, N_BUCKETS - 1))
    return max(int(bucket.min()) - 1, 0), min(int(bucket.max()) + 1, N_BUCKETS - 1)


def _bias_kernel(rb_ref, o_ref):
    h = pl.program_id(0)
    blk = MOBA_BLOCK
    kk = lax.broadcasted_iota(jnp.int32, (blk, blk), 0)
    qq = lax.broadcasted_iota(jnp.int32, (blk, blk), 1)
    col = lax.broadcasted_iota(jnp.int32, (8, 2 * blk), 1)
    max_exact = N_BUCKETS // 2
    for delta in range(N_NEAR):
        dist = delta * blk + col - (blk - 1)
        n = jnp.maximum(dist, 0)
        nf = jnp.maximum(n, 1).astype(F32)
        large = max_exact + (jnp.log(nf / max_exact) / math.log(MAX_DISTANCE / max_exact)
                             * (N_BUCKETS - max_exact)).astype(jnp.int32)
        large = jnp.minimum(large, N_BUCKETS - 1)
        bucket = jnp.where(n < max_exact, n, large)
        lo, hi = _bucket_range(delta)
        line = jnp.full(col.shape, rb_ref[h, lo], F32)
        for b in range(lo + 1, hi + 1):
            line = jnp.where(bucket >= b, rb_ref[h, b], line)
        lines = jnp.broadcast_to(line[0:1, :], (blk, 2 * blk))
        bias = pltpu.roll(lines, blk + 1, 1, stride=1, stride_axis=0)[:, 0:blk]
        if delta == 0:
            bias = jnp.where(qq >= kk, bias, NEG)
        o_ref[0, delta] = bias * LOG2E


def _bias_tiles(rel_bias):
    return pl.pallas_call(
        _bias_kernel,
        grid=(N_HEADS,),
        in_specs=[pl.BlockSpec(memory_space=pltpu.SMEM)],
        out_specs=pl.BlockSpec((1, N_NEAR, MOBA_BLOCK, MOBA_BLOCK), lambda h: (h, 0, 0, 0)),
        out_shape=jax.ShapeDtypeStruct((N_HEADS, N_NEAR, MOBA_BLOCK, MOBA_BLOCK), F32),
        compiler_params=pltpu.CompilerParams(dimension_semantics=("arbitrary",)),
        name="rel_bias_tiles",
    )(rel_bias)


def _group_mean(v, g_ref):
    vb = v.astype(BF16)
    g = g_ref[...]
    half = g.shape[0]
    parts = [_dot(vb[:, s:s + half], g) for s in range(0, v.shape[1], half)]
    return jnp.concatenate(parts, axis=-1)


def _in_kernel(x_ref, mod_ref, n1g_ref, w_ref, qg_ref, kg_ref, g_ref, cw_ref, cb_ref, gng_ref, gnb_ref,
               q_ref, k_ref, v_ref, km_ref, a_ref, abuf, shbuf, ybuf):
    tm = x_ref.shape[1]

    @pl.when(pl.program_id(1) == 0)
    def _():
        abuf[0:HALO, :] = jnp.zeros((HALO, CONV_WIDTH), F32)

    x = x_ref[0]
    ms = jnp.mean(x * x, axis=-1, keepdims=True)
    gain = n1g_ref[...] * (1.0 + mod_ref[0, 1:2, :])
    xb = (x * lax.rsqrt(ms + EPS) * gain + mod_ref[0, 0:1, :]).astype(BF16)

    def proj(c):
        return _dot(xb, w_ref[:, c * ATTN_WIDTH:(c + 1) * ATTN_WIDTH])

    glu = proj(3) * _sigmoid(proj(4))

    def store_pairs(ref, val):
        for p in range(ATTN_WIDTH // PAIR):
            ref[0, p] = val[:, p * PAIR:(p + 1) * PAIR]

    q = proj(0)
    q_gain = qg_ref[...] * (HEAD_DIM ** -0.5 * LOG2E)
    store_pairs(q_ref, (q * lax.rsqrt(_group_mean(q * q, g_ref) + EPS) * q_gain).astype(BF16))
    k = proj(1)
    kn = k * lax.rsqrt(_group_mean(k * k, g_ref) + EPS) * kg_ref[...]
    store_pairs(k_ref, kn.astype(BF16))
    km_ref[0, 0] = jnp.concatenate(
        [jnp.mean(kn[t * MOBA_BLOCK:(t + 1) * MOBA_BLOCK], axis=0, keepdims=True)
         for t in range(tm // MOBA_BLOCK)], axis=0)
    store_pairs(v_ref, proj(2).astype(BF16))

    abuf[HALO:HALO + tm, :] = glu
    nsh = shbuf.shape[1]
    for ph in range(1, 8):
        shbuf[ph - 1] = abuf[ph:ph + nsh, :]
    rc, lc = CONV_TILE
    base = HALO - (CONV_KERNEL - 1)
    for r in range(tm // rc):
        for l in range(CONV_WIDTH // lc):
            cols = slice(l * lc, (l + 1) * lc)
            acc = jnp.broadcast_to(cb_ref[:, cols], (rc // 8, 8, lc))
            for j in range(CONV_KERNEL):
                ph = (base + j) % 8
                row = r * rc + base + j - ph
                tap = abuf[row:row + rc, cols] if ph == 0 else shbuf[ph - 1, row:row + rc, cols]
                acc = acc + cw_ref[j, :, cols] * tap.reshape(rc // 8, 8, lc)
            ybuf[r * rc:(r + 1) * rc, cols] = acc.reshape(rc, lc)
    abuf[0:HALO, :] = abuf[tm:tm + HALO, :]

    y = ybuf[...]
    yc = y - _group_mean(y, g_ref)
    var = _group_mean(yc * yc, g_ref)
    yn = yc * lax.rsqrt(var + EPS) * gng_ref[...] + gnb_ref[...]
    a_ref[0] = (yn * _sigmoid(yn)).astype(BF16)


def _in_proj(x, mod, n1g, w_in, qg, kg, gmat, cw, cb, gng, gnb):
    bsz, s, d = x.shape
    tm = TM_IN
    nt = s // tm
    nblk = tm // MOBA_BLOCK
    tok = lambda w: pl.BlockSpec((1, tm, w), lambda b, i: (b, i, 0))
    npair = ATTN_WIDTH // PAIR
    pairs = pl.BlockSpec((1, npair, tm, PAIR), lambda b, i: (b, 0, i, 0))
    pair_seq = jax.ShapeDtypeStruct((bsz, npair, s, PAIR), BF16)
    return pl.pallas_call(
        _in_kernel,
        grid=(bsz, nt),
        in_specs=[tok(d),
                  pl.BlockSpec((1, 6, d), lambda b, i: (b, 0, 0)),
                  _resident((1, d)),
                  _resident(w_in.shape),
                  _resident((1, ATTN_WIDTH)), _resident((1, ATTN_WIDTH)),
                  _resident(gmat.shape),
                  _resident(cw.shape), _resident((1, CONV_WIDTH)),
                  _resident((1, CONV_WIDTH)), _resident((1, CONV_WIDTH))],
        out_specs=[pairs, pairs, pairs,
                   pl.BlockSpec((1, 1, nblk, ATTN_WIDTH), lambda b, i: (b, i, 0, 0)),
                   tok(CONV_WIDTH)],
        out_shape=[pair_seq, pair_seq, pair_seq,
                   jax.ShapeDtypeStruct((bsz, nt, nblk, ATTN_WIDTH), F32),
                   jax.ShapeDtypeStruct((bsz, s, CONV_WIDTH), BF16)],
        scratch_shapes=[pltpu.VMEM((tm + HALO, CONV_WIDTH), F32),
                        pltpu.VMEM((7, tm + HALO - 8, CONV_WIDTH), F32),
                        pltpu.VMEM((tm, CONV_WIDTH), F32)],
        compiler_params=pltpu.CompilerParams(dimension_semantics=("arbitrary", "arbitrary"),
                                             vmem_limit_bytes=VMEM_LIMIT),
        name="in_proj_conv",
    )(x, mod, n1g, w_in, qg, kg, gmat, cw, cb, gng, gnb)


def _attn_kernel(rb_ref, q_ref, k_ref, v_ref, km_ref, bias_ref, o_ref, vt_ref, s_ref, selb_ref):
    hp = pl.program_id(0)
    s = q_ref.shape[2]
    nb = s // MOBA_BLOCK
    blk = MOBA_BLOCK

    pad_rows = jnp.where(lax.broadcasted_iota(jnp.int32, (V_ROWS - HEAD_DIM, blk), 0) == 0, 1.0, 0.0).astype(BF16)
    for j in range(nb):
        vt = v_ref[0, 0, j * blk:(j + 1) * blk, :].astype(F32).T.astype(BF16)
        for h in range(2):
            vt_ref[h, j, 0:HEAD_DIM, :] = vt[h * HEAD_DIM:(h + 1) * HEAD_DIM, :]
            vt_ref[h, j, HEAD_DIM:V_ROWS, :] = pad_rows

    km = km_ref[0].astype(BF16)
    lane = lax.broadcasted_iota(jnp.int32, (blk, PAIR), 1)
    rows = lax.broadcasted_iota(jnp.int32, (nb, blk), 0).astype(F32)
    far_bias = [rb_ref[2 * hp + h, N_BUCKETS - 1] * LOG2E for h in range(2)]

    m_scores = None
    m_values = None
    for t in range(nb + 1):
        do_scores, do_values = t < nb, t >= 1
        sbuf = t % 2
        m_values, m_scores = m_scores, [None, None]

        if do_scores:
            qpair = q_ref[0, 0, t * blk:(t + 1) * blk, :]
            qms = [jnp.where((lane >= HEAD_DIM) == (h == 1), qpair, jnp.zeros_like(qpair)) for h in range(2)]
            gated = t > MOBA_TOPK
            if gated:
                for h in range(2):
                    gate = jnp.where(rows < float(t), _dot_nt(km, qms[h]), -jnp.inf)
                    sel = jnp.zeros(gate.shape, jnp.bool_)
                    for _ in range(MOBA_TOPK):
                        mx = jnp.max(gate, axis=0, keepdims=True)
                        idx = jnp.min(jnp.where(gate == mx, rows, float(nb)), axis=0, keepdims=True)
                        pick = rows == idx
                        sel = sel | pick
                        gate = jnp.where(pick, -jnp.inf, gate)
                    selb_ref[h] = jnp.where(sel, 0.0, NEG)

        def block_start(j):
            return j * blk if isinstance(j, int) else pl.multiple_of(j * blk, blk)

        def scores(h, j0, extras, m):
            n = len(extras)
            st = _dot_nt(k_ref[0, 0, pl.ds(block_start(j0), n * blk), :], qms[h])
            for u, extra in enumerate(extras):
                sb = st[u * blk:(u + 1) * blk] + extra
                s_ref[sbuf, h, j0 + u] = sb
                mb = jnp.max(sb, axis=0, keepdims=True)
                m = mb if m is None else jnp.maximum(m, mb)
            return m

        def far_scores(j0, n, ms):
            return [scores(h, j0, [selb_ref[h, pl.ds(j0 + u, 1), :] + far_bias[h] for u in range(n)], ms[h])
                    for h in range(2)]

        def values(js, accs):
            out = []
            for h in range(2):
                acc = accs[h]
                for j in js:
                    p = jnp.exp2(s_ref[1 - sbuf, h, j] - m_values[h]).astype(BF16)
                    acc = acc + _dot(vt_ref[h, j], p)
                out.append(acc)
            return out

        n_far = max(t + 1 - N_NEAR, 0) if do_scores else 0
        n_values = t if do_values else 0
        n_loop = (min(n_far, n_values) if do_scores and do_values else max(n_far, n_values)) // LOOP_BLOCKS
        accs = [jnp.zeros((V_ROWS, blk), F32) for _ in range(2)]

        if do_scores:
            n_near = min(N_NEAR, t + 1)
            for h in range(2):
                extras = []
                for j in range(t - n_near + 1, t + 1):
                    extra = bias_ref[h, t - j]
                    if gated and j < t:
                        extra = extra + selb_ref[h, j:j + 1, :]
                    extras.append(extra)
                m_scores[h] = scores(h, t - n_near + 1, extras, None)

        if n_loop > 0:
            def body(g, carry):
                ms, ac = carry
                if do_scores:
                    ms = far_scores(g * LOOP_BLOCKS, LOOP_BLOCKS, ms)
                if do_values:
                    ac = values([g * LOOP_BLOCKS + u for u in range(LOOP_BLOCKS)], ac)
                return ms, ac

            init_m = m_scores if do_scores else [jnp.zeros((1, blk), F32)] * 2
            ms, accs = lax.fori_loop(0, n_loop, body, (init_m, accs))
            if do_scores:
                m_scores = ms
        done = n_loop * LOOP_BLOCKS
        if n_far > done:
            m_scores = far_scores(done, n_far - done, m_scores)
        if n_values > done:
            accs = values(list(range(done, n_values)), accs)

        if do_values:
            out_t = jnp.concatenate([a[0:HEAD_DIM] / a[HEAD_DIM:HEAD_DIM + 1] for a in accs], axis=0)
            o_ref[0, 0, (t - 1) * blk:t * blk, :] = out_t.T.astype(BF16)


def _attention(rel_bias, q, k, v, kmean, bias):
    bsz, npair, s, _ = q.shape
    nb = s // MOBA_BLOCK
    seq = pl.BlockSpec((1, 1, s, PAIR), lambda hp, b: (b, hp, 0, 0))
    return pl.pallas_call(
        _attn_kernel,
        grid=(npair, bsz),
        in_specs=[pl.BlockSpec(memory_space=pltpu.SMEM),
                  seq, seq, seq,
                  pl.BlockSpec((1, nb, PAIR), lambda hp, b: (b, 0, hp)),
                  pl.BlockSpec((2, N_NEAR, MOBA_BLOCK, MOBA_BLOCK), lambda hp, b: (hp, 0, 0, 0))],
        out_specs=seq,
        out_shape=jax.ShapeDtypeStruct((bsz, npair, s, PAIR), BF16),
        scratch_shapes=[pltpu.VMEM((2, nb, V_ROWS, MOBA_BLOCK), BF16),
                        pltpu.VMEM((2, 2, nb, MOBA_BLOCK, MOBA_BLOCK), F32),
                        pltpu.VMEM((2, nb, MOBA_BLOCK), F32)],
        compiler_params=pltpu.CompilerParams(dimension_semantics=("arbitrary", "arbitrary"),
                                             vmem_limit_bytes=VMEM_LIMIT),
        name="moba_attention",
    )(rel_bias, q, k, v, kmean, bias)


def _ffn_kernel(x_ref, ya_ref, a_ref, mod_ref, wo_ref, n2g_ref, wu_ref, fw_ref, fb_ref, wd_ref,
                o_ref, carry_ref, hbuf, x1_ref, xb_ref, act_ref):
    tm = x_ref.shape[1]

    @pl.when(pl.program_id(1) == 0)
    def _():
        carry_ref[...] = jnp.zeros(carry_ref.shape, F32)

    g1 = mod_ref[0, 2:3, :]
    sh2 = mod_ref[0, 3:4, :]
    sc2 = mod_ref[0, 4:5, :]
    g2 = mod_ref[0, 5:6, :]

    halves = [slice(r * (tm // 2), (r + 1) * (tm // 2)) for r in range(2)]
    ycat = jnp.concatenate([ya_ref[0, p] for p in range(ya_ref.shape[1])] + [a_ref[0]], axis=-1)
    y_mix = [_dot(ycat[rows], wo_ref[...]) for rows in halves]
    for rows, ym in zip(halves, y_mix):
        x1 = x_ref[0, rows, :] + g1 * ym
        ms = jnp.mean(x1 * x1, axis=-1, keepdims=True)
        xn = x1 * lax.rsqrt(ms + EPS) * n2g_ref[...]
        x1_ref[rows, :] = x1
        xb_ref[rows, :] = (xn * (1.0 + sc2) + sh2).astype(BF16)

    def stage_in(c0, n, slot, row_parts):
        cols = slice(c0, c0 + n)
        prev = carry_ref[:, cols]
        for t in range(1, FFN_CONV):
            hbuf[slot, t, 0:t, 0:n] = prev[FFN_HALO - t:FFN_HALO]
        for rows in row_parts:
            h = _dot(xb_ref[rows, :], wu_ref[:, cols])
            for t in range(FFN_CONV):
                hbuf[slot, t, pl.ds(t + rows.start, rows.stop - rows.start), 0:n] = h
        carry_ref[:, cols] = h[h.shape[0] - FFN_HALO:, :]

    def conv_out(c0, n, slot):
        cols = slice(c0, c0 + n)
        out = fb_ref[:, cols]
        for t in range(FFN_CONV):
            out = out + fw_ref[FFN_CONV - 1 - t:FFN_CONV - t, cols] * hbuf[slot, t, 0:tm, 0:n]
        return out

    nch = len(FFN_CHUNKS)
    starts = [sum(FFN_CHUNKS[:c]) for c in range(nch)]

    def stage(c):
        row_parts = halves if c == 0 else [slice(0, tm)]
        stage_in(starts[c], FFN_CHUNKS[c], 2 * (c % 2), row_parts)
        stage_in(FFN_HIDDEN + starts[c], FFN_CHUNKS[c], 2 * (c % 2) + 1, row_parts)

    def down(c):
        n = FFN_CHUNKS[c]
        return _dot(act_ref[c % 2, :, 0:n], wd_ref[starts[c]:starts[c] + n, :])

    stage(0)
    acc = jnp.zeros((tm, D_MODEL), F32)
    for c in range(nch):
        if c + 1 < nch:
            stage(c + 1)
        if c >= 1:
            acc = acc + down(c - 1)
        u = conv_out(starts[c], FFN_CHUNKS[c], 2 * (c % 2))
        g = conv_out(FFN_HIDDEN + starts[c], FFN_CHUNKS[c], 2 * (c % 2) + 1)
        act_ref[c % 2, :, 0:FFN_CHUNKS[c]] = (g * _sigmoid(g) * u).astype(BF16)
    acc = acc + down(nch - 1)
    o_ref[0] = x1_ref[...] + g2 * acc


def _out_ffn(x, y_attn, a, mod, w_out, n2g, w_up, fw, fb, w_down):
    bsz, s, d = x.shape
    tm = TM_FFN
    tok = lambda w: pl.BlockSpec((1, tm, w), lambda b, i: (b, i, 0))
    return pl.pallas_call(
        _ffn_kernel,
        grid=(bsz, s // tm),
        in_specs=[tok(d),
                  pl.BlockSpec((1, y_attn.shape[1], tm, PAIR), lambda b, i: (b, 0, i, 0)),
                  tok(CONV_WIDTH),
                  pl.BlockSpec((1, 6, d), lambda b, i: (b, 0, 0)),
                  _resident(w_out.shape), _resident((1, d)),
                  _resident(w_up.shape), _resident(fw.shape), _resident(fb.shape),
                  _resident(w_down.shape)],
        out_specs=tok(d),
        out_shape=jax.ShapeDtypeStruct((bsz, s, d), F32),
        scratch_shapes=[pltpu.VMEM((FFN_HALO, 2 * FFN_HIDDEN), F32),
                        pltpu.VMEM((4, FFN_CONV, tm + FFN_HALO, max(FFN_CHUNKS)), F32),
                        pltpu.VMEM((tm, d), F32),
                        pltpu.VMEM((tm, d), BF16),
                        pltpu.VMEM((2, tm, max(FFN_CHUNKS)), BF16)],
        compiler_params=pltpu.CompilerParams(dimension_semantics=("arbitrary", "arbitrary"),
                                             vmem_limit_bytes=VMEM_LIMIT),
        name="out_proj_ffn",
    )(x, y_attn, a, mod, w_out, n2g, w_up, fw, fb, w_down)


def _group_avg_matrix():
    group = CONV_WIDTH // CONV_GROUPS
    assert group == HEAD_DIM and MXU_TILE % group == 0
    idx = np.arange(MXU_TILE) // group
    return jnp.asarray((idx[:, None] == idx[None, :]).astype(np.float32) / group, dtype=BF16)


def kernel(x, c, rel_bias, ada_w, ada_b, norm1_g, w_in, q_norm_g, k_norm_g, conv_dw_w, conv_dw_b,
           conv_norm_g, conv_norm_b, w_out, norm2_g, w_up, ffn_dw_w, ffn_dw_b, w_down):
    bsz, s, d = x.shape
    depth = ada_w.shape[0]
    bias = _bias_tiles(rel_bias)
    gmat = _group_avg_matrix()
    row = lambda t: t.reshape(1, -1)
    for l in range(depth):
        mod = _modulation(c, ada_w[l], ada_b[l]).reshape(bsz, 6, d)
        q, k, v, kmean, a = _in_proj(
            x, mod, row(norm1_g[l]), w_in[l].astype(BF16),
            row(jnp.tile(q_norm_g[l], N_HEADS)), row(jnp.tile(k_norm_g[l], N_HEADS)), gmat,
            jnp.broadcast_to(conv_dw_w[l][:, None, :], (CONV_KERNEL, 8, CONV_WIDTH)),
            row(conv_dw_b[l]), row(conv_norm_g[l]), row(conv_norm_b[l]))
        kmean = kmean.reshape(bsz, s // MOBA_BLOCK, ATTN_WIDTH)
        y_attn = _attention(rel_bias, q, k, v, kmean, bias)
        x = _out_ffn(x, y_attn, a, mod, w_out[l].astype(BF16), row(norm2_g[l]), w_up[l].astype(BF16),
                     ffn_dw_w[l], row(ffn_dw_b[l]), w_down[l].astype(BF16))
    return x
```

```python
import math

import numpy as np
import jax
import jax.numpy as jnp
from jax import lax
from jax.experimental import pallas as pl
from jax.experimental.pallas import tpu as pltpu

F32 = jnp.float32
BF16 = jnp.bfloat16

D_MODEL = 1024
HEAD_DIM = 64
ATTN_WIDTH = 512
N_HEADS = ATTN_WIDTH // HEAD_DIM
CONV_WIDTH = 512
CONV_GROUPS = 8
CONV_KERNEL = 31
MOBA_BLOCK = 256
MOBA_TOPK = 3
N_BUCKETS = 32
MAX_DISTANCE = 1024
FFN_HIDDEN = 2816
FFN_CONV = 3
EPS = 1e-6
NEG = -1e30
LOG2E = math.log2(math.e)

N_NEAR = 5
PAIR = 2 * HEAD_DIM
V_ROWS = HEAD_DIM + 16
LOOP_BLOCKS = 4
HALO = 32
FFN_HALO = 8
FFN_CHUNKS = (512, 512, 512, 512, 512, 256)
assert sum(FFN_CHUNKS) == FFN_HIDDEN
TM_IN = 512
TM_FFN = 512
CONV_TILE = (64, 256)
MOD_TILE = 1536
MXU_TILE = 256
VMEM_LIMIT = 56 * 1024 * 1024


def _sigmoid(x):
    return 1.0 / (1.0 + jnp.exp(-x))


def _dot(a, b):
    return jnp.dot(a, b, preferred_element_type=F32)


def _dot_nt(a, b):
    return lax.dot_general(a, b, (((1,), (1,)), ((), ())), preferred_element_type=F32)


def _resident(shape):
    zeros = (0,) * len(shape)
    return pl.BlockSpec(shape, lambda *_: zeros, pipeline_mode=pl.Buffered(1))


def _mod_kernel(c_ref, w_ref, b_ref, o_ref):
    c = c_ref[...]
    sc = c * _sigmoid(c)
    o_ref[...] = _dot(sc.astype(BF16), w_ref[...].astype(BF16)) + b_ref[...]


def _modulation(c, ada_w, ada_b):
    bsz, d = c.shape
    n = ada_w.shape[1]
    tn = MOD_TILE
    return pl.pallas_call(
        _mod_kernel,
        grid=(n // tn,),
        in_specs=[pl.BlockSpec((bsz, d), lambda j: (0, 0)),
                  pl.BlockSpec((d, tn), lambda j: (0, j)),
                  pl.BlockSpec((1, tn), lambda j: (0, j))],
        out_specs=pl.BlockSpec((bsz, tn), lambda j: (0, j)),
        out_shape=jax.ShapeDtypeStruct((bsz, n), F32),
        compiler_params=pltpu.CompilerParams(dimension_semantics=("arbitrary",),
                                             vmem_limit_bytes=VMEM_LIMIT),
        name="adaln_mod",
    )(c, ada_w, ada_b.reshape(1, n))


def _bucket_range(delta):
    dist = np.arange(delta * MOBA_BLOCK - (MOBA_BLOCK - 1), delta * MOBA_BLOCK + MOBA_BLOCK)
    n = np.maximum(dist, 0)
    max_exact = N_BUCKETS // 2
    large = max_exact + (np.log(np.maximum(n, 1) / max_exact) / math.log(MAX_DISTANCE / max_exact)
                         * (N_BUCKETS - max_exact)).astype(np.int64)
    bucket = np.where(n < max_exact, n, np.minimum(large, N_BUCKETS - 1))
    return max(int(bucket.min()) - 1, 0), min(int(bucket.max()) + 1, N_BUCKETS - 1)


def _bias_kernel(rb_ref, o_ref):
    h = pl.program_id(0)
    blk = MOBA_BLOCK
    kk = lax.broadcasted_iota(jnp.int32, (blk, blk), 0)
    qq = lax.broadcasted_iota(jnp.int32, (blk, blk), 1)
    col = lax.broadcasted_iota(jnp.int32, (8, 2 * blk), 1)
    max_exact = N_BUCKETS // 2
    for delta in range(N_NEAR):
        dist = delta * blk + col - (blk - 1)
        n = jnp.maximum(dist, 0)
        nf = jnp.maximum(n, 1).astype(F32)
        large = max_exact + (jnp.log(nf / max_exact) / math.log(MAX_DISTANCE / max_exact)
                             * (N_BUCKETS - max_exact)).astype(jnp.int32)
        large = jnp.minimum(large, N_BUCKETS - 1)
        bucket = jnp.where(n < max_exact, n, large)
        lo, hi = _bucket_range(delta)
        line = jnp.full(col.shape, rb_ref[h, lo], F32)
        for b in range(lo + 1, hi + 1):
            line = jnp.where(bucket >= b, rb_ref[h, b], line)
        lines = jnp.broadcast_to(line[0:1, :], (blk, 2 * blk))
        bias = pltpu.roll(lines, blk + 1, 1, stride=1, stride_axis=0)[:, 0:blk]
        if delta == 0:
            bias = jnp.where(qq >= kk, bias, NEG)
        o_ref[0, delta] = bias * LOG2E


def _bias_tiles(rel_bias):
    return pl.pallas_call(
        _bias_kernel,
        grid=(N_HEADS,),
        in_specs=[pl.BlockSpec(memory_space=pltpu.SMEM)],
        out_specs=pl.BlockSpec((1, N_NEAR, MOBA_BLOCK, MOBA_BLOCK), lambda h: (h, 0, 0, 0)),
        out_shape=jax.ShapeDtypeStruct((N_HEADS, N_NEAR, MOBA_BLOCK, MOBA_BLOCK), F32),
        compiler_params=pltpu.CompilerParams(dimension_semantics=("arbitrary",)),
        name="rel_bias_tiles",
    )(rel_bias)


def _group_mean(v, g_ref):
    vb = v.astype(BF16)
    g = g_ref[...]
    half = g.shape[0]
    parts = [_dot(vb[:, s:s + half], g) for s in range(0, v.shape[1], half)]
    return jnp.concatenate(parts, axis=-1)


def _in_kernel(x_ref, mod_ref, n1g_ref, w_ref, qg_ref, kg_ref, g_ref, cw_ref, cb_ref, gng_ref, gnb_ref,
               q_ref, k_ref, v_ref, km_ref, a_ref, abuf, shbuf, ybuf, wb_ref):
    tm = x_ref.shape[1]

    @pl.when((pl.program_id(0) == 0) & (pl.program_id(1) == 0))
    def _():
        for c in range(w_ref.shape[1] // ATTN_WIDTH):
            cols = slice(c * ATTN_WIDTH, (c + 1) * ATTN_WIDTH)
            wb_ref[:, cols] = w_ref[:, cols].astype(BF16)

    @pl.when(pl.program_id(1) == 0)
    def _():
        abuf[0:HALO, :] = jnp.zeros((HALO, CONV_WIDTH), F32)

    x = x_ref[0]
    ms = jnp.mean(x * x, axis=-1, keepdims=True)
    gain = n1g_ref[...] * (1.0 + mod_ref[0, 1:2, :])
    xb = (x * lax.rsqrt(ms + EPS) * gain + mod_ref[0, 0:1, :]).astype(BF16)

    def proj(c):
        return _dot(xb, wb_ref[:, c * ATTN_WIDTH:(c + 1) * ATTN_WIDTH])

    glu = proj(3) * _sigmoid(proj(4))

    def store_pairs(ref, val):
        for p in range(ATTN_WIDTH // PAIR):
            ref[0, p] = val[:, p * PAIR:(p + 1) * PAIR]

    q = proj(0)
    q_gain = qg_ref[...] * (HEAD_DIM ** -0.5 * LOG2E)
    store_pairs(q_ref, (q * lax.rsqrt(_group_mean(q * q, g_ref) + EPS) * q_gain).astype(BF16))
    k = proj(1)
    kn = k * lax.rsqrt(_group_mean(k * k, g_ref) + EPS) * kg_ref[...]
    store_pairs(k_ref, kn.astype(BF16))
    km_ref[0, 0] = jnp.concatenate(
        [jnp.mean(kn[t * MOBA_BLOCK:(t + 1) * MOBA_BLOCK], axis=0, keepdims=True)
         for t in range(tm // MOBA_BLOCK)], axis=0)
    store_pairs(v_ref, proj(2).astype(BF16))

    abuf[HALO:HALO + tm, :] = glu
    nsh = shbuf.shape[1]
    for ph in range(1, 8):
        shbuf[ph - 1] = abuf[ph:ph + nsh, :]
    rc, lc = CONV_TILE
    base = HALO - (CONV_KERNEL - 1)
    for r in range(tm // rc):
        for l in range(CONV_WIDTH // lc):
            cols = slice(l * lc, (l + 1) * lc)
            acc = jnp.broadcast_to(cb_ref[:, cols], (rc // 8, 8, lc))
            for j in range(CONV_KERNEL):
                ph = (base + j) % 8
                row = r * rc + base + j - ph
                tap = abuf[row:row + rc, cols] if ph == 0 else shbuf[ph - 1, row:row + rc, cols]
                acc = acc + cw_ref[j, :, cols] * tap.reshape(rc // 8, 8, lc)
            ybuf[r * rc:(r + 1) * rc, cols] = acc.reshape(rc, lc)
    abuf[0:HALO, :] = abuf[tm:tm + HALO, :]

    y = ybuf[...]
    yc = y - _group_mean(y, g_ref)
    var = _group_mean(yc * yc, g_ref)
    yn = yc * lax.rsqrt(var + EPS) * gng_ref[...] + gnb_ref[...]
    a_ref[0] = (yn * _sigmoid(yn)).astype(BF16)


def _in_proj(x, mod, n1g, w_in, qg, kg, gmat, cw, cb, gng, gnb):
    bsz, s, d = x.shape
    tm = TM_IN
    nt = s // tm
    nblk = tm // MOBA_BLOCK
    tok = lambda w: pl.BlockSpec((1, tm, w), lambda b, i: (b, i, 0))
    npair = ATTN_WIDTH // PAIR
    pairs = pl.BlockSpec((1, npair, tm, PAIR), lambda b, i: (b, 0, i, 0))
    pair_seq = jax.ShapeDtypeStruct((bsz, npair, s, PAIR), BF16)
    return pl.pallas_call(
        _in_kernel,
        grid=(bsz, nt),
        in_specs=[tok(d),
                  pl.BlockSpec((1, 6, d), lambda b, i: (b, 0, 0)),
                  _resident((1, d)),
                  _resident(w_in.shape),
                  _resident((1, ATTN_WIDTH)), _resident((1, ATTN_WIDTH)),
                  _resident(gmat.shape),
                  _resident(cw.shape), _resident((1, CONV_WIDTH)),
                  _resident((1, CONV_WIDTH)), _resident((1, CONV_WIDTH))],
        out_specs=[pairs, pairs, pairs,
                   pl.BlockSpec((1, 1, nblk, ATTN_WIDTH), lambda b, i: (b, i, 0, 0)),
                   tok(CONV_WIDTH)],
        out_shape=[pair_seq, pair_seq, pair_seq,
                   jax.ShapeDtypeStruct((bsz, nt, nblk, ATTN_WIDTH), F32),
                   jax.ShapeDtypeStruct((bsz, s, CONV_WIDTH), BF16)],
        scratch_shapes=[pltpu.VMEM((tm + HALO, CONV_WIDTH), F32),
                        pltpu.VMEM((7, tm + HALO - 8, CONV_WIDTH), F32),
                        pltpu.VMEM((tm, CONV_WIDTH), F32),
                        pltpu.VMEM(w_in.shape, BF16)],
        compiler_params=pltpu.CompilerParams(dimension_semantics=("arbitrary", "arbitrary"),
                                             vmem_limit_bytes=VMEM_LIMIT),
        name="in_proj_conv",
    )(x, mod, n1g, w_in, qg, kg, gmat, cw, cb, gng, gnb)


def _attn_kernel(rb_ref, q_ref, k_ref, v_ref, km_ref, bias_ref, o_ref, vt_ref, s_ref, selb_ref):
    hp = pl.program_id(0)
    s = q_ref.shape[2]
    nb = s // MOBA_BLOCK
    blk = MOBA_BLOCK

    pad_rows = jnp.where(lax.broadcasted_iota(jnp.int32, (V_ROWS - HEAD_DIM, blk), 0) == 0, 1.0, 0.0).astype(BF16)
    for j in range(nb):
        vt = v_ref[0, 0, j * blk:(j + 1) * blk, :].astype(F32).T.astype(BF16)
        for h in range(2):
            vt_ref[h, j, 0:HEAD_DIM, :] = vt[h * HEAD_DIM:(h + 1) * HEAD_DIM, :]
            vt_ref[h, j, HEAD_DIM:V_ROWS, :] = pad_rows

    km = km_ref[0].astype(BF16)
    lane = lax.broadcasted_iota(jnp.int32, (blk, PAIR), 1)
    rows = lax.broadcasted_iota(jnp.int32, (nb, blk), 0).astype(F32)
    far_bias = [rb_ref[2 * hp + h, N_BUCKETS - 1] * LOG2E for h in range(2)]

    m_scores = None
    m_values = None
    for t in range(nb + 1):
        do_scores, do_values = t < nb, t >= 1
        sbuf = t % 2
        m_values, m_scores = m_scores, [None, None]

        if do_scores:
            qpair = q_ref[0, 0, t * blk:(t + 1) * blk, :]
            qms = [jnp.where((lane >= HEAD_DIM) == (h == 1), qpair, jnp.zeros_like(qpair)) for h in range(2)]
            gated = t > MOBA_TOPK
            if gated:
                for h in range(2):
                    gate = jnp.where(rows < float(t), _dot_nt(km, qms[h]), -jnp.inf)
                    sel = jnp.zeros(gate.shape, jnp.bool_)
                    for _ in range(MOBA_TOPK):
                        mx = jnp.max(gate, axis=0, keepdims=True)
                        idx = jnp.min(jnp.where(gate == mx, rows, float(nb)), axis=0, keepdims=True)
                        pick = rows == idx
                        sel = sel | pick
                        gate = jnp.where(pick, -jnp.inf, gate)
                    selb_ref[h] = jnp.where(sel, 0.0, NEG)

        def block_start(j):
            return j * blk if isinstance(j, int) else pl.multiple_of(j * blk, blk)

        def scores(h, j0, extras, m):
            n = len(extras)
            st = _dot_nt(k_ref[0, 0, pl.ds(block_start(j0), n * blk), :], qms[h])
            for u, extra in enumerate(extras):
                sb = st[u * blk:(u + 1) * blk] + extra
                s_ref[sbuf, h, j0 + u] = sb
                mb = jnp.max(sb, axis=0, keepdims=True)
                m = mb if m is None else jnp.maximum(m, mb)
            return m

        def far_scores(j0, n, ms):
            return [scores(h, j0, [selb_ref[h, pl.ds(j0 + u, 1), :] + far_bias[h] for u in range(n)], ms[h])
                    for h in range(2)]

        def values(js, accs):
            out = []
            for h in range(2):
                acc = accs[h]
                for j in js:
                    p = jnp.exp2(s_ref[1 - sbuf, h, j] - m_values[h]).astype(BF16)
                    acc = acc + _dot(vt_ref[h, j], p)
                out.append(acc)
            return out

        n_far = max(t + 1 - N_NEAR, 0) if do_scores else 0
        n_values = t if do_values else 0
        n_loop = (min(n_far, n_values) if do_scores and do_values else max(n_far, n_values)) // LOOP_BLOCKS
        accs = [jnp.zeros((V_ROWS, blk), F32) for _ in range(2)]

        if do_scores:
            n_near = min(N_NEAR, t + 1)
            for h in range(2):
                extras = []
                for j in range(t - n_near + 1, t + 1):
                    extra = bias_ref[h, t - j]
                    if gated and j < t:
                        extra = extra + selb_ref[h, j:j + 1, :]
                    extras.append(extra)
                m_scores[h] = scores(h, t - n_near + 1, extras, None)

        if n_loop > 0:
            def body(g, carry):
                ms, ac = carry
                if do_scores:
                    ms = far_scores(g * LOOP_BLOCKS, LOOP_BLOCKS, ms)
                if do_values:
                    ac = values([g * LOOP_BLOCKS + u for u in range(LOOP_BLOCKS)], ac)
                return ms, ac

            init_m = m_scores if do_scores else [jnp.zeros((1, blk), F32)] * 2
            ms, accs = lax.fori_loop(0, n_loop, body, (init_m, accs))
            if do_scores:
                m_scores = ms
        done = n_loop * LOOP_BLOCKS
        if n_far > done:
            m_scores = far_scores(done, n_far - done, m_scores)
        if n_values > done:
            accs = values(list(range(done, n_values)), accs)

        if do_values:
            out_t = jnp.concatenate([a[0:HEAD_DIM] / a[HEAD_DIM:HEAD_DIM + 1] for a in accs], axis=0)
            o_ref[0, 0, (t - 1) * blk:t * blk, :] = out_t.T.astype(BF16)


def _attention(rel_bias, q, k, v, kmean, bias):
    bsz, npair, s, _ = q.shape
    nb = s // MOBA_BLOCK
    seq = pl.BlockSpec((1, 1, s, PAIR), lambda hp, b: (b, hp, 0, 0))
    return pl.pallas_call(
        _attn_kernel,
        grid=(npair, bsz),
        in_specs=[pl.BlockSpec(memory_space=pltpu.SMEM),
                  seq, seq, seq,
                  pl.BlockSpec((1, nb, PAIR), lambda hp, b: (b, 0, hp)),
                  pl.BlockSpec((2, N_NEAR, MOBA_BLOCK, MOBA_BLOCK), lambda hp, b: (hp, 0, 0, 0))],
        out_specs=seq,
        out_shape=jax.ShapeDtypeStruct((bsz, npair, s, PAIR), BF16),
        scratch_shapes=[pltpu.VMEM((2, nb, V_ROWS, MOBA_BLOCK), BF16),
                        pltpu.VMEM((2, 2, nb, MOBA_BLOCK, MOBA_BLOCK), F32),
                        pltpu.VMEM((2, nb, MOBA_BLOCK), F32)],
        compiler_params=pltpu.CompilerParams(dimension_semantics=("arbitrary", "arbitrary"),
                                             vmem_limit_bytes=VMEM_LIMIT),
        name="moba_attention",
    )(rel_bias, q, k, v, kmean, bias)


def _ffn_kernel(x_ref, ya_ref, a_ref, mod_ref, wo_ref, n2g_ref, wu_ref, fw_ref, fb_ref, wd_ref,
                o_ref, carry_ref, hbuf, x1_ref, xb_ref, act_ref):
    tm = x_ref.shape[1]

    @pl.when(pl.program_id(1) == 0)
    def _():
        carry_ref[...] = jnp.zeros(carry_ref.shape, F32)

    g1 = mod_ref[0, 2:3, :]
    sh2 = mod_ref[0, 3:4, :]
    sc2 = mod_ref[0, 4:5, :]
    g2 = mod_ref[0, 5:6, :]

    halves = [slice(r * (tm // 2), (r + 1) * (tm // 2)) for r in range(2)]
    ycat = jnp.concatenate([ya_ref[0, p] for p in range(ya_ref.shape[1])] + [a_ref[0]], axis=-1)
    y_mix = [_dot(ycat[rows], wo_ref[...]) for rows in halves]
    for rows, ym in zip(halves, y_mix):
        x1 = x_ref[0, rows, :] + g1 * ym
        ms = jnp.mean(x1 * x1, axis=-1, keepdims=True)
        xn = x1 * lax.rsqrt(ms + EPS) * n2g_ref[...]
        x1_ref[rows, :] = x1
        xb_ref[rows, :] = (xn * (1.0 + sc2) + sh2).astype(BF16)

    def stage_in(c0, n, slot, row_parts):
        cols = slice(c0, c0 + n)
        prev = carry_ref[:, cols]
        for t in range(1, FFN_CONV):
            hbuf[slot, t, 0:t, 0:n] = prev[FFN_HALO - t:FFN_HALO]
        for rows in row_parts:
            h = _dot(xb_ref[rows, :], wu_ref[:, cols])
            for t in range(FFN_CONV):
                hbuf[slot, t, pl.ds(t + rows.start, rows.stop - rows.start), 0:n] = h
        carry_ref[:, cols] = h[h.shape[0] - FFN_HALO:, :]

    def conv_out(c0, n, slot):
        cols = slice(c0, c0 + n)
        out = fb_ref[:, cols]
        for t in range(FFN_CONV):
            out = out + fw_ref[FFN_CONV - 1 - t:FFN_CONV - t, cols] * hbuf[slot, t, 0:tm, 0:n]
        return out

    nch = len(FFN_CHUNKS)
    starts = [sum(FFN_CHUNKS[:c]) for c in range(nch)]

    def stage(c):
        row_parts = halves if c == 0 else [slice(0, tm)]
        stage_in(starts[c], FFN_CHUNKS[c], 2 * (c % 2), row_parts)
        stage_in(FFN_HIDDEN + starts[c], FFN_CHUNKS[c], 2 * (c % 2) + 1, row_parts)

    def down(c):
        n = FFN_CHUNKS[c]
        return _dot(act_ref[c % 2, :, 0:n], wd_ref[starts[c]:starts[c] + n, :])

    stage(0)
    acc = jnp.zeros((tm, D_MODEL), F32)
    for c in range(nch):
        if c + 1 < nch:
            stage(c + 1)
        if c >= 1:
            acc = acc + down(c - 1)
        u = conv_out(starts[c], FFN_CHUNKS[c], 2 * (c % 2))
        g = conv_out(FFN_HIDDEN + starts[c], FFN_CHUNKS[c], 2 * (c % 2) + 1)
        act_ref[c % 2, :, 0:FFN_CHUNKS[c]] = (g * _sigmoid(g) * u).astype(BF16)
    acc = acc + down(nch - 1)
    o_ref[0] = x1_ref[...] + g2 * acc


def _out_ffn(x, y_attn, a, mod, w_out, n2g, w_up, fw, fb, w_down):
    bsz, s, d = x.shape
    tm = TM_FFN
    tok = lambda w: pl.BlockSpec((1, tm, w), lambda b, i: (b, i, 0))
    return pl.pallas_call(
        _ffn_kernel,
        grid=(bsz, s // tm),
        in_specs=[tok(d),
                  pl.BlockSpec((1, y_attn.shape[1], tm, PAIR), lambda b, i: (b, 0, i, 0)),
                  tok(CONV_WIDTH),
                  pl.BlockSpec((1, 6, d), lambda b, i: (b, 0, 0)),
                  _resident(w_out.shape), _resident((1, d)),
                  _resident(w_up.shape), _resident(fw.shape), _resident(fb.shape),
                  _resident(w_down.shape)],
        out_specs=tok(d),
        out_shape=jax.ShapeDtypeStruct((bsz, s, d), F32),
        scratch_shapes=[pltpu.VMEM((FFN_HALO, 2 * FFN_HIDDEN), F32),
                        pltpu.VMEM((4, FFN_CONV, tm + FFN_HALO, max(FFN_CHUNKS)), F32),
                        pltpu.VMEM((tm, d), F32),
                        pltpu.VMEM((tm, d), BF16),
                        pltpu.VMEM((2, tm, max(FFN_CHUNKS)), BF16)],
        compiler_params=pltpu.CompilerParams(dimension_semantics=("arbitrary", "arbitrary"),
                                             vmem_limit_bytes=VMEM_LIMIT),
        name="out_proj_ffn",
    )(x, y_attn, a, mod, w_out, n2g, w_up, fw, fb, w_down)


def _group_avg_matrix():
    group = CONV_WIDTH // CONV_GROUPS
    assert group == HEAD_DIM and MXU_TILE % group == 0
    idx = np.arange(MXU_TILE) // group
    return jnp.asarray((idx[:, None] == idx[None, :]).astype(np.float32) / group, dtype=BF16)


def kernel(x, c, rel_bias, ada_w, ada_b, norm1_g, w_in, q_norm_g, k_norm_g, conv_dw_w, conv_dw_b,
           conv_norm_g, conv_norm_b, w_out, norm2_g, w_up, ffn_dw_w, ffn_dw_b, w_down):
    bsz, s, d = x.shape
    depth = ada_w.shape[0]
    bias = _bias_tiles(rel_bias)
    gmat = _group_avg_matrix()
    row = lambda t: t.reshape(1, -1)
    for l in range(depth):
        mod = _modulation(c, ada_w[l], ada_b[l]).reshape(bsz, 6, d)
        q, k, v, kmean, a = _in_proj(
            x, mod, row(norm1_g[l]), w_in[l],
            row(jnp.tile(q_norm_g[l], N_HEADS)), row(jnp.tile(k_norm_g[l], N_HEADS)), gmat,
            jnp.broadcast_to(conv_dw_w[l][:, None, :], (CONV_KERNEL, 8, CONV_WIDTH)),
            row(conv_dw_b[l]), row(conv_norm_g[l]), row(conv_norm_b[l]))
        kmean = kmean.reshape(bsz, s // MOBA_BLOCK, ATTN_WIDTH)
        y_attn = _attention(rel_bias, q, k, v, kmean, bias)
        x = _out_ffn(x, y_attn, a, mod, w_out[l].astype(BF16), row(norm2_g[l]), w_up[l].astype(BF16),
                     ffn_dw_w[l], row(ffn_dw_b[l]), w_down[l].astype(BF16))
    return x
```

```python
import math

import numpy as np
import jax
import jax.numpy as jnp
from jax import lax
from jax.experimental import pallas as pl
from jax.experimental.pallas import tpu as pltpu

F32 = jnp.float32
BF16 = jnp.bfloat16

D_MODEL = 1024
HEAD_DIM = 64
ATTN_WIDTH = 512
N_HEADS = ATTN_WIDTH // HEAD_DIM
CONV_WIDTH = 512
CONV_GROUPS = 8
CONV_KERNEL = 31
MOBA_BLOCK = 256
MOBA_TOPK = 3
N_BUCKETS = 32
MAX_DISTANCE = 1024
FFN_HIDDEN = 2816
FFN_CONV = 3
EPS = 1e-6
NEG = -1e30
LOG2E = math.log2(math.e)

N_NEAR = 5
PAIR = 2 * HEAD_DIM
V_ROWS = HEAD_DIM + 16
LOOP_BLOCKS = 4
HALO = 32
FFN_HALO = 8
FFN_CHUNKS = (512, 512, 512, 512, 512, 256)
assert sum(FFN_CHUNKS) == FFN_HIDDEN
TM_IN = 512
TM_FFN = 512
CONV_TILE = (64, 256)
MOD_TILE = 1536
MXU_TILE = 256
VMEM_LIMIT = 56 * 1024 * 1024


def _sigmoid(x):
    return 1.0 / (1.0 + jnp.exp(-x))


def _dot(a, b):
    return jnp.dot(a, b, preferred_element_type=F32)


def _dot_nt(a, b):
    return lax.dot_general(a, b, (((1,), (1,)), ((), ())), preferred_element_type=F32)


def _resident(shape):
    zeros = (0,) * len(shape)
    return pl.BlockSpec(shape, lambda *_: zeros, pipeline_mode=pl.Buffered(1))


def _mod_kernel(c_ref, w_ref, b_ref, o_ref):
    c = c_ref[...]
    sc = c * _sigmoid(c)
    o_ref[...] = _dot(sc.astype(BF16), w_ref[...].astype(BF16)) + b_ref[...]


def _modulation(c, ada_w, ada_b):
    bsz, d = c.shape
    n = ada_w.shape[1]
    tn = MOD_TILE
    return pl.pallas_call(
        _mod_kernel,
        grid=(n // tn,),
        in_specs=[pl.BlockSpec((bsz, d), lambda j: (0, 0)),
                  pl.BlockSpec((d, tn), lambda j: (0, j)),
                  pl.BlockSpec((1, tn), lambda j: (0, j))],
        out_specs=pl.BlockSpec((bsz, tn), lambda j: (0, j)),
        out_shape=jax.ShapeDtypeStruct((bsz, n), F32),
        compiler_params=pltpu.CompilerParams(dimension_semantics=("arbitrary",),
                                             vmem_limit_bytes=VMEM_LIMIT),
        name="adaln_mod",
    )(c, ada_w, ada_b.reshape(1, n))


def _bucket_range(delta):
    dist = np.arange(delta * MOBA_BLOCK - (MOBA_BLOCK - 1), delta * MOBA_BLOCK + MOBA_BLOCK)
    n = np.maximum(dist, 0)
    max_exact = N_BUCKETS // 2
    large = max_exact + (np.log(np.maximum(n, 1) / max_exact) / math.log(MAX_DISTANCE / max_exact)
                         * (N_BUCKETS - max_exact)).astype(np.int64)
    bucket = np.where(n < max_exact, n, np.minimum(large, N_BUCKETS - 1))
    return max(int(bucket.min()) - 1, 0), min(int(bucket.max()) + 1, N_BUCKETS - 1)


def _bias_kernel(rb_ref, o_ref):
    h = pl.program_id(0)
    blk = MOBA_BLOCK
    kk = lax.broadcasted_iota(jnp.int32, (blk, blk), 0)
    qq = lax.broadcasted_iota(jnp.int32, (blk, blk), 1)
    col = lax.broadcasted_iota(jnp.int32, (8, 2 * blk), 1)
    max_exact = N_BUCKETS // 2
    for delta in range(N_NEAR):
        dist = delta * blk + col - (blk - 1)
        n = jnp.maximum(dist, 0)
        nf = jnp.maximum(n, 1).astype(F32)
        large = max_exact + (jnp.log(nf / max_exact) / math.log(MAX_DISTANCE / max_exact)
                             * (N_BUCKETS - max_exact)).astype(jnp.int32)
        large = jnp.minimum(large, N_BUCKETS - 1)
        bucket = jnp.where(n < max_exact, n, large)
        lo, hi = _bucket_range(delta)
        line = jnp.full(col.shape, rb_ref[h, lo], F32)
        for b in range(lo + 1, hi + 1):
            line = jnp.where(bucket >= b, rb_ref[h, b], line)
        lines = jnp.broadcast_to(line[0:1, :], (blk, 2 * blk))
        bias = pltpu.roll(lines, blk + 1, 1, stride=1, stride_axis=0)[:, 0:blk]
        if delta == 0:
            bias = jnp.where(qq >= kk, bias, NEG)
        o_ref[0, delta] = bias * LOG2E


def _bias_tiles(rel_bias):
    return pl.pallas_call(
        _bias_kernel,
        grid=(N_HEADS,),
        in_specs=[pl.BlockSpec(memory_space=pltpu.SMEM)],
        out_specs=pl.BlockSpec((1, N_NEAR, MOBA_BLOCK, MOBA_BLOCK), lambda h: (h, 0, 0, 0)),
        out_shape=jax.ShapeDtypeStruct((N_HEADS, N_NEAR, MOBA_BLOCK, MOBA_BLOCK), F32),
        compiler_params=pltpu.CompilerParams(dimension_semantics=("arbitrary",)),
        name="rel_bias_tiles",
    )(rel_bias)


def _group_mean(v, g_ref):
    vb = v.astype(BF16)
    g = g_ref[...]
    half = g.shape[0]
    parts = [_dot(vb[:, s:s + half], g) for s in range(0, v.shape[1], half)]
    return jnp.concatenate(parts, axis=-1)


def _in_kernel(x_ref, mod_ref, n1g_ref, w_ref, qg_ref, kg_ref, g_ref, cw_ref, cb_ref, gng_ref, gnb_ref,
               q_ref, k_ref, v_ref, km_ref, a_ref, abuf, shbuf, ybuf):
    tm = x_ref.shape[1]

    @pl.when(pl.program_id(1) == 0)
    def _():
        abuf[0:HALO, :] = jnp.zeros((HALO, CONV_WIDTH), F32)

    x = x_ref[0]
    ms = jnp.mean(x * x, axis=-1, keepdims=True)
    gain = n1g_ref[...] * (1.0 + mod_ref[0, 1:2, :])
    xb = (x * lax.rsqrt(ms + EPS) * gain + mod_ref[0, 0:1, :]).astype(BF16)

    def proj(c):
        return _dot(xb, w_ref[:, c * ATTN_WIDTH:(c + 1) * ATTN_WIDTH])

    glu = proj(3) * _sigmoid(proj(4))

    def store_pairs(ref, val):
        for p in range(ATTN_WIDTH // PAIR):
            ref[0, p] = val[:, p * PAIR:(p + 1) * PAIR]

    q = proj(0)
    q_gain = qg_ref[...] * (HEAD_DIM ** -0.5 * LOG2E)
    store_pairs(q_ref, (q * lax.rsqrt(_group_mean(q * q, g_ref) + EPS) * q_gain).astype(BF16))
    k = proj(1)
    kn = k * lax.rsqrt(_group_mean(k * k, g_ref) + EPS) * kg_ref[...]
    store_pairs(k_ref, kn.astype(BF16))
    km_ref[0, 0] = jnp.concatenate(
        [jnp.mean(kn[t * MOBA_BLOCK:(t + 1) * MOBA_BLOCK], axis=0, keepdims=True)
         for t in range(tm // MOBA_BLOCK)], axis=0)
    store_pairs(v_ref, proj(2).astype(BF16))

    abuf[HALO:HALO + tm, :] = glu
    nsh = shbuf.shape[1]
    for ph in range(1, 8):
        shbuf[ph - 1] = abuf[ph:ph + nsh, :]
    rc, lc = CONV_TILE
    base = HALO - (CONV_KERNEL - 1)
    for r in range(tm // rc):
        for l in range(CONV_WIDTH // lc):
            cols = slice(l * lc, (l + 1) * lc)
            acc = jnp.broadcast_to(cb_ref[:, cols], (rc // 8, 8, lc))
            for j in range(CONV_KERNEL):
                ph = (base + j) % 8
                row = r * rc + base + j - ph
                tap = abuf[row:row + rc, cols] if ph == 0 else shbuf[ph - 1, row:row + rc, cols]
                acc = acc + cw_ref[j, :, cols] * tap.reshape(rc // 8, 8, lc)
            ybuf[r * rc:(r + 1) * rc, cols] = acc.reshape(rc, lc)
    abuf[0:HALO, :] = abuf[tm:tm + HALO, :]

    y = ybuf[...]
    yc = y - _group_mean(y, g_ref)
    var = _group_mean(yc * yc, g_ref)
    yn = yc * lax.rsqrt(var + EPS) * gng_ref[...] + gnb_ref[...]
    a_ref[0] = (yn * _sigmoid(yn)).astype(BF16)


def _in_proj(x, mod, n1g, w_in, qg, kg, gmat, cw, cb, gng, gnb):
    bsz, s, d = x.shape
    tm = TM_IN
    nt = s // tm
    nblk = tm // MOBA_BLOCK
    tok = lambda w: pl.BlockSpec((1, tm, w), lambda b, i: (b, i, 0))
    npair = ATTN_WIDTH // PAIR
    pairs = pl.BlockSpec((1, npair, tm, PAIR), lambda b, i: (b, 0, i, 0))
    pair_seq = jax.ShapeDtypeStruct((bsz, npair, s, PAIR), BF16)
    return pl.pallas_call(
        _in_kernel,
        grid=(bsz, nt),
        in_specs=[tok(d),
                  pl.BlockSpec((1, 6, d), lambda b, i: (b, 0, 0)),
                  _resident((1, d)),
                  _resident(w_in.shape),
                  _resident((1, ATTN_WIDTH)), _resident((1, ATTN_WIDTH)),
                  _resident(gmat.shape),
                  _resident(cw.shape), _resident((1, CONV_WIDTH)),
                  _resident((1, CONV_WIDTH)), _resident((1, CONV_WIDTH))],
        out_specs=[pairs, pairs, pairs,
                   pl.BlockSpec((1, 1, nblk, ATTN_WIDTH), lambda b, i: (b, i, 0, 0)),
                   tok(CONV_WIDTH)],
        out_shape=[pair_seq, pair_seq, pair_seq,
                   jax.ShapeDtypeStruct((bsz, nt, nblk, ATTN_WIDTH), F32),
                   jax.ShapeDtypeStruct((bsz, s, CONV_WIDTH), BF16)],
        scratch_shapes=[pltpu.VMEM((tm + HALO, CONV_WIDTH), F32),
                        pltpu.VMEM((7, tm + HALO - 8, CONV_WIDTH), F32),
                        pltpu.VMEM((tm, CONV_WIDTH), F32)],
        compiler_params=pltpu.CompilerParams(dimension_semantics=("arbitrary", "arbitrary"),
                                             vmem_limit_bytes=VMEM_LIMIT),
        name="in_proj_conv",
    )(x, mod, n1g, w_in, qg, kg, gmat, cw, cb, gng, gnb)


def _attn_kernel(rb_ref, q_ref, k_ref, v_ref, km_ref, bias_ref, o_ref, vt_ref, s_ref, selb_ref):
    hp = pl.program_id(0)
    s = q_ref.shape[2]
    nb = s // MOBA_BLOCK
    blk = MOBA_BLOCK

    pad_rows = jnp.where(lax.broadcasted_iota(jnp.int32, (V_ROWS - HEAD_DIM, blk), 0) == 0, 1.0, 0.0).astype(BF16)
    for j in range(nb):
        vt = v_ref[0, 0, j * blk:(j + 1) * blk, :].astype(F32).T.astype(BF16)
        for h in range(2):
            vt_ref[h, j, 0:HEAD_DIM, :] = vt[h * HEAD_DIM:(h + 1) * HEAD_DIM, :]
            vt_ref[h, j, HEAD_DIM:V_ROWS, :] = pad_rows

    km = km_ref[0].astype(BF16)
    lane = lax.broadcasted_iota(jnp.int32, (blk, PAIR), 1)
    rows = lax.broadcasted_iota(jnp.int32, (nb, blk), 0).astype(F32)
    far_bias = [rb_ref[2 * hp + h, N_BUCKETS - 1] * LOG2E for h in range(2)]

    order = list(range(0, nb, 2)) + list(range(nb - 1 - nb % 2, 0, -2))
    assert sorted(order) == list(range(nb))
    m_scores = None
    m_values = None
    for g in range(nb + 1):
        do_scores, do_values = g < nb, g >= 1
        t = order[g] if do_scores else None
        tv = order[g - 1] if do_values else None
        sbuf = g % 2
        m_values, m_scores = m_scores, [None, None]

        if do_scores:
            qpair = q_ref[0, 0, t * blk:(t + 1) * blk, :]
            qms = [jnp.where((lane >= HEAD_DIM) == (h == 1), qpair, jnp.zeros_like(qpair)) for h in range(2)]
            gated = t > MOBA_TOPK
            if gated:
                for h in range(2):
                    gate = jnp.where(rows < float(t), _dot_nt(km, qms[h]), -jnp.inf)
                    sel = jnp.zeros(gate.shape, jnp.bool_)
                    for _ in range(MOBA_TOPK):
                        mx = jnp.max(gate, axis=0, keepdims=True)
                        idx = jnp.min(jnp.where(gate == mx, rows, float(nb)), axis=0, keepdims=True)
                        pick = rows == idx
                        sel = sel | pick
                        gate = jnp.where(pick, -jnp.inf, gate)
                    selb_ref[h] = jnp.where(sel, 0.0, NEG)

        def block_start(j):
            return j * blk if isinstance(j, int) else pl.multiple_of(j * blk, blk)

        def scores(h, j0, extras, m):
            n = len(extras)
            st = _dot_nt(k_ref[0, 0, pl.ds(block_start(j0), n * blk), :], qms[h])
            for u, extra in enumerate(extras):
                sb = st[u * blk:(u + 1) * blk] + extra
                s_ref[sbuf, h, j0 + u] = sb
                mb = jnp.max(sb, axis=0, keepdims=True)
                m = mb if m is None else jnp.maximum(m, mb)
            return m

        def far_scores(j0, n, ms):
            return [scores(h, j0, [selb_ref[h, pl.ds(j0 + u, 1), :] + far_bias[h] for u in range(n)], ms[h])
                    for h in range(2)]

        def values(js, accs):
            out = []
            for h in range(2):
                acc = accs[h]
                for j in js:
                    p = jnp.exp2(s_ref[1 - sbuf, h, j] - m_values[h]).astype(BF16)
                    acc = acc + _dot(vt_ref[h, j], p)
                out.append(acc)
            return out

        n_far = max(t + 1 - N_NEAR, 0) if do_scores else 0
        n_values = tv + 1 if do_values else 0
        n_loop = (min(n_far, n_values) if do_scores and do_values else max(n_far, n_values)) // LOOP_BLOCKS
        accs = [jnp.zeros((V_ROWS, blk), F32) for _ in range(2)]

        if do_scores:
            n_near = min(N_NEAR, t + 1)
            for h in range(2):
                extras = []
                for j in range(t - n_near + 1, t + 1):
                    extra = bias_ref[h, t - j]
                    if gated and j < t:
                        extra = extra + selb_ref[h, j:j + 1, :]
                    extras.append(extra)
                m_scores[h] = scores(h, t - n_near + 1, extras, None)

        if n_loop > 0:
            def body(g, carry):
                ms, ac = carry
                if do_scores:
                    ms = far_scores(g * LOOP_BLOCKS, LOOP_BLOCKS, ms)
                if do_values:
                    ac = values([g * LOOP_BLOCKS + u for u in range(LOOP_BLOCKS)], ac)
                return ms, ac

            init_m = m_scores if do_scores else [jnp.zeros((1, blk), F32)] * 2
            ms, accs = lax.fori_loop(0, n_loop, body, (init_m, accs))
            if do_scores:
                m_scores = ms
        done = n_loop * LOOP_BLOCKS
        if n_far > done:
            m_scores = far_scores(done, n_far - done, m_scores)
        if n_values > done:
            accs = values(list(range(done, n_values)), accs)

        if do_values:
            out_t = jnp.concatenate([a[0:HEAD_DIM] / a[HEAD_DIM:HEAD_DIM + 1] for a in accs], axis=0)
            o_ref[0, 0, tv * blk:(tv + 1) * blk, :] = out_t.T.astype(BF16)


def _attention(rel_bias, q, k, v, kmean, bias):
    bsz, npair, s, _ = q.shape
    nb = s // MOBA_BLOCK
    seq = pl.BlockSpec((1, 1, s, PAIR), lambda hp, b: (b, hp, 0, 0))
    return pl.pallas_call(
        _attn_kernel,
        grid=(npair, bsz),
        in_specs=[pl.BlockSpec(memory_space=pltpu.SMEM),
                  seq, seq, seq,
                  pl.BlockSpec((1, nb, PAIR), lambda hp, b: (b, 0, hp)),
                  pl.BlockSpec((2, N_NEAR, MOBA_BLOCK, MOBA_BLOCK), lambda hp, b: (hp, 0, 0, 0))],
        out_specs=seq,
        out_shape=jax.ShapeDtypeStruct((bsz, npair, s, PAIR), BF16),
        scratch_shapes=[pltpu.VMEM((2, nb, V_ROWS, MOBA_BLOCK), BF16),
                        pltpu.VMEM((2, 2, nb, MOBA_BLOCK, MOBA_BLOCK), F32),
                        pltpu.VMEM((2, nb, MOBA_BLOCK), F32)],
        compiler_params=pltpu.CompilerParams(dimension_semantics=("arbitrary", "arbitrary"),
                                             vmem_limit_bytes=VMEM_LIMIT),
        name="moba_attention",
    )(rel_bias, q, k, v, kmean, bias)


def _ffn_kernel(x_ref, ya_ref, a_ref, mod_ref, wo_ref, n2g_ref, wu_ref, fw_ref, fb_ref, wd_ref,
                o_ref, carry_ref, hbuf, x1_ref, xb_ref, act_ref):
    tm = x_ref.shape[1]

    @pl.when(pl.program_id(1) == 0)
    def _():
        carry_ref[...] = jnp.zeros(carry_ref.shape, F32)

    g1 = mod_ref[0, 2:3, :]
    sh2 = mod_ref[0, 3:4, :]
    sc2 = mod_ref[0, 4:5, :]
    g2 = mod_ref[0, 5:6, :]

    halves = [slice(r * (tm // 2), (r + 1) * (tm // 2)) for r in range(2)]
    ycat = jnp.concatenate([ya_ref[0, p] for p in range(ya_ref.shape[1])] + [a_ref[0]], axis=-1)
    y_mix = [_dot(ycat[rows], wo_ref[...]) for rows in halves]
    for rows, ym in zip(halves, y_mix):
        x1 = x_ref[0, rows, :] + g1 * ym
        ms = jnp.mean(x1 * x1, axis=-1, keepdims=True)
        xn = x1 * lax.rsqrt(ms + EPS) * n2g_ref[...]
        x1_ref[rows, :] = x1
        xb_ref[rows, :] = (xn * (1.0 + sc2) + sh2).astype(BF16)

    def stage_in(c0, n, slot, row_parts):
        cols = slice(c0, c0 + n)
        prev = carry_ref[:, cols]
        for t in range(1, FFN_CONV):
            hbuf[slot, t, 0:t, 0:n] = prev[FFN_HALO - t:FFN_HALO]
        for rows in row_parts:
            h = _dot(xb_ref[rows, :], wu_ref[:, cols])
            for t in range(FFN_CONV):
                hbuf[slot, t, pl.ds(t + rows.start, rows.stop - rows.start), 0:n] = h
        carry_ref[:, cols] = h[h.shape[0] - FFN_HALO:, :]

    def conv_out(c0, n, slot):
        cols = slice(c0, c0 + n)
        out = fb_ref[:, cols]
        for t in range(FFN_CONV):
            out = out + fw_ref[FFN_CONV - 1 - t:FFN_CONV - t, cols] * hbuf[slot, t, 0:tm, 0:n]
        return out

    nch = len(FFN_CHUNKS)
    starts = [sum(FFN_CHUNKS[:c]) for c in range(nch)]

    def stage(c):
        row_parts = halves if c == 0 else [slice(0, tm)]
        stage_in(starts[c], FFN_CHUNKS[c], 2 * (c % 2), row_parts)
        stage_in(FFN_HIDDEN + starts[c], FFN_CHUNKS[c], 2 * (c % 2) + 1, row_parts)

    def down(c):
        n = FFN_CHUNKS[c]
        return _dot(act_ref[c % 2, :, 0:n], wd_ref[starts[c]:starts[c] + n, :])

    stage(0)
    acc = jnp.zeros((tm, D_MODEL), F32)
    for c in range(nch):
        if c + 1 < nch:
            stage(c + 1)
        if c >= 1:
            acc = acc + down(c - 1)
        u = conv_out(starts[c], FFN_CHUNKS[c], 2 * (c % 2))
        g = conv_out(FFN_HIDDEN + starts[c], FFN_CHUNKS[c], 2 * (c % 2) + 1)
        act_ref[c % 2, :, 0:FFN_CHUNKS[c]] = (g * _sigmoid(g) * u).astype(BF16)
    acc = acc + down(nch - 1)
    o_ref[0] = x1_ref[...] + g2 * acc


def _out_ffn(x, y_attn, a, mod, w_out, n2g, w_up, fw, fb, w_down):
    bsz, s, d = x.shape
    tm = TM_FFN
    tok = lambda w: pl.BlockSpec((1, tm, w), lambda b, i: (b, i, 0))
    return pl.pallas_call(
        _ffn_kernel,
        grid=(bsz, s // tm),
        in_specs=[tok(d),
                  pl.BlockSpec((1, y_attn.shape[1], tm, PAIR), lambda b, i: (b, 0, i, 0)),
                  tok(CONV_WIDTH),
                  pl.BlockSpec((1, 6, d), lambda b, i: (b, 0, 0)),
                  _resident(w_out.shape), _resident((1, d)),
                  _resident(w_up.shape), _resident(fw.shape), _resident(fb.shape),
                  _resident(w_down.shape)],
        out_specs=tok(d),
        out_shape=jax.ShapeDtypeStruct((bsz, s, d), F32),
        scratch_shapes=[pltpu.VMEM((FFN_HALO, 2 * FFN_HIDDEN), F32),
                        pltpu.VMEM((4, FFN_CONV, tm + FFN_HALO, max(FFN_CHUNKS)), F32),
                        pltpu.VMEM((tm, d), F32),
                        pltpu.VMEM((tm, d), BF16),
                        pltpu.VMEM((2, tm, max(FFN_CHUNKS)), BF16)],
        compiler_params=pltpu.CompilerParams(dimension_semantics=("arbitrary", "arbitrary"),
                                             vmem_limit_bytes=VMEM_LIMIT),
        name="out_proj_ffn",
    )(x, y_attn, a, mod, w_out, n2g, w_up, fw, fb, w_down)


def _group_avg_matrix():
    group = CONV_WIDTH // CONV_GROUPS
    assert group == HEAD_DIM and MXU_TILE % group == 0
    idx = np.arange(MXU_TILE) // group
    return jnp.asarray((idx[:, None] == idx[None, :]).astype(np.float32) / group, dtype=BF16)


def kernel(x, c, rel_bias, ada_w, ada_b, norm1_g, w_in, q_norm_g, k_norm_g, conv_dw_w, conv_dw_b,
           conv_norm_g, conv_norm_b, w_out, norm2_g, w_up, ffn_dw_w, ffn_dw_b, w_down):
    bsz, s, d = x.shape
    depth = ada_w.shape[0]
    bias = _bias_tiles(rel_bias)
    gmat = _group_avg_matrix()
    row = lambda t: t.reshape(1, -1)
    for l in range(depth):
        mod = _modulation(c, ada_w[l], ada_b[l]).reshape(bsz, 6, d)
        q, k, v, kmean, a = _in_proj(
            x, mod, row(norm1_g[l]), w_in[l].astype(BF16),
            row(jnp.tile(q_norm_g[l], N_HEADS)), row(jnp.tile(k_norm_g[l], N_HEADS)), gmat,
            jnp.broadcast_to(conv_dw_w[l][:, None, :], (CONV_KERNEL, 8, CONV_WIDTH)),
            row(conv_dw_b[l]), row(conv_norm_g[l]), row(conv_norm_b[l]))
        kmean = kmean.reshape(bsz, s // MOBA_BLOCK, ATTN_WIDTH)
        y_attn = _attention(rel_bias, q, k, v, kmean, bias)
        x = _out_ffn(x, y_attn, a, mod, w_out[l].astype(BF16), row(norm2_g[l]), w_up[l].astype(BF16),
                     ffn_dw_w[l], row(ffn_dw_b[l]), w_down[l].astype(BF16))
    return x
```

```python
import math

import numpy as np
import jax
import jax.numpy as jnp
from jax import lax
from jax.experimental import pallas as pl
from jax.experimental.pallas import tpu as pltpu

F32 = jnp.float32
BF16 = jnp.bfloat16

D_MODEL = 1024
HEAD_DIM = 64
ATTN_WIDTH = 512
N_HEADS = ATTN_WIDTH // HEAD_DIM
CONV_WIDTH = 512
CONV_GROUPS = 8
CONV_KERNEL = 31
MOBA_BLOCK = 256
MOBA_TOPK = 3
N_BUCKETS = 32
MAX_DISTANCE = 1024
FFN_HIDDEN = 2816
FFN_CONV = 3
EPS = 1e-6
NEG = -1e30
LOG2E = math.log2(math.e)

N_NEAR = 5
PAIR = 2 * HEAD_DIM
V_ROWS = HEAD_DIM + 16
LOOP_BLOCKS = 4
HALO = 32
FFN_HALO = 8
FFN_CHUNKS = (512, 512, 512, 512, 512, 256)
assert sum(FFN_CHUNKS) == FFN_HIDDEN
TM_IN = 512
TM_FFN = 512
CONV_TILE = (64, 256)
MOD_TILE = 1536
MXU_TILE = 256
VMEM_LIMIT = 56 * 1024 * 1024


def _sigmoid(x):
    return 1.0 / (1.0 + jnp.exp(-x))


def _dot(a, b):
    return jnp.dot(a, b, preferred_element_type=F32)


def _dot_nt(a, b):
    return lax.dot_general(a, b, (((1,), (1,)), ((), ())), preferred_element_type=F32)


def _resident(shape):
    zeros = (0,) * len(shape)
    return pl.BlockSpec(shape, lambda *_: zeros, pipeline_mode=pl.Buffered(1))


def _mod_kernel(c_ref, w_ref, b_ref, o_ref):
    c = c_ref[...]
    sc = c * _sigmoid(c)
    o_ref[...] = _dot(sc.astype(BF16), w_ref[...].astype(BF16)) + b_ref[...]


def _modulation(c, ada_w, ada_b):
    bsz, d = c.shape
    n = ada_w.shape[1]
    tn = MOD_TILE
    return pl.pallas_call(
        _mod_kernel,
        grid=(n // tn,),
        in_specs=[pl.BlockSpec((bsz, d), lambda j: (0, 0)),
                  pl.BlockSpec((d, tn), lambda j: (0, j)),
                  pl.BlockSpec((1, tn), lambda j: (0, j))],
        out_specs=pl.BlockSpec((bsz, tn), lambda j: (0, j)),
        out_shape=jax.ShapeDtypeStruct((bsz, n), F32),
        compiler_params=pltpu.CompilerParams(dimension_semantics=("arbitrary",),
                                             vmem_limit_bytes=VMEM_LIMIT),
        name="adaln_mod",
    )(c, ada_w, ada_b.reshape(1, n))


def _bucket_range(delta):
    dist = np.arange(delta * MOBA_BLOCK - (MOBA_BLOCK - 1), delta * MOBA_BLOCK + MOBA_BLOCK)
    n = np.maximum(dist, 0)
    max_exact = N_BUCKETS // 2
    large = max_exact + (np.log(np.maximum(n, 1) / max_exact) / math.log(MAX_DISTANCE / max_exact)
                         * (N_BUCKETS - max_exact)).astype(np.int64)
    bucket = np.where(n < max_exact, n, np.minimum(large, N_BUCKETS - 1))
    return max(int(bucket.min()) - 1, 0), min(int(bucket.max()) + 1, N_BUCKETS - 1)


def _bias_kernel(rb_ref, o_ref):
    h = pl.program_id(0)
    blk = MOBA_BLOCK
    kk = lax.broadcasted_iota(jnp.int32, (blk, blk), 0)
    qq = lax.broadcasted_iota(jnp.int32, (blk, blk), 1)
    col = lax.broadcasted_iota(jnp.int32, (8, 2 * blk), 1)
    max_exact = N_BUCKETS // 2
    for delta in range(N_NEAR):
        dist = delta * blk + col - (blk - 1)
        n = jnp.maximum(dist, 0)
        nf = jnp.maximum(n, 1).astype(F32)
        large = max_exact + (jnp.log(nf / max_exact) / math.log(MAX_DISTANCE / max_exact)
                             * (N_BUCKETS - max_exact)).astype(jnp.int32)
        large = jnp.minimum(large, N_BUCKETS - 1)
        bucket = jnp.where(n < max_exact, n, large)
        lo, hi = _bucket_range(delta)
        line = jnp.full(col.shape, rb_ref[h, lo], F32)
        for b in range(lo + 1, hi + 1):
            line = jnp.where(bucket >= b, rb_ref[h, b], line)
        lines = jnp.broadcast_to(line[0:1, :], (blk, 2 * blk))
        bias = pltpu.roll(lines, blk + 1, 1, stride=1, stride_axis=0)[:, 0:blk]
        if delta == 0:
            bias = jnp.where(qq >= kk, bias, NEG)
        o_ref[0, delta] = bias * LOG2E


def _bias_tiles(rel_bias):
    return pl.pallas_call(
        _bias_kernel,
        grid=(N_HEADS,),
        in_specs=[pl.BlockSpec(memory_space=pltpu.SMEM)],
        out_specs=pl.BlockSpec((1, N_NEAR, MOBA_BLOCK, MOBA_BLOCK), lambda h: (h, 0, 0, 0)),
        out_shape=jax.ShapeDtypeStruct((N_HEADS, N_NEAR, MOBA_BLOCK, MOBA_BLOCK), F32),
        compiler_params=pltpu.CompilerParams(dimension_semantics=("arbitrary",)),
        name="rel_bias_tiles",
    )(rel_bias)


def _group_mean(v, g_ref):
    vb = v.astype(BF16)
    g = g_ref[...]
    half = g.shape[0]
    parts = [_dot(vb[:, s:s + half], g) for s in range(0, v.shape[1], half)]
    return jnp.concatenate(parts, axis=-1)


def _in_kernel(x_ref, mod_ref, n1g_ref, w_ref, qg_ref, kg_ref, g_ref, cw_ref, cb_ref, gng_ref, gnb_ref,
               q_ref, k_ref, v_ref, km_ref, a_ref, abuf, shbuf, ybuf):
    tm = x_ref.shape[1]

    @pl.when(pl.program_id(1) == 0)
    def _():
        abuf[0:HALO, :] = jnp.zeros((HALO, CONV_WIDTH), F32)

    x = x_ref[0]
    ms = jnp.mean(x * x, axis=-1, keepdims=True)
    gain = n1g_ref[...] * (1.0 + mod_ref[0, 1:2, :])
    xb = (x * lax.rsqrt(ms + EPS) * gain + mod_ref[0, 0:1, :]).astype(BF16)

    def proj(c):
        return _dot(xb, w_ref[:, c * ATTN_WIDTH:(c + 1) * ATTN_WIDTH])

    nsh = shbuf.shape[1]
    hm = tm // 2
    split = hm + HALO - 8
    for r, (lo, hi) in enumerate(((0, split), (split, nsh))):
        xh = xb[r * hm:(r + 1) * hm]
        glu = (_dot(xh, w_ref[:, 3 * ATTN_WIDTH:4 * ATTN_WIDTH])
               * _sigmoid(_dot(xh, w_ref[:, 4 * ATTN_WIDTH:5 * ATTN_WIDTH])))
        abuf[HALO + r * hm:HALO + (r + 1) * hm, :] = glu
        for ph in range(1, 8):
            shbuf[ph - 1, lo:hi] = abuf[ph + lo:ph + hi, :]

    def store_pairs(ref, val):
        for p in range(ATTN_WIDTH // PAIR):
            ref[0, p] = val[:, p * PAIR:(p + 1) * PAIR]

    q = proj(0)
    q_gain = qg_ref[...] * (HEAD_DIM ** -0.5 * LOG2E)
    store_pairs(q_ref, (q * lax.rsqrt(_group_mean(q * q, g_ref) + EPS) * q_gain).astype(BF16))
    k = proj(1)
    kn = k * lax.rsqrt(_group_mean(k * k, g_ref) + EPS) * kg_ref[...]
    store_pairs(k_ref, kn.astype(BF16))
    km_ref[0, 0] = jnp.concatenate(
        [jnp.mean(kn[t * MOBA_BLOCK:(t + 1) * MOBA_BLOCK], axis=0, keepdims=True)
         for t in range(tm // MOBA_BLOCK)], axis=0)
    store_pairs(v_ref, proj(2).astype(BF16))

    rc, lc = CONV_TILE
    base = HALO - (CONV_KERNEL - 1)
    for r in range(tm // rc):
        for l in range(CONV_WIDTH // lc):
            cols = slice(l * lc, (l + 1) * lc)
            acc = jnp.broadcast_to(cb_ref[:, cols], (rc // 8, 8, lc))
            for j in range(CONV_KERNEL):
                ph = (base + j) % 8
                row = r * rc + base + j - ph
                tap = abuf[row:row + rc, cols] if ph == 0 else shbuf[ph - 1, row:row + rc, cols]
                acc = acc + cw_ref[j, :, cols] * tap.reshape(rc // 8, 8, lc)
            ybuf[r * rc:(r + 1) * rc, cols] = acc.reshape(rc, lc)
    abuf[0:HALO, :] = abuf[tm:tm + HALO, :]

    y = ybuf[...]
    yc = y - _group_mean(y, g_ref)
    var = _group_mean(yc * yc, g_ref)
    yn = yc * lax.rsqrt(var + EPS) * gng_ref[...] + gnb_ref[...]
    a_ref[0] = (yn * _sigmoid(yn)).astype(BF16)


def _in_proj(x, mod, n1g, w_in, qg, kg, gmat, cw, cb, gng, gnb):
    bsz, s, d = x.shape
    tm = TM_IN
    nt = s // tm
    nblk = tm // MOBA_BLOCK
    tok = lambda w: pl.BlockSpec((1, tm, w), lambda b, i: (b, i, 0))
    npair = ATTN_WIDTH // PAIR
    pairs = pl.BlockSpec((1, npair, tm, PAIR), lambda b, i: (b, 0, i, 0))
    pair_seq = jax.ShapeDtypeStruct((bsz, npair, s, PAIR), BF16)
    return pl.pallas_call(
        _in_kernel,
        grid=(bsz, nt),
        in_specs=[tok(d),
                  pl.BlockSpec((1, 6, d), lambda b, i: (b, 0, 0)),
                  _resident((1, d)),
                  _resident(w_in.shape),
                  _resident((1, ATTN_WIDTH)), _resident((1, ATTN_WIDTH)),
                  _resident(gmat.shape),
                  _resident(cw.shape), _resident((1, CONV_WIDTH)),
                  _resident((1, CONV_WIDTH)), _resident((1, CONV_WIDTH))],
        out_specs=[pairs, pairs, pairs,
                   pl.BlockSpec((1, 1, nblk, ATTN_WIDTH), lambda b, i: (b, i, 0, 0)),
                   tok(CONV_WIDTH)],
        out_shape=[pair_seq, pair_seq, pair_seq,
                   jax.ShapeDtypeStruct((bsz, nt, nblk, ATTN_WIDTH), F32),
                   jax.ShapeDtypeStruct((bsz, s, CONV_WIDTH), BF16)],
        scratch_shapes=[pltpu.VMEM((tm + HALO, CONV_WIDTH), F32),
                        pltpu.VMEM((7, tm + HALO - 8, CONV_WIDTH), F32),
                        pltpu.VMEM((tm, CONV_WIDTH), F32)],
        compiler_params=pltpu.CompilerParams(dimension_semantics=("arbitrary", "arbitrary"),
                                             vmem_limit_bytes=VMEM_LIMIT),
        name="in_proj_conv",
    )(x, mod, n1g, w_in, qg, kg, gmat, cw, cb, gng, gnb)


def _attn_kernel(rb_ref, q_ref, k_ref, v_ref, km_ref, bias_ref, o_ref, vt_ref, s_ref, selb_ref):
    hp = pl.program_id(0)
    s = q_ref.shape[2]
    nb = s // MOBA_BLOCK
    blk = MOBA_BLOCK

    pad_rows = jnp.where(lax.broadcasted_iota(jnp.int32, (V_ROWS - HEAD_DIM, blk), 0) == 0, 1.0, 0.0).astype(BF16)
    for j in range(nb):
        vt = v_ref[0, 0, j * blk:(j + 1) * blk, :].astype(F32).T.astype(BF16)
        for h in range(2):
            vt_ref[h, j, 0:HEAD_DIM, :] = vt[h * HEAD_DIM:(h + 1) * HEAD_DIM, :]
            vt_ref[h, j, HEAD_DIM:V_ROWS, :] = pad_rows

    km = km_ref[0].astype(BF16)
    lane = lax.broadcasted_iota(jnp.int32, (blk, PAIR), 1)
    rows = lax.broadcasted_iota(jnp.int32, (nb, blk), 0).astype(F32)
    far_bias = [rb_ref[2 * hp + h, N_BUCKETS - 1] * LOG2E for h in range(2)]

    order = list(range(0, nb, 2)) + list(range(nb - 1 - nb % 2, 0, -2))
    assert sorted(order) == list(range(nb))
    m_scores = None
    m_values = None
    for g in range(nb + 1):
        do_scores, do_values = g < nb, g >= 1
        t = order[g] if do_scores else None
        tv = order[g - 1] if do_values else None
        sbuf = g % 2
        m_values, m_scores = m_scores, [None, None]

        if do_scores:
            qpair = q_ref[0, 0, t * blk:(t + 1) * blk, :]
            qms = [jnp.where((lane >= HEAD_DIM) == (h == 1), qpair, jnp.zeros_like(qpair)) for h in range(2)]
            gated = t > MOBA_TOPK
            if gated:
                for h in range(2):
                    gate = jnp.where(rows < float(t), _dot_nt(km, qms[h]), -jnp.inf)
                    sel = jnp.zeros(gate.shape, jnp.bool_)
                    for _ in range(MOBA_TOPK):
                        mx = jnp.max(gate, axis=0, keepdims=True)
                        idx = jnp.min(jnp.where(gate == mx, rows, float(nb)), axis=0, keepdims=True)
                        pick = rows == idx
                        sel = sel | pick
                        gate = jnp.where(pick, -jnp.inf, gate)
                    selb_ref[h] = jnp.where(sel, 0.0, NEG)

        def block_start(j):
            return j * blk if isinstance(j, int) else pl.multiple_of(j * blk, blk)

        def scores(h, j0, extras, m):
            n = len(extras)
            st = _dot_nt(k_ref[0, 0, pl.ds(block_start(j0), n * blk), :], qms[h])
            for u, extra in enumerate(extras):
                sb = st[u * blk:(u + 1) * blk] + extra
                s_ref[sbuf, h, j0 + u] = sb
                mb = jnp.max(sb, axis=0, keepdims=True)
                m = mb if m is None else jnp.maximum(m, mb)
            return m

        def far_scores(j0, n, ms):
            return [scores(h, j0, [selb_ref[h, pl.ds(j0 + u, 1), :] + far_bias[h] for u in range(n)], ms[h])
                    for h in range(2)]

        def values(js, accs):
            out = []
            for h in range(2):
                acc = accs[h]
                for j in js:
                    p = jnp.exp2(s_ref[1 - sbuf, h, j] - m_values[h]).astype(BF16)
                    acc = acc + _dot(vt_ref[h, j], p)
                out.append(acc)
            return out

        n_far = max(t + 1 - N_NEAR, 0) if do_scores else 0
        n_values = tv + 1 if do_values else 0
        n_loop = (min(n_far, n_values) if do_scores and do_values else max(n_far, n_values)) // LOOP_BLOCKS
        accs = [jnp.zeros((V_ROWS, blk), F32) for _ in range(2)]

        if do_scores:
            n_near = min(N_NEAR, t + 1)
            for h in range(2):
                extras = []
                for j in range(t - n_near + 1, t + 1):
                    extra = bias_ref[h, t - j]
                    if gated and j < t:
                        extra = extra + selb_ref[h, j:j + 1, :]
                    extras.append(extra)
                m_scores[h] = scores(h, t - n_near + 1, extras, None)

        if n_loop > 0:
            def body(g, carry):
                ms, ac = carry
                if do_scores:
                    ms = far_scores(g * LOOP_BLOCKS, LOOP_BLOCKS, ms)
                if do_values:
                    ac = values([g * LOOP_BLOCKS + u for u in range(LOOP_BLOCKS)], ac)
                return ms, ac

            init_m = m_scores if do_scores else [jnp.zeros((1, blk), F32)] * 2
            ms, accs = lax.fori_loop(0, n_loop, body, (init_m, accs))
            if do_scores:
                m_scores = ms
        done = n_loop * LOOP_BLOCKS
        if n_far > done:
            m_scores = far_scores(done, n_far - done, m_scores)
        if n_values > done:
            accs = values(list(range(done, n_values)), accs)

        if do_values:
            out_t = jnp.concatenate([a[0:HEAD_DIM] / a[HEAD_DIM:HEAD_DIM + 1] for a in accs], axis=0)
            o_ref[0, 0, tv * blk:(tv + 1) * blk, :] = out_t.T.astype(BF16)


def _attention(rel_bias, q, k, v, kmean, bias):
    bsz, npair, s, _ = q.shape
    nb = s // MOBA_BLOCK
    seq = pl.BlockSpec((1, 1, s, PAIR), lambda hp, b: (b, hp, 0, 0))
    return pl.pallas_call(
        _attn_kernel,
        grid=(npair, bsz),
        in_specs=[pl.BlockSpec(memory_space=pltpu.SMEM),
                  seq, seq, seq,
                  pl.BlockSpec((1, nb, PAIR), lambda hp, b: (b, 0, hp)),
                  pl.BlockSpec((2, N_NEAR, MOBA_BLOCK, MOBA_BLOCK), lambda hp, b: (hp, 0, 0, 0))],
        out_specs=seq,
        out_shape=jax.ShapeDtypeStruct((bsz, npair, s, PAIR), BF16),
        scratch_shapes=[pltpu.VMEM((2, nb, V_ROWS, MOBA_BLOCK), BF16),
                        pltpu.VMEM((2, 2, nb, MOBA_BLOCK, MOBA_BLOCK), F32),
                        pltpu.VMEM((2, nb, MOBA_BLOCK), F32)],
        compiler_params=pltpu.CompilerParams(dimension_semantics=("arbitrary", "arbitrary"),
                                             vmem_limit_bytes=VMEM_LIMIT),
        name="moba_attention",
    )(rel_bias, q, k, v, kmean, bias)


def _ffn_kernel(x_ref, ya_ref, a_ref, mod_ref, wo_ref, n2g_ref, wu_ref, fw_ref, fb_ref, wd_ref,
                o_ref, carry_ref, hbuf, x1_ref, xb_ref, act_ref):
    tm = x_ref.shape[1]

    @pl.when(pl.program_id(1) == 0)
    def _():
        carry_ref[...] = jnp.zeros(carry_ref.shape, F32)

    g1 = mod_ref[0, 2:3, :]
    sh2 = mod_ref[0, 3:4, :]
    sc2 = mod_ref[0, 4:5, :]
    g2 = mod_ref[0, 5:6, :]

    halves = [slice(r * (tm // 2), (r + 1) * (tm // 2)) for r in range(2)]
    ycat = jnp.concatenate([ya_ref[0, p] for p in range(ya_ref.shape[1])] + [a_ref[0]], axis=-1)
    y_mix = [_dot(ycat[rows], wo_ref[...]) for rows in halves]
    for rows, ym in zip(halves, y_mix):
        x1 = x_ref[0, rows, :] + g1 * ym
        ms = jnp.mean(x1 * x1, axis=-1, keepdims=True)
        xn = x1 * lax.rsqrt(ms + EPS) * n2g_ref[...]
        x1_ref[rows, :] = x1
        xb_ref[rows, :] = (xn * (1.0 + sc2) + sh2).astype(BF16)

    def stage_in(c0, n, slot, row_parts):
        cols = slice(c0, c0 + n)
        prev = carry_ref[:, cols]
        for t in range(1, FFN_CONV):
            hbuf[slot, t, 0:t, 0:n] = prev[FFN_HALO - t:FFN_HALO]
        for rows in row_parts:
            h = _dot(xb_ref[rows, :], wu_ref[:, cols])
            for t in range(FFN_CONV):
                hbuf[slot, t, pl.ds(t + rows.start, rows.stop - rows.start), 0:n] = h
        carry_ref[:, cols] = h[h.shape[0] - FFN_HALO:, :]

    def conv_out(c0, n, slot):
        cols = slice(c0, c0 + n)
        out = fb_ref[:, cols]
        for t in range(FFN_CONV):
            out = out + fw_ref[FFN_CONV - 1 - t:FFN_CONV - t, cols] * hbuf[slot, t, 0:tm, 0:n]
        return out

    nch = len(FFN_CHUNKS)
    starts = [sum(FFN_CHUNKS[:c]) for c in range(nch)]

    def stage(c):
        row_parts = halves if c == 0 else [slice(0, tm)]
        stage_in(starts[c], FFN_CHUNKS[c], 2 * (c % 2), row_parts)
        stage_in(FFN_HIDDEN + starts[c], FFN_CHUNKS[c], 2 * (c % 2) + 1, row_parts)

    def down(c):
        n = FFN_CHUNKS[c]
        return _dot(act_ref[c % 2, :, 0:n], wd_ref[starts[c]:starts[c] + n, :])

    stage(0)
    acc = jnp.zeros((tm, D_MODEL), F32)
    for c in range(nch):
        if c + 1 < nch:
            stage(c + 1)
        if c >= 1:
            acc = acc + down(c - 1)
        u = conv_out(starts[c], FFN_CHUNKS[c], 2 * (c % 2))
        g = conv_out(FFN_HIDDEN + starts[c], FFN_CHUNKS[c], 2 * (c % 2) + 1)
        act_ref[c % 2, :, 0:FFN_CHUNKS[c]] = (g * _sigmoid(g) * u).astype(BF16)
    acc = acc + down(nch - 1)
    o_ref[0] = x1_ref[...] + g2 * acc


def _out_ffn(x, y_attn, a, mod, w_out, n2g, w_up, fw, fb, w_down):
    bsz, s, d = x.shape
    tm = TM_FFN
    tok = lambda w: pl.BlockSpec((1, tm, w), lambda b, i: (b, i, 0))
    return pl.pallas_call(
        _ffn_kernel,
        grid=(bsz, s // tm),
        in_specs=[tok(d),
                  pl.BlockSpec((1, y_attn.shape[1], tm, PAIR), lambda b, i: (b, 0, i, 0)),
                  tok(CONV_WIDTH),
                  pl.BlockSpec((1, 6, d), lambda b, i: (b, 0, 0)),
                  _resident(w_out.shape), _resident((1, d)),
                  _resident(w_up.shape), _resident(fw.shape), _resident(fb.shape),
                  _resident(w_down.shape)],
        out_specs=tok(d),
        out_shape=jax.ShapeDtypeStruct((bsz, s, d), F32),
        scratch_shapes=[pltpu.VMEM((FFN_HALO, 2 * FFN_HIDDEN), F32),
                        pltpu.VMEM((4, FFN_CONV, tm + FFN_HALO, max(FFN_CHUNKS)), F32),
                        pltpu.VMEM((tm, d), F32),
                        pltpu.VMEM((tm, d), BF16),
                        pltpu.VMEM((2, tm, max(FFN_CHUNKS)), BF16)],
        compiler_params=pltpu.CompilerParams(dimension_semantics=("arbitrary", "arbitrary"),
                                             vmem_limit_bytes=VMEM_LIMIT),
        name="out_proj_ffn",
    )(x, y_attn, a, mod, w_out, n2g, w_up, fw, fb, w_down)


def _group_avg_matrix():
    group = CONV_WIDTH // CONV_GROUPS
    assert group == HEAD_DIM and MXU_TILE % group == 0
    idx = np.arange(MXU_TILE) // group
    return jnp.asarray((idx[:, None] == idx[None, :]).astype(np.float32) / group, dtype=BF16)


def kernel(x, c, rel_bias, ada_w, ada_b, norm1_g, w_in, q_norm_g, k_norm_g, conv_dw_w, conv_dw_b,
           conv_norm_g, conv_norm_b, w_out, norm2_g, w_up, ffn_dw_w, ffn_dw_b, w_down):
    bsz, s, d = x.shape
    depth = ada_w.shape[0]
    bias = _bias_tiles(rel_bias)
    gmat = _group_avg_matrix()
    row = lambda t: t.reshape(1, -1)
    for l in range(depth):
        mod = _modulation(c, ada_w[l], ada_b[l]).reshape(bsz, 6, d)
        q, k, v, kmean, a = _in_proj(
            x, mod, row(norm1_g[l]), w_in[l].astype(BF16),
            row(jnp.tile(q_norm_g[l], N_HEADS)), row(jnp.tile(k_norm_g[l], N_HEADS)), gmat,
            jnp.broadcast_to(conv_dw_w[l][:, None, :], (CONV_KERNEL, 8, CONV_WIDTH)),
            row(conv_dw_b[l]), row(conv_norm_g[l]), row(conv_norm_b[l]))
        kmean = kmean.reshape(bsz, s // MOBA_BLOCK, ATTN_WIDTH)
        y_attn = _attention(rel_bias, q, k, v, kmean, bias)
        x = _out_ffn(x, y_attn, a, mod, w_out[l].astype(BF16), row(norm2_g[l]), w_up[l].astype(BF16),
                     ffn_dw_w[l], row(ffn_dw_b[l]), w_down[l].astype(BF16))
    return x
```

```python
import math

import numpy as np
import jax
import jax.numpy as jnp
from jax import lax
from jax.experimental import pallas as pl
from jax.experimental.pallas import tpu as pltpu

F32 = jnp.float32
BF16 = jnp.bfloat16

D_MODEL = 1024
HEAD_DIM = 64
ATTN_WIDTH = 512
N_HEADS = ATTN_WIDTH // HEAD_DIM
CONV_WIDTH = 512
CONV_GROUPS = 8
CONV_KERNEL = 31
MOBA_BLOCK = 256
MOBA_TOPK = 3
N_BUCKETS = 32
MAX_DISTANCE = 1024
FFN_HIDDEN = 2816
FFN_CONV = 3
EPS = 1e-6
NEG = -1e30
LOG2E = math.log2(math.e)

N_NEAR = 5
PAIR = 2 * HEAD_DIM
V_ROWS = HEAD_DIM + 16
LOOP_BLOCKS = 5
HALO = 32
FFN_HALO = 8
FFN_CHUNKS = (512, 512, 512, 512, 512, 256)
assert sum(FFN_CHUNKS) == FFN_HIDDEN
TM_IN = 512
TM_FFN = 512
CONV_TILE = (64, 256)
MOD_TILE = 1536
MXU_TILE = 256
VMEM_LIMIT = 56 * 1024 * 1024


def _sigmoid(x):
    return 1.0 / (1.0 + jnp.exp(-x))


def _dot(a, b):
    return jnp.dot(a, b, preferred_element_type=F32)


def _dot_nt(a, b):
    return lax.dot_general(a, b, (((1,), (1,)), ((), ())), preferred_element_type=F32)


def _resident(shape):
    zeros = (0,) * len(shape)
    return pl.BlockSpec(shape, lambda *_: zeros, pipeline_mode=pl.Buffered(1))


def _mod_kernel(c_ref, w_ref, b_ref, o_ref):
    c = c_ref[...]
    sc = c * _sigmoid(c)
    o_ref[...] = _dot(sc.astype(BF16), w_ref[...].astype(BF16)) + b_ref[...]


def _modulation(c, ada_w, ada_b):
    bsz, d = c.shape
    n = ada_w.shape[1]
    tn = MOD_TILE
    return pl.pallas_call(
        _mod_kernel,
        grid=(n // tn,),
        in_specs=[pl.BlockSpec((bsz, d), lambda j: (0, 0)),
                  pl.BlockSpec((d, tn), lambda j: (0, j)),
                  pl.BlockSpec((1, tn), lambda j: (0, j))],
        out_specs=pl.BlockSpec((bsz, tn), lambda j: (0, j)),
        out_shape=jax.ShapeDtypeStruct((bsz, n), F32),
        compiler_params=pltpu.CompilerParams(dimension_semantics=("arbitrary",),
                                             vmem_limit_bytes=VMEM_LIMIT),
        name="adaln_mod",
    )(c, ada_w, ada_b.reshape(1, n))


def _bucket_range(delta):
    dist = np.arange(delta * MOBA_BLOCK - (MOBA_BLOCK - 1), delta * MOBA_BLOCK + MOBA_BLOCK)
    n = np.maximum(dist, 0)
    max_exact = N_BUCKETS // 2
    large = max_exact + (np.log(np.maximum(n, 1) / max_exact) / math.log(MAX_DISTANCE / max_exact)
                         * (N_BUCKETS - max_exact)).astype(np.int64)
    bucket = np.where(n < max_exact, n, np.minimum(large, N_BUCKETS - 1))
    return max(int(bucket.min()) - 1, 0), min(int(bucket.max()) + 1, N_BUCKETS - 1)


def _bias_kernel(rb_ref, o_ref):
    h = pl.program_id(0)
    blk = MOBA_BLOCK
    kk = lax.broadcasted_iota(jnp.int32, (blk, blk), 0)
    qq = lax.broadcasted_iota(jnp.int32, (blk, blk), 1)
    col = lax.broadcasted_iota(jnp.int32, (8, 2 * blk), 1)
    max_exact = N_BUCKETS // 2
    for delta in range(N_NEAR):
        dist = delta * blk + col - (blk - 1)
        n = jnp.maximum(dist, 0)
        nf = jnp.maximum(n, 1).astype(F32)
        large = max_exact + (jnp.log(nf / max_exact) / math.log(MAX_DISTANCE / max_exact)
                             * (N_BUCKETS - max_exact)).astype(jnp.int32)
        large = jnp.minimum(large, N_BUCKETS - 1)
        bucket = jnp.where(n < max_exact, n, large)
        lo, hi = _bucket_range(delta)
        line = jnp.full(col.shape, rb_ref[h, lo], F32)
        for b in range(lo + 1, hi + 1):
            line = jnp.where(bucket >= b, rb_ref[h, b], line)
        lines = jnp.broadcast_to(line[0:1, :], (blk, 2 * blk))
        bias = pltpu.roll(lines, blk + 1, 1, stride=1, stride_axis=0)[:, 0:blk]
        if delta == 0:
            bias = jnp.where(qq >= kk, bias, NEG)
        o_ref[0, delta] = bias * LOG2E


def _bias_tiles(rel_bias):
    return pl.pallas_call(
        _bias_kernel,
        grid=(N_HEADS,),
        in_specs=[pl.BlockSpec(memory_space=pltpu.SMEM)],
        out_specs=pl.BlockSpec((1, N_NEAR, MOBA_BLOCK, MOBA_BLOCK), lambda h: (h, 0, 0, 0)),
        out_shape=jax.ShapeDtypeStruct((N_HEADS, N_NEAR, MOBA_BLOCK, MOBA_BLOCK), F32),
        compiler_params=pltpu.CompilerParams(dimension_semantics=("arbitrary",)),
        name="rel_bias_tiles",
    )(rel_bias)


def _group_mean(v, g_ref):
    vb = v.astype(BF16)
    g = g_ref[...]
    half = g.shape[0]
    parts = [_dot(vb[:, s:s + half], g) for s in range(0, v.shape[1], half)]
    return jnp.concatenate(parts, axis=-1)


def _in_kernel(x_ref, mod_ref, n1g_ref, w_ref, qg_ref, kg_ref, g_ref, cw_ref, cb_ref, gng_ref, gnb_ref,
               q_ref, k_ref, v_ref, km_ref, a_ref, abuf, shbuf, ybuf):
    tm = x_ref.shape[1]

    @pl.when(pl.program_id(1) == 0)
    def _():
        abuf[0:HALO, :] = jnp.zeros((HALO, CONV_WIDTH), F32)

    x = x_ref[0]
    ms = jnp.mean(x * x, axis=-1, keepdims=True)
    gain = n1g_ref[...] * (1.0 + mod_ref[0, 1:2, :])
    xb = (x * lax.rsqrt(ms + EPS) * gain + mod_ref[0, 0:1, :]).astype(BF16)

    def proj(c):
        return _dot(xb, w_ref[:, c * ATTN_WIDTH:(c + 1) * ATTN_WIDTH])

    nsh = shbuf.shape[1]
    hm = tm // 2
    split = hm + HALO - 8
    for r, (lo, hi) in enumerate(((0, split), (split, nsh))):
        xh = xb[r * hm:(r + 1) * hm]
        glu = (_dot(xh, w_ref[:, 3 * ATTN_WIDTH:4 * ATTN_WIDTH])
               * _sigmoid(_dot(xh, w_ref[:, 4 * ATTN_WIDTH:5 * ATTN_WIDTH])))
        abuf[HALO + r * hm:HALO + (r + 1) * hm, :] = glu
        for ph in range(1, 8):
            shbuf[ph - 1, lo:hi] = abuf[ph + lo:ph + hi, :]

    def store_pairs(ref, val):
        for p in range(ATTN_WIDTH // PAIR):
            ref[0, p] = val[:, p * PAIR:(p + 1) * PAIR]

    q = proj(0)
    q_gain = qg_ref[...] * (HEAD_DIM ** -0.5 * LOG2E)
    store_pairs(q_ref, (q * lax.rsqrt(_group_mean(q * q, g_ref) + EPS) * q_gain).astype(BF16))
    k = proj(1)
    kn = k * lax.rsqrt(_group_mean(k * k, g_ref) + EPS) * kg_ref[...]
    store_pairs(k_ref, kn.astype(BF16))
    km_ref[0, 0] = jnp.concatenate(
        [jnp.mean(kn[t * MOBA_BLOCK:(t + 1) * MOBA_BLOCK], axis=0, keepdims=True)
         for t in range(tm // MOBA_BLOCK)], axis=0)
    store_pairs(v_ref, proj(2).astype(BF16))

    rc, lc = CONV_TILE
    base = HALO - (CONV_KERNEL - 1)
    for r in range(tm // rc):
        for l in range(CONV_WIDTH // lc):
            cols = slice(l * lc, (l + 1) * lc)
            acc = jnp.broadcast_to(cb_ref[:, cols], (rc // 8, 8, lc))
            for j in range(CONV_KERNEL):
                ph = (base + j) % 8
                row = r * rc + base + j - ph
                tap = abuf[row:row + rc, cols] if ph == 0 else shbuf[ph - 1, row:row + rc, cols]
                acc = acc + cw_ref[j, :, cols] * tap.reshape(rc // 8, 8, lc)
            ybuf[r * rc:(r + 1) * rc, cols] = acc.reshape(rc, lc)
    abuf[0:HALO, :] = abuf[tm:tm + HALO, :]

    y = ybuf[...]
    yc = y - _group_mean(y, g_ref)
    var = _group_mean(yc * yc, g_ref)
    yn = yc * lax.rsqrt(var + EPS) * gng_ref[...] + gnb_ref[...]
    a_ref[0] = (yn * _sigmoid(yn)).astype(BF16)


def _in_proj(x, mod, n1g, w_in, qg, kg, gmat, cw, cb, gng, gnb):
    bsz, s, d = x.shape
    tm = TM_IN
    nt = s // tm
    nblk = tm // MOBA_BLOCK
    tok = lambda w: pl.BlockSpec((1, tm, w), lambda b, i: (b, i, 0))
    npair = ATTN_WIDTH // PAIR
    pairs = pl.BlockSpec((1, npair, tm, PAIR), lambda b, i: (b, 0, i, 0))
    pair_seq = jax.ShapeDtypeStruct((bsz, npair, s, PAIR), BF16)
    return pl.pallas_call(
        _in_kernel,
        grid=(bsz, nt),
        in_specs=[tok(d),
                  pl.BlockSpec((1, 6, d), lambda b, i: (b, 0, 0)),
                  _resident((1, d)),
                  _resident(w_in.shape),
                  _resident((1, ATTN_WIDTH)), _resident((1, ATTN_WIDTH)),
                  _resident(gmat.shape),
                  _resident(cw.shape), _resident((1, CONV_WIDTH)),
                  _resident((1, CONV_WIDTH)), _resident((1, CONV_WIDTH))],
        out_specs=[pairs, pairs, pairs,
                   pl.BlockSpec((1, 1, nblk, ATTN_WIDTH), lambda b, i: (b, i, 0, 0)),
                   tok(CONV_WIDTH)],
        out_shape=[pair_seq, pair_seq, pair_seq,
                   jax.ShapeDtypeStruct((bsz, nt, nblk, ATTN_WIDTH), F32),
                   jax.ShapeDtypeStruct((bsz, s, CONV_WIDTH), BF16)],
        scratch_shapes=[pltpu.VMEM((tm + HALO, CONV_WIDTH), F32),
                        pltpu.VMEM((7, tm + HALO - 8, CONV_WIDTH), F32),
                        pltpu.VMEM((tm, CONV_WIDTH), F32)],
        compiler_params=pltpu.CompilerParams(dimension_semantics=("arbitrary", "arbitrary"),
                                             vmem_limit_bytes=VMEM_LIMIT),
        name="in_proj_conv",
    )(x, mod, n1g, w_in, qg, kg, gmat, cw, cb, gng, gnb)


def _attn_kernel(rb_ref, q_ref, k_ref, v_ref, km_ref, bias_ref, o_ref, vt_ref, s_ref, selb_ref):
    hp = pl.program_id(0)
    s = q_ref.shape[2]
    nb = s // MOBA_BLOCK
    blk = MOBA_BLOCK

    pad_rows = jnp.where(lax.broadcasted_iota(jnp.int32, (V_ROWS - HEAD_DIM, blk), 0) == 0, 1.0, 0.0).astype(BF16)
    for j in range(nb):
        vt = v_ref[0, 0, j * blk:(j + 1) * blk, :].astype(F32).T.astype(BF16)
        for h in range(2):
            vt_ref[h, j, 0:HEAD_DIM, :] = vt[h * HEAD_DIM:(h + 1) * HEAD_DIM, :]
            vt_ref[h, j, HEAD_DIM:V_ROWS, :] = pad_rows

    km = km_ref[0].astype(BF16)
    lane = lax.broadcasted_iota(jnp.int32, (blk, PAIR), 1)
    rows = lax.broadcasted_iota(jnp.int32, (nb, blk), 0).astype(F32)
    far_bias = [rb_ref[2 * hp + h, N_BUCKETS - 1] * LOG2E for h in range(2)]

    order = list(range(0, nb, 2)) + list(range(nb - 1 - nb % 2, 0, -2))
    assert sorted(order) == list(range(nb))
    m_scores = None
    m_values = None
    for g in range(nb + 1):
        do_scores, do_values = g < nb, g >= 1
        t = order[g] if do_scores else None
        tv = order[g - 1] if do_values else None
        sbuf = g % 2
        m_values, m_scores = m_scores, [None, None]

        if do_scores:
            qpair = q_ref[0, 0, t * blk:(t + 1) * blk, :]
            qms = [jnp.where((lane >= HEAD_DIM) == (h == 1), qpair, jnp.zeros_like(qpair)) for h in range(2)]
            gated = t > MOBA_TOPK
            if gated:
                for h in range(2):
                    gate = jnp.where(rows < float(t), _dot_nt(km, qms[h]), -jnp.inf)
                    sel = jnp.zeros(gate.shape, jnp.bool_)
                    for _ in range(MOBA_TOPK):
                        mx = jnp.max(gate, axis=0, keepdims=True)
                        idx = jnp.min(jnp.where(gate == mx, rows, float(nb)), axis=0, keepdims=True)
                        pick = rows == idx
                        sel = sel | pick
                        gate = jnp.where(pick, -jnp.inf, gate)
                    selb_ref[h] = jnp.where(sel, 0.0, NEG)

        def block_start(j):
            return j * blk if isinstance(j, int) else pl.multiple_of(j * blk, blk)

        def scores(h, j0, extras, m):
            n = len(extras)
            st = _dot_nt(k_ref[0, 0, pl.ds(block_start(j0), n * blk), :], qms[h])
            for u, extra in enumerate(extras):
                sb = st[u * blk:(u + 1) * blk] + extra
                s_ref[sbuf, h, j0 + u] = sb
                mb = jnp.max(sb, axis=0, keepdims=True)
                m = mb if m is None else jnp.maximum(m, mb)
            return m

        def far_scores(j0, n, ms):
            return [scores(h, j0, [selb_ref[h, pl.ds(j0 + u, 1), :] + far_bias[h] for u in range(n)], ms[h])
                    for h in range(2)]

        def values(js, accs):
            out = []
            for h in range(2):
                acc = accs[h]
                for j in js:
                    p = jnp.exp2(s_ref[1 - sbuf, h, j] - m_values[h]).astype(BF16)
                    acc = acc + _dot(vt_ref[h, j], p)
                out.append(acc)
            return out

        n_far = max(t + 1 - N_NEAR, 0) if do_scores else 0
        n_values = tv + 1 if do_values else 0
        n_loop = (min(n_far, n_values) if do_scores and do_values else max(n_far, n_values)) // LOOP_BLOCKS
        accs = [jnp.zeros((V_ROWS, blk), F32) for _ in range(2)]

        if do_scores:
            n_near = min(N_NEAR, t + 1)
            for h in range(2):
                extras = []
                for j in range(t - n_near + 1, t + 1):
                    extra = bias_ref[h, t - j]
                    if gated and j < t:
                        extra = extra + selb_ref[h, j:j + 1, :]
                    extras.append(extra)
                m_scores[h] = scores(h, t - n_near + 1, extras, None)

        if n_loop > 0:
            def body(g, carry):
                ms, ac = carry
                if do_scores:
                    ms = far_scores(g * LOOP_BLOCKS, LOOP_BLOCKS, ms)
                if do_values:
                    ac = values([g * LOOP_BLOCKS + u for u in range(LOOP_BLOCKS)], ac)
                return ms, ac

            init_m = m_scores if do_scores else [jnp.zeros((1, blk), F32)] * 2
            ms, accs = lax.fori_loop(0, n_loop, body, (init_m, accs))
            if do_scores:
                m_scores = ms
        done = n_loop * LOOP_BLOCKS
        if n_far > done:
            m_scores = far_scores(done, n_far - done, m_scores)
        if n_values > done:
            accs = values(list(range(done, n_values)), accs)

        if do_values:
            out_t = jnp.concatenate([a[0:HEAD_DIM] / a[HEAD_DIM:HEAD_DIM + 1] for a in accs], axis=0)
            o_ref[0, 0, tv * blk:(tv + 1) * blk, :] = out_t.T.astype(BF16)


def _attention(rel_bias, q, k, v, kmean, bias):
    bsz, npair, s, _ = q.shape
    nb = s // MOBA_BLOCK
    seq = pl.BlockSpec((1, 1, s, PAIR), lambda hp, b: (b, hp, 0, 0))
    return pl.pallas_call(
        _attn_kernel,
        grid=(npair, bsz),
        in_specs=[pl.BlockSpec(memory_space=pltpu.SMEM),
                  seq, seq, seq,
                  pl.BlockSpec((1, nb, PAIR), lambda hp, b: (b, 0, hp)),
                  pl.BlockSpec((2, N_NEAR, MOBA_BLOCK, MOBA_BLOCK), lambda hp, b: (hp, 0, 0, 0))],
        out_specs=seq,
        out_shape=jax.ShapeDtypeStruct((bsz, npair, s, PAIR), BF16),
        scratch_shapes=[pltpu.VMEM((2, nb, V_ROWS, MOBA_BLOCK), BF16),
                        pltpu.VMEM((2, 2, nb, MOBA_BLOCK, MOBA_BLOCK), F32),
                        pltpu.VMEM((2, nb, MOBA_BLOCK), F32)],
        compiler_params=pltpu.CompilerParams(dimension_semantics=("arbitrary", "arbitrary"),
                                             vmem_limit_bytes=VMEM_LIMIT),
        name="moba_attention",
    )(rel_bias, q, k, v, kmean, bias)


def _ffn_kernel(x_ref, ya_ref, a_ref, mod_ref, wo_ref, n2g_ref, wu_ref, fw_ref, fb_ref, wd_ref,
                o_ref, carry_ref, hbuf, x1_ref, xb_ref, act_ref):
    tm = x_ref.shape[1]

    @pl.when(pl.program_id(1) == 0)
    def _():
        carry_ref[...] = jnp.zeros(carry_ref.shape, F32)

    g1 = mod_ref[0, 2:3, :]
    sh2 = mod_ref[0, 3:4, :]
    sc2 = mod_ref[0, 4:5, :]
    g2 = mod_ref[0, 5:6, :]

    halves = [slice(r * (tm // 2), (r + 1) * (tm // 2)) for r in range(2)]
    ycat = jnp.concatenate([ya_ref[0, p] for p in range(ya_ref.shape[1])] + [a_ref[0]], axis=-1)
    y_mix = [_dot(ycat[rows], wo_ref[...]) for rows in halves]
    for rows, ym in zip(halves, y_mix):
        x1 = x_ref[0, rows, :] + g1 * ym
        ms = jnp.mean(x1 * x1, axis=-1, keepdims=True)
        xn = x1 * lax.rsqrt(ms + EPS) * n2g_ref[...]
        x1_ref[rows, :] = x1
        xb_ref[rows, :] = (xn * (1.0 + sc2) + sh2).astype(BF16)

    def stage_in(c0, n, slot, row_parts):
        cols = slice(c0, c0 + n)
        prev = carry_ref[:, cols]
        for t in range(1, FFN_CONV):
            hbuf[slot, t, 0:t, 0:n] = prev[FFN_HALO - t:FFN_HALO]
        for rows in row_parts:
            h = _dot(xb_ref[rows, :], wu_ref[:, cols])
            for t in range(FFN_CONV):
                hbuf[slot, t, pl.ds(t + rows.start, rows.stop - rows.start), 0:n] = h
        carry_ref[:, cols] = h[h.shape[0] - FFN_HALO:, :]

    def conv_out(c0, n, slot):
        cols = slice(c0, c0 + n)
        out = fb_ref[:, cols]
        for t in range(FFN_CONV):
            out = out + fw_ref[FFN_CONV - 1 - t:FFN_CONV - t, cols] * hbuf[slot, t, 0:tm, 0:n]
        return out

    nch = len(FFN_CHUNKS)
    starts = [sum(FFN_CHUNKS[:c]) for c in range(nch)]

    def stage(c):
        row_parts = halves if c == 0 else [slice(0, tm)]
        stage_in(starts[c], FFN_CHUNKS[c], 2 * (c % 2), row_parts)
        stage_in(FFN_HIDDEN + starts[c], FFN_CHUNKS[c], 2 * (c % 2) + 1, row_parts)

    def down(c):
        n = FFN_CHUNKS[c]
        return _dot(act_ref[c % 2, :, 0:n], wd_ref[starts[c]:starts[c] + n, :])

    stage(0)
    acc = jnp.zeros((tm, D_MODEL), F32)
    for c in range(nch):
        if c + 1 < nch:
            stage(c + 1)
        if c >= 1:
            acc = acc + down(c - 1)
        u = conv_out(starts[c], FFN_CHUNKS[c], 2 * (c % 2))
        g = conv_out(FFN_HIDDEN + starts[c], FFN_CHUNKS[c], 2 * (c % 2) + 1)
        act_ref[c % 2, :, 0:FFN_CHUNKS[c]] = (g * _sigmoid(g) * u).astype(BF16)
    acc = acc + down(nch - 1)
    o_ref[0] = x1_ref[...] + g2 * acc


def _out_ffn(x, y_attn, a, mod, w_out, n2g, w_up, fw, fb, w_down):
    bsz, s, d = x.shape
    tm = TM_FFN
    tok = lambda w: pl.BlockSpec((1, tm, w), lambda b, i: (b, i, 0))
    return pl.pallas_call(
        _ffn_kernel,
        grid=(bsz, s // tm),
        in_specs=[tok(d),
                  pl.BlockSpec((1, y_attn.shape[1], tm, PAIR), lambda b, i: (b, 0, i, 0)),
                  tok(CONV_WIDTH),
                  pl.BlockSpec((1, 6, d), lambda b, i: (b, 0, 0)),
                  _resident(w_out.shape), _resident((1, d)),
                  _resident(w_up.shape), _resident(fw.shape), _resident(fb.shape),
                  _resident(w_down.shape)],
        out_specs=tok(d),
        out_shape=jax.ShapeDtypeStruct((bsz, s, d), F32),
        scratch_shapes=[pltpu.VMEM((FFN_HALO, 2 * FFN_HIDDEN), F32),
                        pltpu.VMEM((4, FFN_CONV, tm + FFN_HALO, max(FFN_CHUNKS)), F32),
                        pltpu.VMEM((tm, d), F32),
                        pltpu.VMEM((tm, d), BF16),
                        pltpu.VMEM((2, tm, max(FFN_CHUNKS)), BF16)],
        compiler_params=pltpu.CompilerParams(dimension_semantics=("arbitrary", "arbitrary"),
                                             vmem_limit_bytes=VMEM_LIMIT),
        name="out_proj_ffn",
    )(x, y_attn, a, mod, w_out, n2g, w_up, fw, fb, w_down)


def _group_avg_matrix():
    group = CONV_WIDTH // CONV_GROUPS
    assert group == HEAD_DIM and MXU_TILE % group == 0
    idx = np.arange(MXU_TILE) // group
    return jnp.asarray((idx[:, None] == idx[None, :]).astype(np.float32) / group, dtype=BF16)


def kernel(x, c, rel_bias, ada_w, ada_b, norm1_g, w_in, q_norm_g, k_norm_g, conv_dw_w, conv_dw_b,
           conv_norm_g, conv_norm_b, w_out, norm2_g, w_up, ffn_dw_w, ffn_dw_b, w_down):
    bsz, s, d = x.shape
    depth = ada_w.shape[0]
    bias = _bias_tiles(rel_bias)
    gmat = _group_avg_matrix()
    row = lambda t: t.reshape(1, -1)
    for l in range(depth):
        mod = _modulation(c, ada_w[l], ada_b[l]).reshape(bsz, 6, d)
        q, k, v, kmean, a = _in_proj(
            x, mod, row(norm1_g[l]), w_in[l].astype(BF16),
            row(jnp.tile(q_norm_g[l], N_HEADS)), row(jnp.tile(k_norm_g[l], N_HEADS)), gmat,
            jnp.broadcast_to(conv_dw_w[l][:, None, :], (CONV_KERNEL, 8, CONV_WIDTH)),
            row(conv_dw_b[l]), row(conv_norm_g[l]), row(conv_norm_b[l]))
        kmean = kmean.reshape(bsz, s // MOBA_BLOCK, ATTN_WIDTH)
        y_attn = _attention(rel_bias, q, k, v, kmean, bias)
        x = _out_ffn(x, y_attn, a, mod, w_out[l].astype(BF16), row(norm2_g[l]), w_up[l].astype(BF16),
                     ffn_dw_w[l], row(ffn_dw_b[l]), w_down[l].astype(BF16))
    return x
```

```python
import math

import numpy as np
import jax
import jax.numpy as jnp
from jax import lax
from jax.experimental import pallas as pl
from jax.experimental.pallas import tpu as pltpu

F32 = jnp.float32
BF16 = jnp.bfloat16

D_MODEL = 1024
HEAD_DIM = 64
ATTN_WIDTH = 512
N_HEADS = ATTN_WIDTH // HEAD_DIM
CONV_WIDTH = 512
CONV_GROUPS = 8
CONV_KERNEL = 31
MOBA_BLOCK = 256
MOBA_TOPK = 3
N_BUCKETS = 32
MAX_DISTANCE = 1024
FFN_HIDDEN = 2816
FFN_CONV = 3
EPS = 1e-6
NEG = -1e30
LOG2E = math.log2(math.e)

N_NEAR = 5
PAIR = 2 * HEAD_DIM
V_ROWS = HEAD_DIM + 16
LOOP_BLOCKS = 5
ROLL_MIN_BLOCKS = 11
HALO = 32
FFN_HALO = 8
FFN_CHUNKS = (512, 512, 512, 512, 512, 256)
assert sum(FFN_CHUNKS) == FFN_HIDDEN
TM_IN = 512
TM_FFN = 512
CONV_TILE = (64, 256)
MOD_TILE = 1536
MXU_TILE = 256
VMEM_LIMIT = 56 * 1024 * 1024


def _sigmoid(x):
    return 1.0 / (1.0 + jnp.exp(-x))


def _dot(a, b):
    return jnp.dot(a, b, preferred_element_type=F32)


def _dot_nt(a, b):
    return lax.dot_general(a, b, (((1,), (1,)), ((), ())), preferred_element_type=F32)


def _resident(shape):
    zeros = (0,) * len(shape)
    return pl.BlockSpec(shape, lambda *_: zeros, pipeline_mode=pl.Buffered(1))


def _mod_kernel(c_ref, w_ref, b_ref, o_ref):
    c = c_ref[...]
    sc = c * _sigmoid(c)
    o_ref[...] = _dot(sc.astype(BF16), w_ref[...].astype(BF16)) + b_ref[...]


def _modulation(c, ada_w, ada_b):
    bsz, d = c.shape
    n = ada_w.shape[1]
    tn = MOD_TILE
    return pl.pallas_call(
        _mod_kernel,
        grid=(n // tn,),
        in_specs=[pl.BlockSpec((bsz, d), lambda j: (0, 0)),
                  pl.BlockSpec((d, tn), lambda j: (0, j)),
                  pl.BlockSpec((1, tn), lambda j: (0, j))],
        out_specs=pl.BlockSpec((bsz, tn), lambda j: (0, j)),
        out_shape=jax.ShapeDtypeStruct((bsz, n), F32),
        compiler_params=pltpu.CompilerParams(dimension_semantics=("arbitrary",),
                                             vmem_limit_bytes=VMEM_LIMIT),
        name="adaln_mod",
    )(c, ada_w, ada_b.reshape(1, n))


def _bucket_range(delta):
    dist = np.arange(delta * MOBA_BLOCK - (MOBA_BLOCK - 1), delta * MOBA_BLOCK + MOBA_BLOCK)
    n = np.maximum(dist, 0)
    max_exact = N_BUCKETS // 2
    large = max_exact + (np.log(np.maximum(n, 1) / max_exact) / math.log(MAX_DISTANCE / max_exact)
                         * (N_BUCKETS - max_exact)).astype(np.int64)
    bucket = np.where(n < max_exact, n, np.minimum(large, N_BUCKETS - 1))
    return max(int(bucket.min()) - 1, 0), min(int(bucket.max()) + 1, N_BUCKETS - 1)


def _bias_kernel(rb_ref, o_ref):
    h = pl.program_id(0)
    blk = MOBA_BLOCK
    kk = lax.broadcasted_iota(jnp.int32, (blk, blk), 0)
    qq = lax.broadcasted_iota(jnp.int32, (blk, blk), 1)
    col = lax.broadcasted_iota(jnp.int32, (8, 2 * blk), 1)
    max_exact = N_BUCKETS // 2
    for delta in range(N_NEAR):
        dist = delta * blk + col - (blk - 1)
        n = jnp.maximum(dist, 0)
        nf = jnp.maximum(n, 1).astype(F32)
        large = max_exact + (jnp.log(nf / max_exact) / math.log(MAX_DISTANCE / max_exact)
                             * (N_BUCKETS - max_exact)).astype(jnp.int32)
        large = jnp.minimum(large, N_BUCKETS - 1)
        bucket = jnp.where(n < max_exact, n, large)
        lo, hi = _bucket_range(delta)
        line = jnp.full(col.shape, rb_ref[h, lo], F32)
        for b in range(lo + 1, hi + 1):
            line = jnp.where(bucket >= b, rb_ref[h, b], line)
        lines = jnp.broadcast_to(line[0:1, :], (blk, 2 * blk))
        bias = pltpu.roll(lines, blk + 1, 1, stride=1, stride_axis=0)[:, 0:blk]
        if delta == 0:
            bias = jnp.where(qq >= kk, bias, NEG)
        o_ref[0, delta] = bias * LOG2E


def _bias_tiles(rel_bias):
    return pl.pallas_call(
        _bias_kernel,
        grid=(N_HEADS,),
        in_specs=[pl.BlockSpec(memory_space=pltpu.SMEM)],
        out_specs=pl.BlockSpec((1, N_NEAR, MOBA_BLOCK, MOBA_BLOCK), lambda h: (h, 0, 0, 0)),
        out_shape=jax.ShapeDtypeStruct((N_HEADS, N_NEAR, MOBA_BLOCK, MOBA_BLOCK), F32),
        compiler_params=pltpu.CompilerParams(dimension_semantics=("arbitrary",)),
        name="rel_bias_tiles",
    )(rel_bias)


def _group_mean(v, g_ref):
    vb = v.astype(BF16)
    g = g_ref[...]
    half = g.shape[0]
    parts = [_dot(vb[:, s:s + half], g) for s in range(0, v.shape[1], half)]
    return jnp.concatenate(parts, axis=-1)


def _in_kernel(x_ref, mod_ref, n1g_ref, w_ref, qg_ref, kg_ref, g_ref, cw_ref, cb_ref, gng_ref, gnb_ref,
               q_ref, k_ref, v_ref, km_ref, a_ref, abuf, shbuf, ybuf):
    tm = x_ref.shape[1]

    @pl.when(pl.program_id(1) == 0)
    def _():
        abuf[0:HALO, :] = jnp.zeros((HALO, CONV_WIDTH), F32)

    x = x_ref[0]
    ms = jnp.mean(x * x, axis=-1, keepdims=True)
    gain = n1g_ref[...] * (1.0 + mod_ref[0, 1:2, :])
    xb = (x * lax.rsqrt(ms + EPS) * gain + mod_ref[0, 0:1, :]).astype(BF16)

    def proj(c):
        return _dot(xb, w_ref[:, c * ATTN_WIDTH:(c + 1) * ATTN_WIDTH])

    nsh = shbuf.shape[1]
    hm = tm // 2
    split = hm + HALO - 8
    for r, (lo, hi) in enumerate(((0, split), (split, nsh))):
        xh = xb[r * hm:(r + 1) * hm]
        glu = (_dot(xh, w_ref[:, 3 * ATTN_WIDTH:4 * ATTN_WIDTH])
               * _sigmoid(_dot(xh, w_ref[:, 4 * ATTN_WIDTH:5 * ATTN_WIDTH])))
        abuf[HALO + r * hm:HALO + (r + 1) * hm, :] = glu
        for ph in range(1, 8):
            shbuf[ph - 1, lo:hi] = abuf[ph + lo:ph + hi, :]

    def store_pairs(ref, val):
        for p in range(ATTN_WIDTH // PAIR):
            ref[0, p] = val[:, p * PAIR:(p + 1) * PAIR]

    q = proj(0)
    q_gain = qg_ref[...] * (HEAD_DIM ** -0.5 * LOG2E)
    store_pairs(q_ref, (q * lax.rsqrt(_group_mean(q * q, g_ref) + EPS) * q_gain).astype(BF16))
    k = proj(1)
    kn = k * lax.rsqrt(_group_mean(k * k, g_ref) + EPS) * kg_ref[...]
    store_pairs(k_ref, kn.astype(BF16))
    km_ref[0, 0] = jnp.concatenate(
        [jnp.mean(kn[t * MOBA_BLOCK:(t + 1) * MOBA_BLOCK], axis=0, keepdims=True)
         for t in range(tm // MOBA_BLOCK)], axis=0)
    store_pairs(v_ref, proj(2).astype(BF16))

    rc, lc = CONV_TILE
    base = HALO - (CONV_KERNEL - 1)
    for r in range(tm // rc):
        for l in range(CONV_WIDTH // lc):
            cols = slice(l * lc, (l + 1) * lc)
            acc = jnp.broadcast_to(cb_ref[:, cols], (rc // 8, 8, lc))
            for j in range(CONV_KERNEL):
                ph = (base + j) % 8
                row = r * rc + base + j - ph
                tap = abuf[row:row + rc, cols] if ph == 0 else shbuf[ph - 1, row:row + rc, cols]
                acc = acc + cw_ref[j, :, cols] * tap.reshape(rc // 8, 8, lc)
            ybuf[r * rc:(r + 1) * rc, cols] = acc.reshape(rc, lc)
    abuf[0:HALO, :] = abuf[tm:tm + HALO, :]

    y = ybuf[...]
    yc = y - _group_mean(y, g_ref)
    var = _group_mean(yc * yc, g_ref)
    yn = yc * lax.rsqrt(var + EPS) * gng_ref[...] + gnb_ref[...]
    a_ref[0] = (yn * _sigmoid(yn)).astype(BF16)


def _in_proj(x, mod, n1g, w_in, qg, kg, gmat, cw, cb, gng, gnb):
    bsz, s, d = x.shape
    tm = TM_IN
    nt = s // tm
    nblk = tm // MOBA_BLOCK
    tok = lambda w: pl.BlockSpec((1, tm, w), lambda b, i: (b, i, 0))
    npair = ATTN_WIDTH // PAIR
    pairs = pl.BlockSpec((1, npair, tm, PAIR), lambda b, i: (b, 0, i, 0))
    pair_seq = jax.ShapeDtypeStruct((bsz, npair, s, PAIR), BF16)
    return pl.pallas_call(
        _in_kernel,
        grid=(bsz, nt),
        in_specs=[tok(d),
                  pl.BlockSpec((1, 6, d), lambda b, i: (b, 0, 0)),
                  _resident((1, d)),
                  _resident(w_in.shape),
                  _resident((1, ATTN_WIDTH)), _resident((1, ATTN_WIDTH)),
                  _resident(gmat.shape),
                  _resident(cw.shape), _resident((1, CONV_WIDTH)),
                  _resident((1, CONV_WIDTH)), _resident((1, CONV_WIDTH))],
        out_specs=[pairs, pairs, pairs,
                   pl.BlockSpec((1, 1, nblk, ATTN_WIDTH), lambda b, i: (b, i, 0, 0)),
                   tok(CONV_WIDTH)],
        out_shape=[pair_seq, pair_seq, pair_seq,
                   jax.ShapeDtypeStruct((bsz, nt, nblk, ATTN_WIDTH), F32),
                   jax.ShapeDtypeStruct((bsz, s, CONV_WIDTH), BF16)],
        scratch_shapes=[pltpu.VMEM((tm + HALO, CONV_WIDTH), F32),
                        pltpu.VMEM((7, tm + HALO - 8, CONV_WIDTH), F32),
                        pltpu.VMEM((tm, CONV_WIDTH), F32)],
        compiler_params=pltpu.CompilerParams(dimension_semantics=("arbitrary", "arbitrary"),
                                             vmem_limit_bytes=VMEM_LIMIT),
        name="in_proj_conv",
    )(x, mod, n1g, w_in, qg, kg, gmat, cw, cb, gng, gnb)


def _attn_kernel(rb_ref, q_ref, k_ref, v_ref, km_ref, bias_ref, o_ref, vt_ref, s_ref, selb_ref):
    hp = pl.program_id(0)
    s = q_ref.shape[2]
    nb = s // MOBA_BLOCK
    blk = MOBA_BLOCK

    pad_rows = jnp.where(lax.broadcasted_iota(jnp.int32, (V_ROWS - HEAD_DIM, blk), 0) == 0, 1.0, 0.0).astype(BF16)
    for j in range(nb):
        vt = v_ref[0, 0, j * blk:(j + 1) * blk, :].astype(F32).T.astype(BF16)
        for h in range(2):
            vt_ref[h, j, 0:HEAD_DIM, :] = vt[h * HEAD_DIM:(h + 1) * HEAD_DIM, :]
            vt_ref[h, j, HEAD_DIM:V_ROWS, :] = pad_rows

    km = km_ref[0].astype(BF16)
    lane = lax.broadcasted_iota(jnp.int32, (blk, PAIR), 1)
    rows = lax.broadcasted_iota(jnp.int32, (nb, blk), 0).astype(F32)
    far_bias = [rb_ref[2 * hp + h, N_BUCKETS - 1] * LOG2E for h in range(2)]

    order = list(range(0, nb, 2)) + list(range(nb - 1 - nb % 2, 0, -2))
    assert sorted(order) == list(range(nb))
    m_scores = None
    m_values = None
    for g in range(nb + 1):
        do_scores, do_values = g < nb, g >= 1
        t = order[g] if do_scores else None
        tv = order[g - 1] if do_values else None
        sbuf = g % 2
        m_values, m_scores = m_scores, [None, None]

        if do_scores:
            qpair = q_ref[0, 0, t * blk:(t + 1) * blk, :]
            qms = [jnp.where((lane >= HEAD_DIM) == (h == 1), qpair, jnp.zeros_like(qpair)) for h in range(2)]
            gated = t > MOBA_TOPK
            if gated:
                for h in range(2):
                    gate = jnp.where(rows < float(t), _dot_nt(km, qms[h]), -jnp.inf)
                    sel = jnp.zeros(gate.shape, jnp.bool_)
                    for _ in range(MOBA_TOPK):
                        mx = jnp.max(gate, axis=0, keepdims=True)
                        idx = jnp.min(jnp.where(gate == mx, rows, float(nb)), axis=0, keepdims=True)
                        pick = rows == idx
                        sel = sel | pick
                        gate = jnp.where(pick, -jnp.inf, gate)
                    selb_ref[h] = jnp.where(sel, 0.0, NEG)

        def block_start(j):
            return j * blk if isinstance(j, int) else pl.multiple_of(j * blk, blk)

        def scores(h, j0, extras, m):
            n = len(extras)
            st = _dot_nt(k_ref[0, 0, pl.ds(block_start(j0), n * blk), :], qms[h])
            for u, extra in enumerate(extras):
                sb = st[u * blk:(u + 1) * blk] + extra
                s_ref[sbuf, h, j0 + u] = sb
                mb = jnp.max(sb, axis=0, keepdims=True)
                m = mb if m is None else jnp.maximum(m, mb)
            return m

        def far_scores(j0, n, ms):
            return [scores(h, j0, [selb_ref[h, pl.ds(j0 + u, 1), :] + far_bias[h] for u in range(n)], ms[h])
                    for h in range(2)]

        def values(js, accs):
            out = []
            for h in range(2):
                acc = accs[h]
                for j in js:
                    p = jnp.exp2(s_ref[1 - sbuf, h, j] - m_values[h]).astype(BF16)
                    acc = acc + _dot(vt_ref[h, j], p)
                out.append(acc)
            return out

        n_far = max(t + 1 - N_NEAR, 0) if do_scores else 0
        n_values = tv + 1 if do_values else 0
        common = min(n_far, n_values) if do_scores and do_values else max(n_far, n_values)
        n_loop = common // LOOP_BLOCKS if common >= ROLL_MIN_BLOCKS else 0
        accs = [jnp.zeros((V_ROWS, blk), F32) for _ in range(2)]

        if do_scores:
            n_near = min(N_NEAR, t + 1)
            for h in range(2):
                extras = []
                for j in range(t - n_near + 1, t + 1):
                    extra = bias_ref[h, t - j]
                    if gated and j < t:
                        extra = extra + selb_ref[h, j:j + 1, :]
                    extras.append(extra)
                m_scores[h] = scores(h, t - n_near + 1, extras, None)

        if n_loop > 0:
            def body(g, carry):
                ms, ac = carry
                if do_scores:
                    ms = far_scores(g * LOOP_BLOCKS, LOOP_BLOCKS, ms)
                if do_values:
                    ac = values([g * LOOP_BLOCKS + u for u in range(LOOP_BLOCKS)], ac)
                return ms, ac

            init_m = m_scores if do_scores else [jnp.zeros((1, blk), F32)] * 2
            ms, accs = lax.fori_loop(0, n_loop, body, (init_m, accs))
            if do_scores:
                m_scores = ms
        done = n_loop * LOOP_BLOCKS
        if n_far > done:
            m_scores = far_scores(done, n_far - done, m_scores)
        if n_values > done:
            accs = values(list(range(done, n_values)), accs)

        if do_values:
            out_t = jnp.concatenate([a[0:HEAD_DIM] / a[HEAD_DIM:HEAD_DIM + 1] for a in accs], axis=0)
            o_ref[0, 0, tv * blk:(tv + 1) * blk, :] = out_t.T.astype(BF16)


def _attention(rel_bias, q, k, v, kmean, bias):
    bsz, npair, s, _ = q.shape
    nb = s // MOBA_BLOCK
    seq = pl.BlockSpec((1, 1, s, PAIR), lambda hp, b: (b, hp, 0, 0))
    return pl.pallas_call(
        _attn_kernel,
        grid=(npair, bsz),
        in_specs=[pl.BlockSpec(memory_space=pltpu.SMEM),
                  seq, seq, seq,
                  pl.BlockSpec((1, nb, PAIR), lambda hp, b: (b, 0, hp)),
                  pl.BlockSpec((2, N_NEAR, MOBA_BLOCK, MOBA_BLOCK), lambda hp, b: (hp, 0, 0, 0))],
        out_specs=seq,
        out_shape=jax.ShapeDtypeStruct((bsz, npair, s, PAIR), BF16),
        scratch_shapes=[pltpu.VMEM((2, nb, V_ROWS, MOBA_BLOCK), BF16),
                        pltpu.VMEM((2, 2, nb, MOBA_BLOCK, MOBA_BLOCK), F32),
                        pltpu.VMEM((2, nb, MOBA_BLOCK), F32)],
        compiler_params=pltpu.CompilerParams(dimension_semantics=("arbitrary", "arbitrary"),
                                             vmem_limit_bytes=VMEM_LIMIT),
        name="moba_attention",
    )(rel_bias, q, k, v, kmean, bias)


def _ffn_kernel(x_ref, ya_ref, a_ref, mod_ref, wo_ref, n2g_ref, wu_ref, fw_ref, fb_ref, wd_ref,
                o_ref, carry_ref, hbuf, x1_ref, xb_ref, act_ref):
    tm = x_ref.shape[1]

    @pl.when(pl.program_id(1) == 0)
    def _():
        carry_ref[...] = jnp.zeros(carry_ref.shape, F32)

    g1 = mod_ref[0, 2:3, :]
    sh2 = mod_ref[0, 3:4, :]
    sc2 = mod_ref[0, 4:5, :]
    g2 = mod_ref[0, 5:6, :]

    halves = [slice(r * (tm // 2), (r + 1) * (tm // 2)) for r in range(2)]
    ycat = jnp.concatenate([ya_ref[0, p] for p in range(ya_ref.shape[1])] + [a_ref[0]], axis=-1)
    y_mix = [_dot(ycat[rows], wo_ref[...]) for rows in halves]
    for rows, ym in zip(halves, y_mix):
        x1 = x_ref[0, rows, :] + g1 * ym
        ms = jnp.mean(x1 * x1, axis=-1, keepdims=True)
        xn = x1 * lax.rsqrt(ms + EPS) * n2g_ref[...]
        x1_ref[rows, :] = x1
        xb_ref[rows, :] = (xn * (1.0 + sc2) + sh2).astype(BF16)

    def stage_in(c0, n, slot, row_parts):
        cols = slice(c0, c0 + n)
        prev = carry_ref[:, cols]
        for t in range(1, FFN_CONV):
            hbuf[slot, t, 0:t, 0:n] = prev[FFN_HALO - t:FFN_HALO]
        for rows in row_parts:
            h = _dot(xb_ref[rows, :], wu_ref[:, cols])
            for t in range(FFN_CONV):
                hbuf[slot, t, pl.ds(t + rows.start, rows.stop - rows.start), 0:n] = h
        carry_ref[:, cols] = h[h.shape[0] - FFN_HALO:, :]

    def conv_out(c0, n, slot):
        cols = slice(c0, c0 + n)
        out = fb_ref[:, cols]
        for t in range(FFN_CONV):
            out = out + fw_ref[FFN_CONV - 1 - t:FFN_CONV - t, cols] * hbuf[slot, t, 0:tm, 0:n]
        return out

    nch = len(FFN_CHUNKS)
    starts = [sum(FFN_CHUNKS[:c]) for c in range(nch)]

    def stage(c):
        row_parts = halves if c == 0 else [slice(0, tm)]
        stage_in(starts[c], FFN_CHUNKS[c], 2 * (c % 2), row_parts)
        stage_in(FFN_HIDDEN + starts[c], FFN_CHUNKS[c], 2 * (c % 2) + 1, row_parts)

    def down(c):
        n = FFN_CHUNKS[c]
        return _dot(act_ref[c % 2, :, 0:n], wd_ref[starts[c]:starts[c] + n, :])

    stage(0)
    acc = jnp.zeros((tm, D_MODEL), F32)
    for c in range(nch):
        if c + 1 < nch:
            stage(c + 1)
        if c >= 1:
            acc = acc + down(c - 1)
        u = conv_out(starts[c], FFN_CHUNKS[c], 2 * (c % 2))
        g = conv_out(FFN_HIDDEN + starts[c], FFN_CHUNKS[c], 2 * (c % 2) + 1)
        act_ref[c % 2, :, 0:FFN_CHUNKS[c]] = (g * _sigmoid(g) * u).astype(BF16)
    acc = acc + down(nch - 1)
    o_ref[0] = x1_ref[...] + g2 * acc


def _out_ffn(x, y_attn, a, mod, w_out, n2g, w_up, fw, fb, w_down):
    bsz, s, d = x.shape
    tm = TM_FFN
    tok = lambda w: pl.BlockSpec((1, tm, w), lambda b, i: (b, i, 0))
    return pl.pallas_call(
        _ffn_kernel,
        grid=(bsz, s // tm),
        in_specs=[tok(d),
                  pl.BlockSpec((1, y_attn.shape[1], tm, PAIR), lambda b, i: (b, 0, i, 0)),
                  tok(CONV_WIDTH),
                  pl.BlockSpec((1, 6, d), lambda b, i: (b, 0, 0)),
                  _resident(w_out.shape), _resident((1, d)),
                  _resident(w_up.shape), _resident(fw.shape), _resident(fb.shape),
                  _resident(w_down.shape)],
        out_specs=tok(d),
        out_shape=jax.ShapeDtypeStruct((bsz, s, d), F32),
        scratch_shapes=[pltpu.VMEM((FFN_HALO, 2 * FFN_HIDDEN), F32),
                        pltpu.VMEM((4, FFN_CONV, tm + FFN_HALO, max(FFN_CHUNKS)), F32),
                        pltpu.VMEM((tm, d), F32),
                        pltpu.VMEM((tm, d), BF16),
                        pltpu.VMEM((2, tm, max(FFN_CHUNKS)), BF16)],
        compiler_params=pltpu.CompilerParams(dimension_semantics=("arbitrary", "arbitrary"),
                                             vmem_limit_bytes=VMEM_LIMIT),
        name="out_proj_ffn",
    )(x, y_attn, a, mod, w_out, n2g, w_up, fw, fb, w_down)


def _group_avg_matrix():
    group = CONV_WIDTH // CONV_GROUPS
    assert group == HEAD_DIM and MXU_TILE % group == 0
    idx = np.arange(MXU_TILE) // group
    return jnp.asarray((idx[:, None] == idx[None, :]).astype(np.float32) / group, dtype=BF16)


def kernel(x, c, rel_bias, ada_w, ada_b, norm1_g, w_in, q_norm_g, k_norm_g, conv_dw_w, conv_dw_b,
           conv_norm_g, conv_norm_b, w_out, norm2_g, w_up, ffn_dw_w, ffn_dw_b, w_down):
    bsz, s, d = x.shape
    depth = ada_w.shape[0]
    bias = _bias_tiles(rel_bias)
    gmat = _group_avg_matrix()
    row = lambda t: t.reshape(1, -1)
    for l in range(depth):
        mod = _modulation(c, ada_w[l], ada_b[l]).reshape(bsz, 6, d)
        q, k, v, kmean, a = _in_proj(
            x, mod, row(norm1_g[l]), w_in[l].astype(BF16),
            row(jnp.tile(q_norm_g[l], N_HEADS)), row(jnp.tile(k_norm_g[l], N_HEADS)), gmat,
            jnp.broadcast_to(conv_dw_w[l][:, None, :], (CONV_KERNEL, 8, CONV_WIDTH)),
            row(conv_dw_b[l]), row(conv_norm_g[l]), row(conv_norm_b[l]))
        kmean = kmean.reshape(bsz, s // MOBA_BLOCK, ATTN_WIDTH)
        y_attn = _attention(rel_bias, q, k, v, kmean, bias)
        x = _out_ffn(x, y_attn, a, mod, w_out[l].astype(BF16), row(norm2_g[l]), w_up[l].astype(BF16),
                     ffn_dw_w[l], row(ffn_dw_b[l]), w_down[l].astype(BF16))
    return x
```
